```python
import math
import jax, jax.numpy as jnp
from jax import lax
import numpy as np

D_MODEL = 2048
BATCH = 2
SEQ = 16384
DEPTH = 2

N_MIXERS = 2
N_NSA_LAYERS = (DEPTH + N_MIXERS - 1) // N_MIXERS
N_MLA_LAYERS = DEPTH // N_MIXERS
D_FF = 5632
RMS_EPS = 1e-6
ROPE_THETA = 10000.0
NEG_INF = -1e30
FORCE_BONUS = 1e9

NSA_HEADS = 16
NSA_HEAD_DIM = 128
NSA_KV_GROUPS = 4
NSA_HEADS_PER_GROUP = NSA_HEADS // NSA_KV_GROUPS
NSA_N_BRANCHES = 3
NSA_CMP_LEN = 32
NSA_CMP_STRIDE = 16
NSA_CMP_HIDDEN = 256
NSA_SEL_BLOCK = 64
NSA_N_SELECT = 16
NSA_WINDOW = 512
NSA_Q_BLOCK = 64
NSA_Q_COLS = NSA_HEADS * NSA_HEAD_DIM
NSA_KV_COLS = NSA_N_BRANCHES * 2 * NSA_KV_GROUPS * NSA_HEAD_DIM
NSA_GATE_COLS = NSA_N_BRANCHES * NSA_HEADS
NSA_IN_COLS = NSA_Q_COLS + NSA_KV_COLS + NSA_GATE_COLS

MLA_HEADS = 16
MLA_Q_RANK = 512
MLA_KV_RANK = 512
MLA_NOPE_DIM = 128
MLA_ROPE_DIM = 64
MLA_V_DIM = 128
MLA_QK_DIM = MLA_NOPE_DIM + MLA_ROPE_DIM
MLA_IN_COLS = MLA_Q_RANK + MLA_KV_RANK + MLA_ROPE_DIM
MLA_Q_BLOCK = 128

kernel_name = 'hybrid_nsa_mla_macaron'


def rms_norm(x, w):
    x32 = x.astype(jnp.float32)
    y = x32 * lax.rsqrt(jnp.mean(x32 * x32, axis=-1, keepdims=True) + RMS_EPS)
    return (y * w.astype(jnp.float32)).astype(x.dtype)


def swiglu(h, w_in, w_out):
    g, u = jnp.split(h @ w_in, 2, axis=-1)
    return (jax.nn.silu(g) * u) @ w_out


def rope_tables(pos, dim):
    inv = 1.0 / (ROPE_THETA ** (jnp.arange(0, dim, 2, dtype=jnp.float32) / dim))
    ang = pos.astype(jnp.float32)[..., None] * inv
    return jnp.cos(ang), jnp.sin(ang)


def apply_rope(x, cos, sin):
    c = cos[:, :, None, :]
    s = sin[:, :, None, :]
    x1, x2 = jnp.split(x.astype(jnp.float32), 2, axis=-1)
    return jnp.concatenate([x1 * c - x2 * s, x2 * c + x1 * s], axis=-1).astype(x.dtype)


def nsa_mixer(h, positions, w_in, cmp_pe, cmp_w1, cmp_w2, w_out):
    B, S, _ = h.shape
    H, G, R, dk = NSA_HEADS, NSA_KV_GROUPS, NSA_HEADS_PER_GROUP, NSA_HEAD_DIM
    QB = NSA_Q_BLOCK
    dt = h.dtype
    scale = dk ** -0.5
    proj = h @ w_in
    q = proj[..., :NSA_Q_COLS].reshape(B, S, H, dk)
    kv = proj[..., NSA_Q_COLS:NSA_Q_COLS + NSA_KV_COLS].reshape(B, S, NSA_N_BRANCHES, 2, G, dk)
    gates = jax.nn.sigmoid(proj[..., NSA_Q_COLS + NSA_KV_COLS:].astype(jnp.float32)).reshape(B, S, G, R, NSA_N_BRANCHES)
    cos, sin = rope_tables(positions, dk)
    q = apply_rope(q, cos, sin).reshape(B, S, G, R, dk)
    k_cmp_tok, v_cmp_tok = kv[:, :, 0, 0], kv[:, :, 0, 1]
    k_slc, v_slc = apply_rope(kv[:, :, 1, 0], cos, sin), kv[:, :, 1, 1]
    k_win, v_win = apply_rope(kv[:, :, 2, 0], cos, sin), kv[:, :, 2, 1]

    n_cmp = (S - NSA_CMP_LEN) // NSA_CMP_STRIDE + 1
    blk_start = jnp.arange(n_cmp) * NSA_CMP_STRIDE
    cmp_idx = blk_start[:, None] + jnp.arange(NSA_CMP_LEN)[None, :]
    cmp_end = blk_start + NSA_CMP_LEN - 1

    def compress(tok, pe, w1, w2):
        blocks = tok[:, cmp_idx] + pe[None, None, :, None, :]
        blocks = blocks.transpose(0, 1, 3, 2, 4).reshape(B, n_cmp, G, NSA_CMP_LEN * dk)
        return jax.nn.gelu(blocks @ w1) @ w2

    k_cmp = compress(k_cmp_tok, cmp_pe[0], cmp_w1[0], cmp_w2[0])
    v_cmp = compress(v_cmp_tok, cmp_pe[1], cmp_w1[1], cmp_w2[1])
    cos_c, sin_c = rope_tables(positions[:, cmp_end], dk)
    k_cmp = apply_rope(k_cmp, cos_c, sin_c)

    n_sel_blk = S // NSA_SEL_BLOCK
    n_top = min(NSA_N_SELECT, n_sel_blk)
    sel_start = jnp.arange(n_sel_blk) * NSA_SEL_BLOCK
    overlap = ((blk_start[:, None] <= sel_start[None, :] + NSA_SEL_BLOCK - 1)
               & (cmp_end[:, None] >= sel_start[None, :])).astype(jnp.float32)
    k_slc_blk = k_slc.reshape(B, n_sel_blk, NSA_SEL_BLOCK, G, dk).transpose(0, 3, 1, 2, 4)
    v_slc_blk = v_slc.reshape(B, n_sel_blk, NSA_SEL_BLOCK, G, dk).transpose(0, 3, 1, 2, 4)
    b_ix = jnp.arange(B)[:, None, None, None]
    g_ix = jnp.arange(G)[None, None, :, None]
    blk_off = jnp.arange(NSA_SEL_BLOCK)
    sel_ids = jnp.arange(n_sel_blk)

    pad = ((0, 0), (NSA_WINDOW, 0), (0, 0), (0, 0))
    k_win_p = jnp.pad(k_win, pad)
    v_win_p = jnp.pad(v_win, pad)
    win_len = NSA_WINDOW + QB
    win_off = jnp.arange(win_len) - NSA_WINDOW

    def block(start):
        t = start + jnp.arange(QB)
        qb = lax.dynamic_slice_in_dim(q, start, QB, axis=1)
        gb = lax.dynamic_slice_in_dim(gates, start, QB, axis=1)
        s_c = jnp.einsum('bqgrd,bngd->bqgrn', qb, k_cmp).astype(jnp.float32) * scale
        vc = (cmp_end[None, :] <= t[:, None])[None, :, None, None, :]
        p_c = jax.nn.softmax(jnp.where(vc, s_c, NEG_INF), axis=-1) * vc
        o_c = jnp.einsum('bqgrn,bngd->bqgrd', p_c.astype(dt), v_cmp)
        imp = jnp.einsum('bqgrn,nj->bqgj', p_c, overlap)
        cur = (t // NSA_SEL_BLOCK)[:, None]
        j = sel_ids[None, :]
        forced = ((j == 0) | (j == cur) | (j == cur - 1)).astype(jnp.float32)
        future = j > cur
        imp = jnp.where(future[None, :, None, :], NEG_INF, imp + FORCE_BONUS * forced[None, :, None, :])
        top_val, top_idx = lax.top_k(imp, n_top)
        ks = k_slc_blk[b_ix, g_ix, top_idx]
        vs = v_slc_blk[b_ix, g_ix, top_idx]
        s_s = jnp.einsum('bqgrd,bqgnkd->bqgrnk', qb, ks).astype(jnp.float32) * scale
        tok = top_idx[..., None] * NSA_SEL_BLOCK + blk_off
        m_s = (top_val > 0.5 * NEG_INF)[..., None] & (tok <= t[None, :, None, None, None])
        s_s = jnp.where(m_s[:, :, :, None], s_s, NEG_INF).reshape(B, QB, G, R, n_top * NSA_SEL_BLOCK)
        p_s = jax.nn.softmax(s_s, axis=-1).reshape(B, QB, G, R, n_top, NSA_SEL_BLOCK)
        o_s = jnp.einsum('bqgrnk,bqgnkd->bqgrd', p_s.astype(dt), vs)
        kw = lax.dynamic_slice_in_dim(k_win_p, start, win_len, axis=1)
        vw = lax.dynamic_slice_in_dim(v_win_p, start, win_len, axis=1)
        kpos = start + win_off
        m_w = (kpos[None, :] <= t[:, None]) & (t[:, None] - kpos[None, :] < NSA_WINDOW) & (kpos[None, :] >= 0)
        s_w = jnp.einsum('bqgrd,bkgd->bqgrk', qb, kw).astype(jnp.float32) * scale
        p_w = jax.nn.softmax(jnp.where(m_w[None, :, None, None, :], s_w, NEG_INF), axis=-1)
        o_w = jnp.einsum('bqgrk,bkgd->bqgrd', p_w.astype(dt), vw)
        o = gb[..., 0:1] * o_c + gb[..., 1:2] * o_s + gb[..., 2:3] * o_w
        return o.astype(dt).reshape(B, QB, H * dk)

    starts = jnp.arange(S // QB) * QB
    out = lax.map(block, starts).transpose(1, 0, 2, 3).reshape(B, S, H * dk)
    return out @ w_out


def mla_mixer(h, positions, w_in, q_norm_w, kv_norm_w, w_uq, w_ukv, w_out):
    B, S, _ = h.shape
    H, QB = MLA_HEADS, MLA_Q_BLOCK
    dt = h.dtype
    proj = h @ w_in
    c_q, c_kv, k_r = jnp.split(proj, [MLA_Q_RANK, MLA_Q_RANK + MLA_KV_RANK], axis=-1)
    q = (rms_norm(c_q, q_norm_w) @ w_uq).reshape(B, S, H, MLA_QK_DIM)
    kv = (rms_norm(c_kv, kv_norm_w) @ w_ukv).reshape(B, S, H, MLA_NOPE_DIM + MLA_V_DIM)
    cos, sin = rope_tables(positions, MLA_ROPE_DIM)
    q_nope, q_rope = jnp.split(q, [MLA_NOPE_DIM], axis=-1)
    q_rope = apply_rope(q_rope, cos, sin)
    k_nope, v = jnp.split(kv, [MLA_NOPE_DIM], axis=-1)
    k_rope = apply_rope(k_r[:, :, None, :], cos, sin)[:, :, 0]
    scale = MLA_QK_DIM ** -0.5
    k_pos = jnp.arange(S)

    def block(start):
        t = start + jnp.arange(QB)
        qn = lax.dynamic_slice_in_dim(q_nope, start, QB, axis=1)
        qr = lax.dynamic_slice_in_dim(q_rope, start, QB, axis=1)
        s = (jnp.einsum('bqhd,bkhd->bhqk', qn, k_nope)
             + jnp.einsum('bqhd,bkd->bhqk', qr, k_rope)).astype(jnp.float32) * scale
        mask = k_pos[None, :] <= t[:, None]
        p = jax.nn.softmax(jnp.where(mask[None, None], s, NEG_INF), axis=-1)
        return jnp.einsum('bhqk,bkhd->bqhd', p.astype(dt), v).reshape(B, QB, H * MLA_V_DIM)

    starts = jnp.arange(S // QB) * QB
    out = lax.map(block, starts).transpose(1, 0, 2, 3).reshape(B, S, H * MLA_V_DIM)
    return out @ w_out


def setup_inputs(seed: int = 0) -> dict:
    key = jax.random.key(seed)
    ks = jax.random.split(key, 18)

    def normal(k, shape, fan_in):
        return jax.random.normal(k, shape, jnp.float32) * (fan_in ** -0.5)

    def gain(k, shape):
        return 1.0 + 0.05 * jax.random.normal(k, shape, jnp.float32)

    return {
        'x': jax.random.normal(ks[0], (BATCH, SEQ, D_MODEL), jnp.float32),
        'positions': jnp.tile(jnp.arange(SEQ, dtype=jnp.int32)[None, :], (BATCH, 1)),
        'ffn_norm_w': gain(ks[1], (DEPTH, 2, D_MODEL)),
        'ffn_w_in': normal(ks[2], (DEPTH, 2, D_MODEL, 2 * D_FF), D_MODEL),
        'ffn_w_out': normal(ks[3], (DEPTH, 2, D_FF, D_MODEL), D_FF),
        'mix_norm_w': gain(ks[4], (DEPTH, D_MODEL)),
        'nsa_w_in': normal(ks[5], (N_NSA_LAYERS, D_MODEL, NSA_IN_COLS), D_MODEL),
        'nsa_cmp_pe': 0.1 * jax.random.normal(ks[6], (N_NSA_LAYERS, 2, NSA_CMP_LEN, NSA_HEAD_DIM), jnp.float32),
        'nsa_cmp_w1': normal(ks[7], (N_NSA_LAYERS, 2, NSA_CMP_LEN * NSA_HEAD_DIM, NSA_CMP_HIDDEN), NSA_CMP_LEN * NSA_HEAD_DIM),
        'nsa_cmp_w2': normal(ks[8], (N_NSA_LAYERS, 2, NSA_CMP_HIDDEN, NSA_HEAD_DIM), NSA_CMP_HIDDEN),
        'nsa_w_out': normal(ks[9], (N_NSA_LAYERS, NSA_HEADS * NSA_HEAD_DIM, D_MODEL), NSA_HEADS * NSA_HEAD_DIM),
        'mla_w_in': normal(ks[10], (N_MLA_LAYERS, D_MODEL, MLA_IN_COLS), D_MODEL),
        'mla_q_norm_w': gain(ks[11], (N_MLA_LAYERS, MLA_Q_RANK)),
        'mla_kv_norm_w': gain(ks[12], (N_MLA_LAYERS, MLA_KV_RANK)),
        'mla_w_uq': normal(ks[13], (N_MLA_LAYERS, MLA_Q_RANK, MLA_HEADS * MLA_QK_DIM), MLA_Q_RANK),
        'mla_w_ukv': normal(ks[14], (N_MLA_LAYERS, MLA_KV_RANK, MLA_HEADS * (MLA_NOPE_DIM + MLA_V_DIM)), MLA_KV_RANK),
        'mla_w_out': normal(ks[15], (N_MLA_LAYERS, MLA_HEADS * MLA_V_DIM, D_MODEL), MLA_HEADS * MLA_V_DIM),
        'final_norm_w': gain(ks[16], (D_MODEL,)),
    }


def reference(x, positions, ffn_norm_w, ffn_w_in, ffn_w_out, mix_norm_w,
              nsa_w_in, nsa_cmp_pe, nsa_cmp_w1, nsa_cmp_w2, nsa_w_out,
              mla_w_in, mla_q_norm_w, mla_kv_norm_w, mla_w_uq, mla_w_ukv, mla_w_out,
              final_norm_w):
    for i in range(DEPTH):
        x = x + 0.5 * swiglu(rms_norm(x, ffn_norm_w[i, 0]), ffn_w_in[i, 0], ffn_w_out[i, 0])
        h = rms_norm(x, mix_norm_w[i])
        j = i // N_MIXERS
        if i % N_MIXERS == 0:
            x = x + nsa_mixer(h, positions, nsa_w_in[j], nsa_cmp_pe[j], nsa_cmp_w1[j], nsa_cmp_w2[j], nsa_w_out[j])
        else:
            x = x + mla_mixer(h, positions, mla_w_in[j], mla_q_norm_w[j], mla_kv_norm_w[j],
                              mla_w_uq[j], mla_w_ukv[j], mla_w_out[j])
        x = x + 0.5 * swiglu(rms_norm(x, ffn_norm_w[i, 1]), ffn_w_in[i, 1], ffn_w_out[i, 1])
    return rms_norm(x, final_norm_w)
```

```python
import functools
import math

import jax
import jax.numpy as jnp
from jax import lax
from jax.experimental import pallas as pl
from jax.experimental.pallas import tpu as pltpu

F32 = jnp.float32
BF16 = jnp.bfloat16

LANE = 128
RMS_EPS = 1e-6
ROPE_THETA = 10000.0
NEG_INF = -1e30
FORCE_BONUS = 1e9
LOG2E = math.log2(math.e)
VMEM_LIMIT = 56 * 1024 * 1024

NSA_HEADS = 16
NSA_HEAD_DIM = 128
NSA_KV_GROUPS = 4
NSA_R = NSA_HEADS // NSA_KV_GROUPS
NSA_N_BRANCHES = 3
NSA_CMP_LEN = 32
NSA_CMP_STRIDE = 16
NSA_SEL_BLOCK = 64
NSA_N_SELECT = 16
NSA_WINDOW = 512

MLA_HEADS = 16
MLA_NOPE_DIM = 128
MLA_ROPE_DIM = 64
MLA_V_DIM = 128
MLA_QK_DIM = MLA_NOPE_DIM + MLA_ROPE_DIM


def _pick(n, pref):
    if n <= pref:
        return n
    t = (pref // LANE) * LANE
    while t > LANE and n % t:
        t -= LANE
    assert n % t == 0, (n, pref)
    return t


def _div_pow2(x, d):
    assert d & (d - 1) == 0, d
    return jnp.right_shift(x, d.bit_length() - 1)


def _mod_pow2(x, d):
    assert d & (d - 1) == 0, d
    return jnp.bitwise_and(x, d - 1)


def _params(*sem):
    return pltpu.CompilerParams(dimension_semantics=sem, vmem_limit_bytes=VMEM_LIMIT)


def _rms_rows(x, w):
    ms = jnp.mean(x * x, axis=-1, keepdims=True)
    return x * lax.rsqrt(ms + RMS_EPS) * w


def _norm_matmul_kernel(x_ref, nw_ref, w_ref, o_ref, xn_scr):
    @pl.when(pl.program_id(1) == 0)
    def _():
        xn_scr[...] = _rms_rows(x_ref[...], nw_ref[...]).astype(BF16)

    o_ref[...] = jnp.dot(xn_scr[...], w_ref[...], preferred_element_type=F32).astype(o_ref.dtype)


def norm_matmul(x, norm_w, w, *, x_col_block=0, out_dtype=F32, tm=1024, tn=1024):
    T = x.shape[0]
    K, N = w.shape
    tm, tn = _pick(T, tm), _pick(N, tn)
    return pl.pallas_call(
        _norm_matmul_kernel,
        grid=(T // tm, N // tn),
        in_specs=[
            pl.BlockSpec((tm, K), lambda i, j: (i, x_col_block)),
            pl.BlockSpec((1, K), lambda i, j: (0, 0)),
            pl.BlockSpec((K, tn), lambda i, j: (0, j)),
        ],
        out_specs=pl.BlockSpec((tm, tn), lambda i, j: (i, j)),
        out_shape=jax.ShapeDtypeStruct((T, N), out_dtype),
        scratch_shapes=[pltpu.VMEM((tm, K), BF16)],
        compiler_params=_params("parallel", "arbitrary"),
        name="norm_matmul",
    )(x, norm_w.reshape(1, K).astype(F32), w)


def _matmul_residual_kernel(a_ref, w_ref, r_ref, o_ref):
    o_ref[...] = r_ref[...] + jnp.dot(a_ref[...], w_ref[...], preferred_element_type=F32)


def matmul_residual(a, w, res, *, tm=1024, tn=1024):
    T, K = a.shape
    N = w.shape[1]
    tm, tn = _pick(T, tm), _pick(N, tn)
    return pl.pallas_call(
        _matmul_residual_kernel,
        grid=(T // tm, N // tn),
        in_specs=[
            pl.BlockSpec((tm, K), lambda i, j: (i, 0)),
            pl.BlockSpec((K, tn), lambda i, j: (0, j)),
            pl.BlockSpec((tm, tn), lambda i, j: (i, j)),
        ],
        out_specs=pl.BlockSpec((tm, tn), lambda i, j: (i, j)),
        out_shape=jax.ShapeDtypeStruct((T, N), F32),
        compiler_params=_params("parallel", "arbitrary"),
        name="matmul_residual",
    )(a, w, res)


def _ffn_kernel(x_ref, nw_ref, wg_ref, wu_ref, wo_ref, fw_ref, o_ref, xn_scr, acc_scr, *, final_norm):
    f = pl.program_id(1)

    @pl.when(f == 0)
    def _():
        xn_scr[...] = _rms_rows(x_ref[...], nw_ref[...]).astype(BF16)
        acc_scr[...] = jnp.zeros_like(acc_scr)

    xn = xn_scr[...]
    g = jnp.dot(xn, wg_ref[...], preferred_element_type=F32)
    u = jnp.dot(xn, wu_ref[...], preferred_element_type=F32)
    h = (g * jax.nn.sigmoid(g) * u).astype(BF16)
    acc_scr[...] += jnp.dot(h, wo_ref[...], preferred_element_type=F32)

    @pl.when(f == pl.num_programs(1) - 1)
    def _():
        y = x_ref[...] + 0.5 * acc_scr[...]
        if final_norm:
            y = _rms_rows(y, fw_ref[...])
        o_ref[...] = y


def ffn_half_step(x, norm_w, w_in, w_out, final_w=None, *, tm=512, tf=512):
    T, D = x.shape
    DFF = w_out.shape[0]
    tm, tf = _pick(T, tm), _pick(DFF, tf)
    nf = DFF // tf
    final_norm = final_w is not None
    fw = (final_w if final_norm else norm_w).reshape(1, D).astype(F32)
    return pl.pallas_call(
        functools.partial(_ffn_kernel, final_norm=final_norm),
        grid=(T // tm, nf),
        in_specs=[
            pl.BlockSpec((tm, D), lambda i, f: (i, 0)),
            pl.BlockSpec((1, D), lambda i, f: (0, 0)),
            pl.BlockSpec((D, tf), lambda i, f: (0, f)),
            pl.BlockSpec((D, tf), lambda i, f: (0, f + nf)),
            pl.BlockSpec((tf, D), lambda i, f: (f, 0)),
            pl.BlockSpec((1, D), lambda i, f: (0, 0)),
        ],
        out_specs=pl.BlockSpec((tm, D), lambda i, f: (i, 0)),
        out_shape=jax.ShapeDtypeStruct((T, D), F32),
        scratch_shapes=[pltpu.VMEM((tm, D), BF16), pltpu.VMEM((tm, D), F32)],
        compiler_params=_params("parallel", "arbitrary"),
        name="ffn_half_step",
    )(x, norm_w.reshape(1, D).astype(F32), w_in, w_in, w_out, fw)


def _flash_kernel(*refs, R, E, has_kx, tq, tk, mode, window, c_exp, chunks_per_version):
    it = iter(refs)
    qm_ref = next(it)
    qx_ref = next(it) if E else None
    km_ref = next(it)
    kx_ref = next(it) if has_kx else None
    v_ref = next(it)
    o_ref = next(it)
    q_scr, m_scr, l_scr, acc_scr = it
    M = R * tq
    i = pl.program_id(2)

    for e in range(max(E, 1)):
        for r in range(R):
            q_scr[e, r * tq:(r + 1) * tq, 0:LANE] = qm_ref[0, :, r * LANE:(r + 1) * LANE]
            if E:
                q_scr[e, r * tq:(r + 1) * tq, LANE:2 * LANE] = qx_ref[0, :, e * LANE:(e + 1) * LANE]
    m_scr[...] = jnp.full_like(m_scr, NEG_INF)
    l_scr[...] = jnp.zeros_like(l_scr)
    acc_scr[...] = jnp.zeros_like(acc_scr)

    def chunk(c, masked):
        k0 = pl.multiple_of(c * tk, tk)
        k = km_ref[0, pl.ds(k0, tk), :]
        if has_kx:
            k = jnp.concatenate([k, kx_ref[0, pl.ds(k0, tk), :]], axis=1)
        q = q_scr[c // chunks_per_version] if E > 1 else q_scr[0]
        s = lax.dot_general(q, k, (((1,), (1,)), ((), ())), preferred_element_type=F32)
        if masked:
            row = lax.broadcasted_iota(jnp.int32, (M, tk), 0)
            col = lax.broadcasted_iota(jnp.int32, (M, tk), 1)
            t = i * tq + (_mod_pow2(row, tq) if R > 1 else row)
            kpos = k0 + col
            ok = kpos <= t
            if mode == "window":
                ok = jnp.logical_and(ok, t - kpos < window)
            s = jnp.where(ok, s, NEG_INF)
        m_prev = m_scr[...]
        m_new = jnp.maximum(m_prev, jnp.max(s, axis=1, keepdims=True))
        alpha = jnp.exp2((m_prev - m_new) * c_exp)
        p = jnp.exp2((s - pltpu.repeat(m_new, tk // LANE, axis=1)) * c_exp)
        l_scr[...] = alpha * l_scr[...] + jnp.sum(p, axis=1, keepdims=True)
        v = v_ref[0, pl.ds(k0, tk), :]
        acc_scr[...] = alpha * acc_scr[...] + jnp.dot(p.astype(BF16), v, preferred_element_type=F32)
        m_scr[...] = m_new

    def run(lo, hi, masked):
        def body(c, carry):
            chunk(c, masked)
            return carry

        lax.fori_loop(lo, hi, body, 0)

    if mode == "causal":
        n_full = (i * tq) // tk
        run(0, n_full, False)
        for d in range(max(1, tq // tk)):
            chunk(n_full + d, True)
    else:
        lo = jnp.maximum(i * tq - (window - 1), 0) // tk
        hi = ((i + 1) * tq - 1) // tk
        run(lo, hi + 1, True)

    out = acc_scr[...] / l_scr[...]
    for r in range(R):
        o_ref[0, :, r * LANE:(r + 1) * LANE] = out[r * tq:(r + 1) * tq].astype(o_ref.dtype)


def flash_attention(qm, qx, km, kx, v, *, n_groups, R, E, km_col, v_col, kx_batched, mode, scale,
                    window=0, tq, tk, keys_per_version=None, out_dtype=F32):
    B, S, _ = qm.shape
    tq, tk = _pick(S, tq), _pick(S, tk)
    assert tq % tk == 0 or tk % tq == 0
    has_kx = kx is not None
    dk = 2 * LANE if has_kx else LANE
    cpv = (keys_per_version // tk) if keys_per_version else 1
    M = R * tq
    in_specs = [pl.BlockSpec((1, tq, R * LANE), lambda b, g, i: (b, i, g))]
    args = [qm]
    if E:
        in_specs.append(pl.BlockSpec((1, tq, E * LANE), lambda b, g, i: (b, i, g)))
        args.append(qx)
    in_specs.append(pl.BlockSpec((1, S, LANE), lambda b, g, i: (b, 0, km_col(g))))
    args.append(km)
    if has_kx:
        in_specs.append(pl.BlockSpec((1, S, LANE), lambda b, g, i: (b if kx_batched else 0, 0, 0)))
        args.append(kx)
    in_specs.append(pl.BlockSpec((1, S, LANE), lambda b, g, i: (b, 0, v_col(g))))
    args.append(v)
    kern = functools.partial(_flash_kernel, R=R, E=E, has_kx=has_kx, tq=tq, tk=tk, mode=mode, window=window,
                             c_exp=scale * LOG2E, chunks_per_version=cpv)
    return pl.pallas_call(
        kern,
        grid=(B, n_groups, S // tq),
        in_specs=in_specs,
        out_specs=pl.BlockSpec((1, tq, R * LANE), lambda b, g, i: (b, i, g)),
        out_shape=jax.ShapeDtypeStruct((B, S, n_groups * R * LANE), out_dtype),
        scratch_shapes=[
            pltpu.VMEM((max(E, 1), M, dk), BF16),
            pltpu.VMEM((M, LANE), F32),
            pltpu.VMEM((M, LANE), F32),
            pltpu.VMEM((M, LANE), F32),
        ],
        compiler_params=_params("parallel", "parallel", "arbitrary"),
        name="flash_" + mode + ("_x%d" % E if E else ""),
    )(*args)


def _compress_kernel(x_ref, pe_ref, w1_ref, w2_ref, o_ref):
    x = x_ref[0, 0, 0]
    ncp = x.shape[0]
    a = jnp.dot((x + pe_ref[0, 0:1, :]).astype(BF16), w1_ref[0, 0], preferred_element_type=F32)
    b = jnp.dot((x + pe_ref[0, 1:2, :]).astype(BF16), w1_ref[0, 1], preferred_element_type=F32)
    h = jax.nn.gelu(a + pltpu.roll(b, ncp - 1, axis=0))
    o_ref[0, 0, 0] = jnp.dot(h.astype(BF16), w2_ref[0], preferred_element_type=F32)


def compress_tokens(x, pe, w1, w2):
    _, B, G, NCP, KW = x.shape
    hidden, dk = w2.shape[1], w2.shape[2]
    return pl.pallas_call(
        _compress_kernel,
        grid=(2, B, G),
        in_specs=[
            pl.BlockSpec((1, 1, 1, NCP, KW), lambda s, b, g: (s, b, g, 0, 0)),
            pl.BlockSpec((1, 2, KW), lambda s, b, g: (s, 0, 0)),
            pl.BlockSpec((1, 2, KW, hidden), lambda s, b, g: (s, 0, 0, 0)),
            pl.BlockSpec((1, hidden, dk), lambda s, b, g: (s, 0, 0)),
        ],
        out_specs=pl.BlockSpec((1, 1, 1, NCP, dk), lambda s, b, g: (s, b, g, 0, 0)),
        out_shape=jax.ShapeDtypeStruct((2, B, G, NCP, dk), F32),
        compiler_params=_params("parallel", "parallel", "parallel"),
        name="compress_tokens",
    )(x, pe, w1, w2)


def _cmp_select_kernel(q_ref, kc_ref, vc_ref, ov_ref, oc_ref, nm_ref, *, R, tq, n_cmp, n_top, scale):
    i = pl.program_id(2)
    M = R * tq
    q = jnp.concatenate([q_ref[0, :, r * LANE:(r + 1) * LANE] for r in range(R)], axis=0)
    kc = kc_ref[0, 0]
    ncp = kc.shape[0]
    s = lax.dot_general(q, kc, (((1,), (1,)), ((), ())), preferred_element_type=F32) * scale
    row = lax.broadcasted_iota(jnp.int32, (M, ncp), 0)
    col = lax.broadcasted_iota(jnp.int32, (M, ncp), 1)
    t = i * tq + _mod_pow2(row, tq)
    valid = jnp.logical_and(col * NSA_CMP_STRIDE + (NSA_CMP_LEN - 1) <= t, col < n_cmp)
    sm = jnp.where(valid, s, NEG_INF)
    e = jnp.exp(sm - jnp.max(sm, axis=1, keepdims=True))
    p = jnp.where(valid, e / jnp.sum(e, axis=1, keepdims=True), 0.0)
    oc = jnp.dot(p.astype(BF16), vc_ref[0, 0], preferred_element_type=F32)
    for r in range(R):
        oc_ref[0, :, r * LANE:(r + 1) * LANE] = oc[r * tq:(r + 1) * tq].astype(oc_ref.dtype)

    psum = p[0:tq]
    for r in range(1, R):
        psum = psum + p[r * tq:(r + 1) * tq]
    hi = psum.astype(BF16)
    lo = (psum - hi.astype(F32)).astype(BF16)
    ov = ov_ref[...]
    imp = jnp.dot(hi, ov, preferred_element_type=F32) + jnp.dot(lo, ov, preferred_element_type=F32)
    nb = imp.shape[1]
    j = lax.broadcasted_iota(jnp.int32, (tq, nb), 1)
    jf = j.astype(F32)
    cur = _div_pow2(i * tq + lax.broadcasted_iota(jnp.int32, (tq, nb), 0), NSA_SEL_BLOCK)
    forced = jnp.logical_or(j == 0, jnp.logical_or(j == cur, j == cur - 1))
    x = jnp.where(j > cur, NEG_INF, imp + FORCE_BONUS * forced.astype(F32))
    sel = jnp.zeros((tq, nb), F32)
    for _ in range(n_top):
        mx = jnp.max(x, axis=1, keepdims=True)
        first = jnp.min(jnp.where(x == mx, jf, float(nb)), axis=1, keepdims=True)
        hit = jf == first
        sel = jnp.where(jnp.logical_and(hit, mx > 0.5 * NEG_INF), 1.0, sel)
        x = jnp.where(hit, -jnp.inf, x)
    nm_ref[0] = jnp.where(sel > 0.5, 0.0, NEG_INF).astype(nm_ref.dtype)


def cmp_attention_select(q, k_cmp, v_cmp, overlap, *, n_cmp, n_top, scale, tq=128):
    B, S, _ = q.shape
    G, R = NSA_KV_GROUPS, NSA_R
    NCP, NB = overlap.shape
    tq = _pick(S, tq)
    kern = functools.partial(_cmp_select_kernel, R=R, tq=tq, n_cmp=n_cmp, n_top=n_top, scale=scale)
    return pl.pallas_call(
        kern,
        grid=(B, G, S // tq),
        in_specs=[
            pl.BlockSpec((1, tq, R * LANE), lambda b, g, i: (b, i, g)),
            pl.BlockSpec((1, 1, NCP, LANE), lambda b, g, i: (b, g, 0, 0)),
            pl.BlockSpec((1, 1, NCP, LANE), lambda b, g, i: (b, g, 0, 0)),
            pl.BlockSpec((NCP, NB), lambda b, g, i: (0, 0)),
        ],
        out_specs=[
            pl.BlockSpec((1, tq, R * LANE), lambda b, g, i: (b, i, g)),
            pl.BlockSpec((1, tq, NB), lambda b, g, i: (b, i, g)),
        ],
        out_shape=[
            jax.ShapeDtypeStruct((B, S, G * R * LANE), F32),
            jax.ShapeDtypeStruct((B, S, G * NB), BF16),
        ],
        compiler_params=_params("parallel", "parallel", "arbitrary"),
        name="cmp_attention_select",
    )(q, k_cmp, v_cmp, overlap)


def _rope_tables(pos, dim):
    inv = 1.0 / (ROPE_THETA ** (jnp.arange(0, dim, 2, dtype=F32) / dim))
    ang = pos.astype(F32)[..., None] * inv
    return jnp.cos(ang), jnp.sin(ang)


def _apply_rope(x, cos, sin):
    c, s = cos[..., None, :], sin[..., None, :]
    x1, x2 = jnp.split(x, 2, axis=-1)
    return jnp.concatenate([x1 * c - x2 * s, x2 * c + x1 * s], axis=-1)


def nsa_mixer(x, B, S, positions, norm_w, w_in, cmp_pe, cmp_w1, cmp_w2, w_out):
    T, D = x.shape
    H, G, R, dk = NSA_HEADS, NSA_KV_GROUPS, NSA_R, NSA_HEAD_DIM
    q_cols, kv_cols = H * dk, NSA_N_BRANCHES * 2 * G * dk
    n_gate = NSA_N_BRANCHES * H
    scale = dk ** -0.5

    w_qkv = w_in[:, :q_cols + kv_cols].astype(BF16)
    w_gate = jnp.pad(w_in[:, q_cols + kv_cols:], ((0, 0), (0, LANE - n_gate))).astype(BF16)
    proj = norm_matmul(x, norm_w, w_qkv)
    gates = jax.nn.sigmoid(norm_matmul(x, norm_w, w_gate)[:, :n_gate]).reshape(T, H, NSA_N_BRANCHES)

    cos, sin = _rope_tables(positions.reshape(T), dk)
    q = _apply_rope(proj[:, :q_cols].reshape(T, H, dk), cos, sin).astype(BF16).reshape(B, S, q_cols)
    kv = proj[:, q_cols:].reshape(T, NSA_N_BRANCHES, 2, G, dk)
    k_slc = _apply_rope(kv[:, 1, 0], cos, sin).astype(BF16).reshape(B, S, G * dk)
    v_slc = kv[:, 1, 1].astype(BF16).reshape(B, S, G * dk)
    k_win = _apply_rope(kv[:, 2, 0], cos, sin).astype(BF16).reshape(B, S, G * dk)
    v_win = kv[:, 2, 1].astype(BF16).reshape(B, S, G * dk)

    ncp = S // NSA_CMP_STRIDE
    n_cmp = (S - NSA_CMP_LEN) // NSA_CMP_STRIDE + 1
    slab = NSA_CMP_STRIDE * dk
    xc = kv[:, 0].reshape(B, S, 2, G, dk).transpose(2, 0, 3, 1, 4).reshape(2, B, G, ncp, slab)
    hidden = cmp_w1.shape[-1]
    kvc = compress_tokens(xc, cmp_pe.reshape(2, 2, slab), cmp_w1.reshape(2, 2, slab, hidden).astype(BF16),
                          cmp_w2.astype(BF16))
    cmp_end = jnp.minimum(jnp.arange(ncp) * NSA_CMP_STRIDE + NSA_CMP_LEN - 1, S - 1)
    cos_c, sin_c = _rope_tables(positions[:, cmp_end], dk)
    k_cmp = _apply_rope(kvc[0].transpose(0, 2, 1, 3), cos_c, sin_c).transpose(0, 2, 1, 3).astype(BF16)
    v_cmp = kvc[1].astype(BF16)

    n_sel = S // NSA_SEL_BLOCK
    nb = -(-n_sel // LANE) * LANE
    blk_start = jnp.arange(ncp) * NSA_CMP_STRIDE
    sel_start = jnp.arange(nb) * NSA_SEL_BLOCK
    overlap = ((blk_start[:, None] <= sel_start[None, :] + NSA_SEL_BLOCK - 1)
               & (blk_start[:, None] + NSA_CMP_LEN - 1 >= sel_start[None, :])
               & (jnp.arange(ncp)[:, None] < n_cmp) & (jnp.arange(nb)[None, :] < n_sel)).astype(BF16)
    o_cmp, neg_mask = cmp_attention_select(q, k_cmp, v_cmp, overlap, n_cmp=n_cmp,
                                           n_top=min(NSA_N_SELECT, n_sel), scale=scale)

    blk_onehot = (jnp.arange(S)[:, None] // NSA_SEL_BLOCK % LANE == jnp.arange(LANE)[None, :]).astype(BF16)[None]
    o_slc = flash_attention(q, neg_mask, k_slc, blk_onehot, v_slc, n_groups=G, R=R, E=nb // LANE,
                            km_col=lambda g: g, v_col=lambda g: g, kx_batched=False, mode="causal", scale=scale,
                            tq=256, tk=256, keys_per_version=LANE * NSA_SEL_BLOCK)
    o_win = flash_attention(q, None, k_win, None, v_win, n_groups=G, R=R, E=0,
                            km_col=lambda g: g, v_col=lambda g: g, kx_batched=False, mode="window", scale=scale,
                            window=NSA_WINDOW, tq=256, tk=256)

    o = (gates[:, :, 0:1] * o_cmp.reshape(T, H, dk) + gates[:, :, 1:2] * o_slc.reshape(T, H, dk)
         + gates[:, :, 2:3] * o_win.reshape(T, H, dk))
    return matmul_residual(o.reshape(T, H * dk).astype(BF16), w_out.astype(BF16), x)


def mla_mixer(x, B, S, positions, norm_w, w_in, q_norm_w, kv_norm_w, w_uq, w_ukv, w_out):
    T, D = x.shape
    H = MLA_HEADS
    q_rank, kv_rank = q_norm_w.shape[0], kv_norm_w.shape[0]
    assert q_rank == kv_rank and q_rank % LANE == 0
    pad_r = LANE - MLA_ROPE_DIM

    w_in_p = jnp.pad(w_in, ((0, 0), (0, pad_r))).astype(BF16)
    proj = norm_matmul(x, norm_w, w_in_p)
    w_uq3 = w_uq.reshape(q_rank, H, MLA_QK_DIM)
    w_uq_p = jnp.concatenate([w_uq3[:, :, :MLA_NOPE_DIM].reshape(q_rank, H * MLA_NOPE_DIM),
                              w_uq3[:, :, MLA_NOPE_DIM:].reshape(q_rank, H * MLA_ROPE_DIM)], axis=1).astype(BF16)
    qf = norm_matmul(proj, q_norm_w, w_uq_p, x_col_block=0)
    kvb = norm_matmul(proj, kv_norm_w, w_ukv.astype(BF16), x_col_block=1, out_dtype=BF16)

    cos, sin = _rope_tables(positions.reshape(T), MLA_ROPE_DIM)
    q_nope = qf[:, :H * MLA_NOPE_DIM].astype(BF16).reshape(B, S, H * MLA_NOPE_DIM)
    q_rope = _apply_rope(qf[:, H * MLA_NOPE_DIM:].reshape(T, H, MLA_ROPE_DIM), cos, sin)
    q_rope = jnp.pad(q_rope, ((0, 0), (0, 0), (0, pad_r))).astype(BF16).reshape(B, S, H * LANE)
    k_r = proj[:, 2 * q_rank:2 * q_rank + MLA_ROPE_DIM]
    k_rope = _apply_rope(k_r[:, None, :], cos, sin)[:, 0]
    k_rope = jnp.pad(k_rope, ((0, 0), (0, pad_r))).astype(BF16).reshape(B, S, LANE)

    o = flash_attention(q_nope, q_rope, kvb.reshape(B, S, 2 * H * LANE), k_rope, kvb.reshape(B, S, 2 * H * LANE),
                        n_groups=H, R=1, E=1, km_col=lambda h: 2 * h, v_col=lambda h: 2 * h + 1, kx_batched=True,
                        mode="causal", scale=MLA_QK_DIM ** -0.5, tq=512, tk=512, out_dtype=BF16)
    return matmul_residual(o.reshape(T, H * MLA_V_DIM), w_out.astype(BF16), x)


def kernel(x, positions, ffn_norm_w, ffn_w_in, ffn_w_out, mix_norm_w, nsa_w_in, nsa_cmp_pe, nsa_cmp_w1, nsa_cmp_w2,
           nsa_w_out, mla_w_in, mla_q_norm_w, mla_kv_norm_w, mla_w_uq, mla_w_ukv, mla_w_out, final_norm_w):
    B, S, D = x.shape
    depth = ffn_norm_w.shape[0]
    n_mixers = 2
    h = x.reshape(B * S, D)
    for i in range(depth):
        h = ffn_half_step(h, ffn_norm_w[i, 0], ffn_w_in[i, 0].astype(BF16), ffn_w_out[i, 0].astype(BF16))
        j = i // n_mixers
        if i % n_mixers == 0:
            h = nsa_mixer(h, B, S, positions, mix_norm_w[i], nsa_w_in[j], nsa_cmp_pe[j], nsa_cmp_w1[j],
                          nsa_cmp_w2[j], nsa_w_out[j])
        else:
            h = mla_mixer(h, B, S, positions, mix_norm_w[i], mla_w_in[j], mla_q_norm_w[j], mla_kv_norm_w[j],
                          mla_w_uq[j], mla_w_ukv[j], mla_w_out[j])
        last = i == depth - 1
        h = ffn_half_step(h, ffn_norm_w[i, 1], ffn_w_in[i, 1].astype(BF16), ffn_w_out[i, 1].astype(BF16),
                          final_w=final_norm_w if last else None)
    return h.reshape(B, S, D)
```

```python
import functools
import math

import jax
import jax.numpy as jnp
from jax import lax
from jax.experimental import pallas as pl
from jax.experimental.pallas import tpu as pltpu

F32 = jnp.float32
BF16 = jnp.bfloat16

LANE = 128
RMS_EPS = 1e-6
ROPE_THETA = 10000.0
NEG_INF = -1e30
FORCE_BONUS = 1e9
LOG2E = math.log2(math.e)
VMEM_LIMIT = 56 * 1024 * 1024

NSA_HEADS = 16
NSA_HEAD_DIM = 128
NSA_KV_GROUPS = 4
NSA_R = NSA_HEADS // NSA_KV_GROUPS
NSA_N_BRANCHES = 3
NSA_CMP_LEN = 32
NSA_CMP_STRIDE = 16
NSA_SEL_BLOCK = 64
NSA_N_SELECT = 16
NSA_WINDOW = 512

MLA_HEADS = 16
MLA_NOPE_DIM = 128
MLA_ROPE_DIM = 64
MLA_V_DIM = 128
MLA_QK_DIM = MLA_NOPE_DIM + MLA_ROPE_DIM


def _pick(n, pref):
    if n <= pref:
        return n
    t = (pref // LANE) * LANE
    while t > LANE and n % t:
        t -= LANE
    assert n % t == 0, (n, pref)
    return t


def _div_pow2(x, d):
    assert d & (d - 1) == 0, d
    return jnp.right_shift(x, d.bit_length() - 1)


def _mod_pow2(x, d):
    assert d & (d - 1) == 0, d
    return jnp.bitwise_and(x, d - 1)


def _params(*sem):
    return pltpu.CompilerParams(dimension_semantics=sem, vmem_limit_bytes=VMEM_LIMIT)


def _rms_rows(x, w):
    ms = jnp.mean(x * x, axis=-1, keepdims=True)
    return x * lax.rsqrt(ms + RMS_EPS) * w


def _norm_matmul_kernel(x_ref, nw_ref, w_ref, o_ref, xn_scr):
    @pl.when(pl.program_id(1) == 0)
    def _():
        xn_scr[...] = _rms_rows(x_ref[...], nw_ref[...]).astype(BF16)

    o_ref[...] = jnp.dot(xn_scr[...], w_ref[...], preferred_element_type=F32).astype(o_ref.dtype)


def norm_matmul(x, norm_w, w, *, x_col_block=0, out_dtype=F32, tm=1024, tn=1024):
    T = x.shape[0]
    K, N = w.shape
    tm, tn = _pick(T, tm), _pick(N, tn)
    return pl.pallas_call(
        _norm_matmul_kernel,
        grid=(T // tm, N // tn),
        in_specs=[
            pl.BlockSpec((tm, K), lambda i, j: (i, x_col_block)),
            pl.BlockSpec((1, K), lambda i, j: (0, 0)),
            pl.BlockSpec((K, tn), lambda i, j: (0, j)),
        ],
        out_specs=pl.BlockSpec((tm, tn), lambda i, j: (i, j)),
        out_shape=jax.ShapeDtypeStruct((T, N), out_dtype),
        scratch_shapes=[pltpu.VMEM((tm, K), BF16)],
        compiler_params=_params("parallel", "arbitrary"),
        name="norm_matmul",
    )(x, norm_w.reshape(1, K).astype(F32), w)


def _matmul_residual_kernel(a_ref, w_ref, r_ref, o_ref):
    o_ref[...] = r_ref[...] + jnp.dot(a_ref[...], w_ref[...], preferred_element_type=F32)


def matmul_residual(a, w, res, *, tm=1024, tn=1024):
    T, K = a.shape
    N = w.shape[1]
    tm, tn = _pick(T, tm), _pick(N, tn)
    return pl.pallas_call(
        _matmul_residual_kernel,
        grid=(T // tm, N // tn),
        in_specs=[
            pl.BlockSpec((tm, K), lambda i, j: (i, 0)),
            pl.BlockSpec((K, tn), lambda i, j: (0, j)),
            pl.BlockSpec((tm, tn), lambda i, j: (i, j)),
        ],
        out_specs=pl.BlockSpec((tm, tn), lambda i, j: (i, j)),
        out_shape=jax.ShapeDtypeStruct((T, N), F32),
        compiler_params=_params("parallel", "arbitrary"),
        name="matmul_residual",
    )(a, w, res)


def _ffn_kernel(x_ref, nw_ref, wg_ref, wu_ref, wo_ref, fw_ref, o_ref, xn_scr, acc_scr, *, final_norm):
    f = pl.program_id(1)

    @pl.when(f == 0)
    def _():
        xn_scr[...] = _rms_rows(x_ref[...], nw_ref[...]).astype(BF16)
        acc_scr[...] = jnp.zeros_like(acc_scr)

    xn = xn_scr[...]
    g = jnp.dot(xn, wg_ref[...], preferred_element_type=F32)
    u = jnp.dot(xn, wu_ref[...], preferred_element_type=F32)
    h = (g * jax.nn.sigmoid(g) * u).astype(BF16)
    acc_scr[...] += jnp.dot(h, wo_ref[...], preferred_element_type=F32)

    @pl.when(f == pl.num_programs(1) - 1)
    def _():
        y = x_ref[...] + 0.5 * acc_scr[...]
        if final_norm:
            y = _rms_rows(y, fw_ref[...])
        o_ref[...] = y


def ffn_half_step(x, norm_w, w_in, w_out, final_w=None, *, tm=512, tf=512):
    T, D = x.shape
    DFF = w_out.shape[0]
    tm, tf = _pick(T, tm), _pick(DFF, tf)
    nf = DFF // tf
    final_norm = final_w is not None
    fw = (final_w if final_norm else norm_w).reshape(1, D).astype(F32)
    return pl.pallas_call(
        functools.partial(_ffn_kernel, final_norm=final_norm),
        grid=(T // tm, nf),
        in_specs=[
            pl.BlockSpec((tm, D), lambda i, f: (i, 0)),
            pl.BlockSpec((1, D), lambda i, f: (0, 0)),
            pl.BlockSpec((D, tf), lambda i, f: (0, f)),
            pl.BlockSpec((D, tf), lambda i, f: (0, f + nf)),
            pl.BlockSpec((tf, D), lambda i, f: (f, 0)),
            pl.BlockSpec((1, D), lambda i, f: (0, 0)),
        ],
        out_specs=pl.BlockSpec((tm, D), lambda i, f: (i, 0)),
        out_shape=jax.ShapeDtypeStruct((T, D), F32),
        scratch_shapes=[pltpu.VMEM((tm, D), BF16), pltpu.VMEM((tm, D), F32)],
        compiler_params=_params("parallel", "arbitrary"),
        name="ffn_half_step",
    )(x, norm_w.reshape(1, D).astype(F32), w_in, w_in, w_out, fw)


def _flash_kernel(*refs, R, E, has_kx, tq, tk, mode, window, chunks_per_version, depth):
    it = iter(refs)
    qm_ref = next(it)
    qx_ref = next(it) if E else None
    km_ref = next(it)
    kx_ref = next(it) if has_kx else None
    v_ref = next(it)
    o_ref = next(it)
    q_scr, m_scr, acc_scr, s_scr = it
    M = R * tq
    i = pl.program_id(2)

    for e in range(max(E, 1)):
        for r in range(R):
            q_scr[e, r * tq:(r + 1) * tq, 0:LANE] = qm_ref[0, :, r * LANE:(r + 1) * LANE]
            if E:
                q_scr[e, r * tq:(r + 1) * tq, LANE:2 * LANE] = qx_ref[0, :, e * LANE:(e + 1) * LANE]
    m_scr[...] = jnp.full_like(m_scr, NEG_INF)
    acc_scr[...] = jnp.zeros_like(acc_scr)
    ones_col = jnp.where(lax.broadcasted_iota(jnp.int32, (tk, LANE), 1) == 0, 1.0, 0.0).astype(BF16)

    def scores(c):
        k0 = pl.multiple_of(c * tk, tk)
        k = km_ref[0, pl.ds(k0, tk), :]
        if has_kx:
            k = jnp.concatenate([k, kx_ref[0, pl.ds(k0, tk), :]], axis=1)
        q = q_scr[c // chunks_per_version] if E > 1 else q_scr[0]
        return lax.dot_general(q, k, (((1,), (1,)), ((), ())), preferred_element_type=F32).astype(BF16)

    def softmax_pv(c, s, masked):
        k0 = pl.multiple_of(c * tk, tk)
        if masked:
            row = lax.broadcasted_iota(jnp.int32, (M, tk), 0)
            col = lax.broadcasted_iota(jnp.int32, (M, tk), 1)
            t = i * tq + (_mod_pow2(row, tq) if R > 1 else row)
            kpos = k0 + col
            ok = kpos <= t
            if mode == "window":
                ok = jnp.logical_and(ok, t - kpos < window)
            s = jnp.where(ok, s, NEG_INF)
        m_prev = m_scr[...]
        m_new = jnp.maximum(m_prev, jnp.max(s, axis=1, keepdims=True).astype(F32))
        alpha = jnp.exp2(m_prev - m_new)
        p = jnp.exp2(s - pltpu.repeat(m_new.astype(BF16), tk // LANE, axis=1))
        v1 = jnp.concatenate([v_ref[0, pl.ds(k0, tk), :], ones_col], axis=1)
        acc_scr[...] = pltpu.repeat(alpha, 2, axis=1) * acc_scr[...] + jnp.dot(p, v1, preferred_element_type=F32)
        m_scr[...] = m_new

    if mode == "causal":
        n_full = (i * tq) // tk
        bufs = [s_scr.at[u] for u in range(depth)]
        bufs[0][...] = scores(0)

        def group(j, carry):
            for u in range(depth):
                bufs[(u + 1) % depth][...] = scores(depth * j + u + 1)
                softmax_pv(depth * j + u, bufs[u][...], False)
            return carry

        n_groups = n_full // depth
        lax.fori_loop(0, n_groups, group, 0)

        def tail(c, carry):
            softmax_pv(c, bufs[0][...], False)
            bufs[0][...] = scores(c + 1)
            return carry

        lax.fori_loop(depth * n_groups, n_full, tail, 0)
        softmax_pv(n_full, bufs[0][...], True)
    else:
        lo = jnp.maximum(i * tq - (window - 1), 0) // tk
        hi = ((i + 1) * tq - 1) // tk

        def body(c, carry):
            softmax_pv(c, scores(c), True)
            return carry

        lax.fori_loop(lo, hi + 1, body, 0)

    out = acc_scr[:, 0:LANE] / acc_scr[:, LANE:LANE + 1]
    for r in range(R):
        o_ref[0, :, r * LANE:(r + 1) * LANE] = out[r * tq:(r + 1) * tq].astype(o_ref.dtype)


def flash_attention(qm, qx, km, kx, v, *, n_groups, R, E, km_col, v_col, kx_batched, mode,
                    window=0, tq, tk, depth=2, keys_per_version=None, out_dtype=F32):
    B, S, _ = qm.shape
    tq, tk = _pick(S, tq), _pick(S, tk)
    assert tk % tq == 0
    has_kx = kx is not None
    dk = 2 * LANE if has_kx else LANE
    cpv = (keys_per_version // tk) if keys_per_version else 1
    M = R * tq
    in_specs = [pl.BlockSpec((1, tq, R * LANE), lambda b, g, i: (b, i, g))]
    args = [qm]
    if E:
        in_specs.append(pl.BlockSpec((1, tq, E * LANE), lambda b, g, i: (b, i, g)))
        args.append(qx)
    in_specs.append(pl.BlockSpec((1, S, LANE), lambda b, g, i: (b, 0, km_col(g))))
    args.append(km)
    if has_kx:
        in_specs.append(pl.BlockSpec((1, S, LANE), lambda b, g, i: (b if kx_batched else 0, 0, 0)))
        args.append(kx)
    in_specs.append(pl.BlockSpec((1, S, LANE), lambda b, g, i: (b, 0, v_col(g))))
    args.append(v)
    kern = functools.partial(_flash_kernel, R=R, E=E, has_kx=has_kx, tq=tq, tk=tk, mode=mode, window=window,
                             chunks_per_version=cpv, depth=depth)
    return pl.pallas_call(
        kern,
        grid=(B, n_groups, S // tq),
        in_specs=in_specs,
        out_specs=pl.BlockSpec((1, tq, R * LANE), lambda b, g, i: (b, i, g)),
        out_shape=jax.ShapeDtypeStruct((B, S, n_groups * R * LANE), out_dtype),
        scratch_shapes=[
            pltpu.VMEM((max(E, 1), M, dk), BF16),
            pltpu.VMEM((M, LANE), F32),
            pltpu.VMEM((M, 2 * LANE), F32),
            pltpu.VMEM((depth, M, tk), BF16),
        ],
        compiler_params=_params("parallel", "parallel", "arbitrary"),
        name="flash_" + mode + ("_x%d" % E if E else ""),
    )(*args)


def _compress_kernel(x_ref, pe_ref, w1_ref, w2_ref, o_ref):
    x = x_ref[0, 0, 0]
    ncp = x.shape[0]
    a = jnp.dot((x + pe_ref[0, 0:1, :]).astype(BF16), w1_ref[0, 0], preferred_element_type=F32)
    b = jnp.dot((x + pe_ref[0, 1:2, :]).astype(BF16), w1_ref[0, 1], preferred_element_type=F32)
    h = jax.nn.gelu(a + pltpu.roll(b, ncp - 1, axis=0))
    o_ref[0, 0, 0] = jnp.dot(h.astype(BF16), w2_ref[0], preferred_element_type=F32)


def compress_tokens(x, pe, w1, w2):
    _, B, G, NCP, KW = x.shape
    hidden, dk = w2.shape[1], w2.shape[2]
    return pl.pallas_call(
        _compress_kernel,
        grid=(2, B, G),
        in_specs=[
            pl.BlockSpec((1, 1, 1, NCP, KW), lambda s, b, g: (s, b, g, 0, 0)),
            pl.BlockSpec((1, 2, KW), lambda s, b, g: (s, 0, 0)),
            pl.BlockSpec((1, 2, KW, hidden), lambda s, b, g: (s, 0, 0, 0)),
            pl.BlockSpec((1, hidden, dk), lambda s, b, g: (s, 0, 0)),
        ],
        out_specs=pl.BlockSpec((1, 1, 1, NCP, dk), lambda s, b, g: (s, b, g, 0, 0)),
        out_shape=jax.ShapeDtypeStruct((2, B, G, NCP, dk), F32),
        compiler_params=_params("parallel", "parallel", "parallel"),
        name="compress_tokens",
    )(x, pe, w1, w2)


def _cmp_select_kernel(q_ref, kc_ref, vc_ref, ov_ref, oc_ref, nm_ref, *, R, tq, n_cmp, n_top):
    i = pl.program_id(2)
    M = R * tq
    q = jnp.concatenate([q_ref[0, :, r * LANE:(r + 1) * LANE] for r in range(R)], axis=0)
    kc = kc_ref[0, 0]
    ncp = kc.shape[0]
    s = lax.dot_general(q, kc, (((1,), (1,)), ((), ())), preferred_element_type=F32)
    row = lax.broadcasted_iota(jnp.int32, (M, ncp), 0)
    col = lax.broadcasted_iota(jnp.int32, (M, ncp), 1)
    t = i * tq + _mod_pow2(row, tq)
    valid = jnp.logical_and(col * NSA_CMP_STRIDE + (NSA_CMP_LEN - 1) <= t, col < n_cmp)
    sm = jnp.where(valid, s, NEG_INF)
    e = jnp.exp2(sm - jnp.max(sm, axis=1, keepdims=True))
    p = jnp.where(valid, e / jnp.sum(e, axis=1, keepdims=True), 0.0)
    oc = jnp.dot(p.astype(BF16), vc_ref[0, 0], preferred_element_type=F32)
    for r in range(R):
        oc_ref[0, :, r * LANE:(r + 1) * LANE] = oc[r * tq:(r + 1) * tq].astype(oc_ref.dtype)

    psum = p[0:tq]
    for r in range(1, R):
        psum = psum + p[r * tq:(r + 1) * tq]
    hi = psum.astype(BF16)
    lo = (psum - hi.astype(F32)).astype(BF16)
    ov = ov_ref[...]
    imp = jnp.dot(hi, ov, preferred_element_type=F32) + jnp.dot(lo, ov, preferred_element_type=F32)
    nb = imp.shape[1]
    j = lax.broadcasted_iota(jnp.int32, (tq, nb), 1)
    jf = j.astype(F32)
    cur = _div_pow2(i * tq + lax.broadcasted_iota(jnp.int32, (tq, nb), 0), NSA_SEL_BLOCK)
    forced = jnp.logical_or(j == 0, jnp.logical_or(j == cur, j == cur - 1))
    x = jnp.where(j > cur, NEG_INF, imp + FORCE_BONUS * forced.astype(F32))
    sel = jnp.zeros((tq, nb), F32)
    for _ in range(n_top):
        mx = jnp.max(x, axis=1, keepdims=True)
        first = jnp.min(jnp.where(x == mx, jf, float(nb)), axis=1, keepdims=True)
        hit = jf == first
        sel = jnp.where(jnp.logical_and(hit, mx > 0.5 * NEG_INF), 1.0, sel)
        x = jnp.where(hit, -jnp.inf, x)
    nm_ref[0] = jnp.where(sel > 0.5, 0.0, NEG_INF).astype(nm_ref.dtype)


def cmp_attention_select(q, k_cmp, v_cmp, overlap, *, n_cmp, n_top, tq=128):
    B, S, _ = q.shape
    G, R = NSA_KV_GROUPS, NSA_R
    NCP, NB = overlap.shape
    tq = _pick(S, tq)
    kern = functools.partial(_cmp_select_kernel, R=R, tq=tq, n_cmp=n_cmp, n_top=n_top)
    return pl.pallas_call(
        kern,
        grid=(B, G, S // tq),
        in_specs=[
            pl.BlockSpec((1, tq, R * LANE), lambda b, g, i: (b, i, g)),
            pl.BlockSpec((1, 1, NCP, LANE), lambda b, g, i: (b, g, 0, 0)),
            pl.BlockSpec((1, 1, NCP, LANE), lambda b, g, i: (b, g, 0, 0)),
            pl.BlockSpec((NCP, NB), lambda b, g, i: (0, 0)),
        ],
        out_specs=[
            pl.BlockSpec((1, tq, R * LANE), lambda b, g, i: (b, i, g)),
            pl.BlockSpec((1, tq, NB), lambda b, g, i: (b, i, g)),
        ],
        out_shape=[
            jax.ShapeDtypeStruct((B, S, G * R * LANE), F32),
            jax.ShapeDtypeStruct((B, S, G * NB), BF16),
        ],
        compiler_params=_params("parallel", "parallel", "arbitrary"),
        name="cmp_attention_select",
    )(q, k_cmp, v_cmp, overlap)


def _rope_tables(pos, dim):
    inv = 1.0 / (ROPE_THETA ** (jnp.arange(0, dim, 2, dtype=F32) / dim))
    ang = pos.astype(F32)[..., None] * inv
    return jnp.cos(ang), jnp.sin(ang)


def _apply_rope(x, cos, sin):
    c, s = cos[..., None, :], sin[..., None, :]
    x1, x2 = jnp.split(x, 2, axis=-1)
    return jnp.concatenate([x1 * c - x2 * s, x2 * c + x1 * s], axis=-1)


def nsa_mixer(x, B, S, positions, norm_w, w_in, cmp_pe, cmp_w1, cmp_w2, w_out):
    T, D = x.shape
    H, G, R, dk = NSA_HEADS, NSA_KV_GROUPS, NSA_R, NSA_HEAD_DIM
    q_cols, kv_cols = H * dk, NSA_N_BRANCHES * 2 * G * dk
    n_gate = NSA_N_BRANCHES * H
    q_scale = dk ** -0.5 * LOG2E

    w_qkv = w_in[:, :q_cols + kv_cols].astype(BF16)
    w_gate = jnp.pad(w_in[:, q_cols + kv_cols:], ((0, 0), (0, LANE - n_gate))).astype(BF16)
    proj = norm_matmul(x, norm_w, w_qkv)
    gates = jax.nn.sigmoid(norm_matmul(x, norm_w, w_gate)[:, :n_gate]).reshape(T, H, NSA_N_BRANCHES)

    cos, sin = _rope_tables(positions.reshape(T), dk)
    q = (_apply_rope(proj[:, :q_cols].reshape(T, H, dk), cos, sin) * q_scale).astype(BF16).reshape(B, S, q_cols)
    kv = proj[:, q_cols:].reshape(T, NSA_N_BRANCHES, 2, G, dk)
    k_slc = _apply_rope(kv[:, 1, 0], cos, sin).astype(BF16).reshape(B, S, G * dk)
    v_slc = kv[:, 1, 1].astype(BF16).reshape(B, S, G * dk)
    k_win = _apply_rope(kv[:, 2, 0], cos, sin).astype(BF16).reshape(B, S, G * dk)
    v_win = kv[:, 2, 1].astype(BF16).reshape(B, S, G * dk)

    ncp = S // NSA_CMP_STRIDE
    n_cmp = (S - NSA_CMP_LEN) // NSA_CMP_STRIDE + 1
    slab = NSA_CMP_STRIDE * dk
    xc = kv[:, 0].reshape(B, S, 2, G, dk).transpose(2, 0, 3, 1, 4).reshape(2, B, G, ncp, slab)
    hidden = cmp_w1.shape[-1]
    kvc = compress_tokens(xc, cmp_pe.reshape(2, 2, slab), cmp_w1.reshape(2, 2, slab, hidden).astype(BF16),
                          cmp_w2.astype(BF16))
    cmp_end = jnp.minimum(jnp.arange(ncp) * NSA_CMP_STRIDE + NSA_CMP_LEN - 1, S - 1)
    cos_c, sin_c = _rope_tables(positions[:, cmp_end], dk)
    k_cmp = _apply_rope(kvc[0].transpose(0, 2, 1, 3), cos_c, sin_c).transpose(0, 2, 1, 3).astype(BF16)
    v_cmp = kvc[1].astype(BF16)

    n_sel = S // NSA_SEL_BLOCK
    nb = -(-n_sel // LANE) * LANE
    blk_start = jnp.arange(ncp) * NSA_CMP_STRIDE
    sel_start = jnp.arange(nb) * NSA_SEL_BLOCK
    overlap = ((blk_start[:, None] <= sel_start[None, :] + NSA_SEL_BLOCK - 1)
               & (blk_start[:, None] + NSA_CMP_LEN - 1 >= sel_start[None, :])
               & (jnp.arange(ncp)[:, None] < n_cmp) & (jnp.arange(nb)[None, :] < n_sel)).astype(BF16)
    o_cmp, neg_mask = cmp_attention_select(q, k_cmp, v_cmp, overlap, n_cmp=n_cmp, n_top=min(NSA_N_SELECT, n_sel))

    blk_onehot = (jnp.arange(S)[:, None] // NSA_SEL_BLOCK % LANE == jnp.arange(LANE)[None, :]).astype(BF16)[None]
    o_slc = flash_attention(q, neg_mask, k_slc, blk_onehot, v_slc, n_groups=G, R=R, E=nb // LANE,
                            km_col=lambda g: g, v_col=lambda g: g, kx_batched=False, mode="causal",
                            tq=256, tk=512, depth=2, keys_per_version=LANE * NSA_SEL_BLOCK)
    o_win = flash_attention(q, None, k_win, None, v_win, n_groups=G, R=R, E=0,
                            km_col=lambda g: g, v_col=lambda g: g, kx_batched=False, mode="window",
                            window=NSA_WINDOW, tq=256, tk=256)

    o = (gates[:, :, 0:1] * o_cmp.reshape(T, H, dk) + gates[:, :, 1:2] * o_slc.reshape(T, H, dk)
         + gates[:, :, 2:3] * o_win.reshape(T, H, dk))
    return matmul_residual(o.reshape(T, H * dk).astype(BF16), w_out.astype(BF16), x)


def mla_mixer(x, B, S, positions, norm_w, w_in, q_norm_w, kv_norm_w, w_uq, w_ukv, w_out):
    T, D = x.shape
    H = MLA_HEADS
    q_rank, kv_rank = q_norm_w.shape[0], kv_norm_w.shape[0]
    assert q_rank == kv_rank and q_rank % LANE == 0
    pad_r = LANE - MLA_ROPE_DIM

    w_in_p = jnp.pad(w_in, ((0, 0), (0, pad_r))).astype(BF16)
    proj = norm_matmul(x, norm_w, w_in_p)
    w_uq3 = w_uq.reshape(q_rank, H, MLA_QK_DIM)
    w_uq_p = jnp.concatenate([w_uq3[:, :, :MLA_NOPE_DIM].reshape(q_rank, H * MLA_NOPE_DIM),
                              w_uq3[:, :, MLA_NOPE_DIM:].reshape(q_rank, H * MLA_ROPE_DIM)], axis=1).astype(BF16)
    qf = norm_matmul(proj, q_norm_w, w_uq_p, x_col_block=0)
    kvb = norm_matmul(proj, kv_norm_w, w_ukv.astype(BF16), x_col_block=1, out_dtype=BF16)

    cos, sin = _rope_tables(positions.reshape(T), MLA_ROPE_DIM)
    q_scale = MLA_QK_DIM ** -0.5 * LOG2E
    q_nope = (qf[:, :H * MLA_NOPE_DIM] * q_scale).astype(BF16).reshape(B, S, H * MLA_NOPE_DIM)
    q_rope = _apply_rope(qf[:, H * MLA_NOPE_DIM:].reshape(T, H, MLA_ROPE_DIM), cos, sin) * q_scale
    q_rope = jnp.pad(q_rope, ((0, 0), (0, 0), (0, pad_r))).astype(BF16).reshape(B, S, H * LANE)
    k_r = proj[:, 2 * q_rank:2 * q_rank + MLA_ROPE_DIM]
    k_rope = _apply_rope(k_r[:, None, :], cos, sin)[:, 0]
    k_rope = jnp.pad(k_rope, ((0, 0), (0, pad_r))).astype(BF16).reshape(B, S, LANE)

    o = flash_attention(q_nope, q_rope, kvb.reshape(B, S, 2 * H * LANE), k_rope, kvb.reshape(B, S, 2 * H * LANE),
                        n_groups=H, R=1, E=1, km_col=lambda h: 2 * h, v_col=lambda h: 2 * h + 1, kx_batched=True,
                        mode="causal", tq=512, tk=512, depth=4, out_dtype=BF16)
    return matmul_residual(o.reshape(T, H * MLA_V_DIM), w_out.astype(BF16), x)


def kernel(x, positions, ffn_norm_w, ffn_w_in, ffn_w_out, mix_norm_w, nsa_w_in, nsa_cmp_pe, nsa_cmp_w1, nsa_cmp_w2,
           nsa_w_out, mla_w_in, mla_q_norm_w, mla_kv_norm_w, mla_w_uq, mla_w_ukv, mla_w_out, final_norm_w):
    B, S, D = x.shape
    depth = ffn_norm_w.shape[0]
    n_mixers = 2
    h = x.reshape(B * S, D)
    for i in range(depth):
        h = ffn_half_step(h, ffn_norm_w[i, 0], ffn_w_in[i, 0].astype(BF16), ffn_w_out[i, 0].astype(BF16))
        j = i // n_mixers
        if i % n_mixers == 0:
            h = nsa_mixer(h, B, S, positions, mix_norm_w[i], nsa_w_in[j], nsa_cmp_pe[j], nsa_cmp_w1[j],
                          nsa_cmp_w2[j], nsa_w_out[j])
        else:
            h = mla_mixer(h, B, S, positions, mix_norm_w[i], mla_w_in[j], mla_q_norm_w[j], mla_kv_norm_w[j],
                          mla_w_uq[j], mla_w_ukv[j], mla_w_out[j])
        last = i == depth - 1
        h = ffn_half_step(h, ffn_norm_w[i, 1], ffn_w_in[i, 1].astype(BF16), ffn_w_out[i, 1].astype(BF16),
                          final_w=final_norm_w if last else None)
    return h.reshape(B, S, D)
```

```python
import functools
import math

import jax
import jax.numpy as jnp
from jax import lax
from jax.experimental import pallas as pl
from jax.experimental.pallas import tpu as pltpu

F32 = jnp.float32
BF16 = jnp.bfloat16

LANE = 128
BF16_ROWS = 16
RMS_EPS = 1e-6
ROPE_THETA = 10000.0
NEG_INF = -1e30
FORCE_BONUS = 1e9
LOG2E = math.log2(math.e)
VMEM_LIMIT = 56 * 1024 * 1024

NSA_HEADS = 16
NSA_HEAD_DIM = 128
NSA_KV_GROUPS = 4
NSA_R = NSA_HEADS // NSA_KV_GROUPS
NSA_N_BRANCHES = 3
NSA_CMP_LEN = 32
NSA_CMP_STRIDE = 16
NSA_SEL_BLOCK = 64
NSA_N_SELECT = 16
NSA_WINDOW = 512

MLA_HEADS = 16
MLA_NOPE_DIM = 128
MLA_ROPE_DIM = 64
MLA_V_DIM = 128
MLA_QK_DIM = MLA_NOPE_DIM + MLA_ROPE_DIM


def _pick(n, pref):
    if n <= pref:
        return n
    t = (pref // LANE) * LANE
    while t > LANE and n % t:
        t -= LANE
    assert n % t == 0, (n, pref)
    return t


def _div_pow2(x, d):
    assert d & (d - 1) == 0, d
    return jnp.right_shift(x, d.bit_length() - 1)


def _mod_pow2(x, d):
    assert d & (d - 1) == 0, d
    return jnp.bitwise_and(x, d - 1)


def _params(*sem):
    return pltpu.CompilerParams(dimension_semantics=sem, vmem_limit_bytes=VMEM_LIMIT)


def _rotate_half(y, c2, s2):
    return y * c2 + pltpu.roll(y, LANE // 2, axis=1) * s2


def _rms_rows(x, w):
    ms = jnp.mean(x * x, axis=-1, keepdims=True)
    return x * lax.rsqrt(ms + RMS_EPS) * w


def _proj_kernel(*refs, has_rope, rope_mask, scale_mask, scale, sigmoid):
    if has_rope:
        x_ref, nw_ref, w_ref, c2_ref, s2_ref, o_ref, xn_scr = refs
    else:
        x_ref, nw_ref, w_ref, o_ref, xn_scr = refs
    j = pl.program_id(1)

    @pl.when(j == 0)
    def _():
        xn_scr[...] = _rms_rows(x_ref[...], nw_ref[...]).astype(BF16)

    y = jnp.dot(xn_scr[...], w_ref[...], preferred_element_type=F32)
    if has_rope:
        rope_on = jnp.bitwise_and(lax.shift_right_logical(jnp.int32(rope_mask), j), 1) == 1
        factor = jnp.where(jnp.bitwise_and(lax.shift_right_logical(jnp.int32(scale_mask), j), 1) == 1, scale, 1.0)
        c2 = jnp.where(rope_on, c2_ref[...], 1.0) * factor
        s2 = jnp.where(rope_on, s2_ref[...], 0.0) * factor
        for h in range(y.shape[1] // LANE):
            o_ref[:, h * LANE:(h + 1) * LANE] = _rotate_half(y[:, h * LANE:(h + 1) * LANE], c2, s2).astype(o_ref.dtype)
    else:
        if sigmoid:
            y = jax.nn.sigmoid(y)
        o_ref[...] = y.astype(o_ref.dtype)


def proj(x, norm_w, w, *, n_blocks, tn, col_map=None, x_col_block=0, rope=None, rope_mask=0, scale_mask=0, scale=1.0,
         sigmoid=False, out_dtype=F32, tm=1024):
    T = x.shape[0]
    K = w.shape[0]
    tm = _pick(T, tm)
    col_map = col_map or (lambda j: j)
    has_rope = rope is not None
    assert has_rope or scale_mask == 0
    in_specs = [
        pl.BlockSpec((tm, K), lambda i, j: (i, x_col_block)),
        pl.BlockSpec((1, K), lambda i, j: (0, 0)),
        pl.BlockSpec((K, tn), lambda i, j: (0, col_map(j))),
    ]
    args = [x, norm_w.reshape(1, K).astype(F32), w]
    if has_rope:
        in_specs += [pl.BlockSpec((tm, LANE), lambda i, j: (i, 0))] * 2
        args += list(rope)
    kern = functools.partial(_proj_kernel, has_rope=has_rope, rope_mask=rope_mask, scale_mask=scale_mask,
                             scale=scale, sigmoid=sigmoid)
    return pl.pallas_call(
        kern,
        grid=(T // tm, n_blocks),
        in_specs=in_specs,
        out_specs=pl.BlockSpec((tm, tn), lambda i, j: (i, j)),
        out_shape=jax.ShapeDtypeStruct((T, n_blocks * tn), out_dtype),
        scratch_shapes=[pltpu.VMEM((tm, K), BF16)],
        compiler_params=_params("parallel", "arbitrary"),
        name="proj",
    )(*args)


def _matmul_residual_kernel(*refs):
    *a_refs, w_ref, r_ref, o_ref = refs
    a = a_refs[0][...]
    if len(a_refs) > 1:
        a = a.astype(F32)
        for a_ref in a_refs[1:]:
            a = a + a_ref[...].astype(F32)
        a = a.astype(BF16)
    o_ref[...] = r_ref[...] + jnp.dot(a, w_ref[...], preferred_element_type=F32)


def matmul_residual(a_list, w, res, *, tm=1024, tn=1024):
    T, K = a_list[0].shape
    N = w.shape[1]
    tm, tn = _pick(T, tm), _pick(N, tn)
    return pl.pallas_call(
        _matmul_residual_kernel,
        grid=(T // tm, N // tn),
        in_specs=[pl.BlockSpec((tm, K), lambda i, j: (i, 0))] * len(a_list) + [
            pl.BlockSpec((K, tn), lambda i, j: (0, j)),
            pl.BlockSpec((tm, tn), lambda i, j: (i, j)),
        ],
        out_specs=pl.BlockSpec((tm, tn), lambda i, j: (i, j)),
        out_shape=jax.ShapeDtypeStruct((T, N), F32),
        compiler_params=_params("parallel", "arbitrary"),
        name="matmul_residual",
    )(*a_list, w, res)


def _ffn_kernel(x_ref, nw_ref, wg_ref, wu_ref, wo_ref, fw_ref, o_ref, xn_scr, acc_scr, *, final_norm):
    f = pl.program_id(1)

    @pl.when(f == 0)
    def _():
        xn_scr[...] = _rms_rows(x_ref[...], nw_ref[...]).astype(BF16)
        acc_scr[...] = jnp.zeros_like(acc_scr)

    xn = xn_scr[...]
    g = jnp.dot(xn, wg_ref[...], preferred_element_type=F32)
    u = jnp.dot(xn, wu_ref[...], preferred_element_type=F32)
    h = (g * jax.nn.sigmoid(g) * u).astype(BF16)
    acc_scr[...] += jnp.dot(h, wo_ref[...], preferred_element_type=F32)

    @pl.when(f == pl.num_programs(1) - 1)
    def _():
        y = x_ref[...] + 0.5 * acc_scr[...]
        if final_norm:
            y = _rms_rows(y, fw_ref[...])
        o_ref[...] = y


def ffn_half_step(x, norm_w, w_in, w_out, final_w=None, *, tm=512, tf=512):
    T, D = x.shape
    DFF = w_out.shape[0]
    tm, tf = _pick(T, tm), _pick(DFF, tf)
    nf = DFF // tf
    final_norm = final_w is not None
    fw = (final_w if final_norm else norm_w).reshape(1, D).astype(F32)
    return pl.pallas_call(
        functools.partial(_ffn_kernel, final_norm=final_norm),
        grid=(T // tm, nf),
        in_specs=[
            pl.BlockSpec((tm, D), lambda i, f: (i, 0)),
            pl.BlockSpec((1, D), lambda i, f: (0, 0)),
            pl.BlockSpec((D, tf), lambda i, f: (0, f)),
            pl.BlockSpec((D, tf), lambda i, f: (0, f + nf)),
            pl.BlockSpec((tf, D), lambda i, f: (f, 0)),
            pl.BlockSpec((1, D), lambda i, f: (0, 0)),
        ],
        out_specs=pl.BlockSpec((tm, D), lambda i, f: (i, 0)),
        out_shape=jax.ShapeDtypeStruct((T, D), F32),
        scratch_shapes=[pltpu.VMEM((tm, D), BF16), pltpu.VMEM((tm, D), F32)],
        compiler_params=_params("parallel", "arbitrary"),
        name="ffn_half_step",
    )(x, norm_w.reshape(1, D).astype(F32), w_in, w_in, w_out, fw)


def _flash_kernel(*refs, R, E, has_kx, gate_cols, tq, tk, mode, window, chunks_per_version, depth):
    it = iter(refs)
    qm_ref = next(it)
    qx_ref = next(it) if E else None
    km_ref = next(it)
    kx_ref = next(it) if has_kx else None
    v_ref = next(it)
    gate_ref = next(it) if gate_cols else None
    o_ref = next(it)
    q_scr, m_scr, acc_scr, s_scr = it
    M = R * tq
    i = pl.program_id(2)

    for e in range(max(E, 1)):
        for r in range(R):
            q_scr[e, r * tq:(r + 1) * tq, 0:LANE] = qm_ref[0, :, r * LANE:(r + 1) * LANE]
            if E:
                q_scr[e, r * tq:(r + 1) * tq, LANE:2 * LANE] = qx_ref[0, :, e * LANE:(e + 1) * LANE]
    m_scr[...] = jnp.full_like(m_scr, NEG_INF)
    acc_scr[...] = jnp.zeros_like(acc_scr)
    ones_col = jnp.where(lax.broadcasted_iota(jnp.int32, (tk, LANE), 1) == 0, 1.0, 0.0).astype(BF16)

    def scores(c):
        k0 = pl.multiple_of(c * tk, tk)
        k = km_ref[0, pl.ds(k0, tk), :]
        if has_kx:
            k = jnp.concatenate([k, kx_ref[0, pl.ds(k0, tk), :]], axis=1)
        q = q_scr[c // chunks_per_version] if E > 1 else q_scr[0]
        return lax.dot_general(q, k, (((1,), (1,)), ((), ())), preferred_element_type=F32).astype(BF16)

    def softmax_pv(c, s, masked):
        k0 = pl.multiple_of(c * tk, tk)
        if masked:
            row = lax.broadcasted_iota(jnp.int32, (M, tk), 0)
            col = lax.broadcasted_iota(jnp.int32, (M, tk), 1)
            t = i * tq + (_mod_pow2(row, tq) if R > 1 else row)
            kpos = k0 + col
            ok = kpos <= t
            if mode == "window":
                ok = jnp.logical_and(ok, t - kpos < window)
            s = jnp.where(ok, s, NEG_INF)
        m_prev = m_scr[...]
        m_new = jnp.maximum(m_prev, jnp.max(s, axis=1, keepdims=True).astype(F32))
        alpha = jnp.exp2(m_prev - m_new)
        p = jnp.exp2(s - pltpu.repeat(m_new.astype(BF16), tk // LANE, axis=1))
        v1 = jnp.concatenate([v_ref[0, pl.ds(k0, tk), :], ones_col], axis=1)
        acc_scr[...] = pltpu.repeat(alpha, 2, axis=1) * acc_scr[...] + jnp.dot(p, v1, preferred_element_type=F32)
        m_scr[...] = m_new

    if mode == "causal":
        n_full = (i * tq) // tk
        bufs = [s_scr.at[u] for u in range(depth)]
        bufs[0][...] = scores(0)

        def group(j, carry):
            for u in range(depth):
                bufs[(u + 1) % depth][...] = scores(depth * j + u + 1)
                softmax_pv(depth * j + u, bufs[u][...], False)
            return carry

        n_groups = n_full // depth
        lax.fori_loop(0, n_groups, group, 0)

        def tail(c, carry):
            softmax_pv(c, bufs[0][...], False)
            bufs[0][...] = scores(c + 1)
            return carry

        lax.fori_loop(depth * n_groups, n_full, tail, 0)
        softmax_pv(n_full, bufs[0][...], True)
    else:
        lo = jnp.maximum(i * tq - (window - 1), 0) // tk
        hi = ((i + 1) * tq - 1) // tk

        def body(c, carry):
            softmax_pv(c, scores(c), True)
            return carry

        lax.fori_loop(lo, hi + 1, body, 0)

    out = acc_scr[:, 0:LANE] / acc_scr[:, LANE:LANE + 1]
    for r in range(R):
        o_r = out[r * tq:(r + 1) * tq]
        if gate_cols:
            o_r = o_r * gate_ref[0, :, gate_cols[r]:gate_cols[r] + 1]
        o_ref[0, :, r * LANE:(r + 1) * LANE] = o_r.astype(o_ref.dtype)


def flash_attention(qm, qx, km, kx, v, gate=None, *, n_groups, R, E, qm_col, qx_col=None, km_col, v_col, kx_batched,
                    gate_cols=None, mode, window=0, tq, tk, depth=2, keys_per_version=None):
    B, S, _ = qm.shape
    tq, tk = _pick(S, tq), _pick(S, tk)
    assert tk % tq == 0
    has_kx = kx is not None
    dk = 2 * LANE if has_kx else LANE
    cpv = (keys_per_version // tk) if keys_per_version else 1
    M = R * tq
    in_specs = [pl.BlockSpec((1, tq, R * LANE), lambda b, g, i: (b, i, qm_col(g)))]
    args = [qm]
    if E:
        in_specs.append(pl.BlockSpec((1, tq, E * LANE), lambda b, g, i: (b, i, qx_col(g))))
        args.append(qx)
    in_specs.append(pl.BlockSpec((1, S, LANE), lambda b, g, i: (b, 0, km_col(g))))
    args.append(km)
    if has_kx:
        in_specs.append(pl.BlockSpec((1, S, LANE), lambda b, g, i: (b if kx_batched else 0, 0, 0)))
        args.append(kx)
    in_specs.append(pl.BlockSpec((1, S, LANE), lambda b, g, i: (b, 0, v_col(g))))
    args.append(v)
    if gate_cols:
        in_specs.append(pl.BlockSpec((1, tq, LANE), lambda b, g, i: (b, i, g)))
        args.append(gate)
    kern = functools.partial(_flash_kernel, R=R, E=E, has_kx=has_kx, gate_cols=gate_cols, tq=tq, tk=tk, mode=mode,
                             window=window, chunks_per_version=cpv, depth=depth)
    return pl.pallas_call(
        kern,
        grid=(B, n_groups, S // tq),
        in_specs=in_specs,
        out_specs=pl.BlockSpec((1, tq, R * LANE), lambda b, g, i: (b, i, g)),
        out_shape=jax.ShapeDtypeStruct((B, S, n_groups * R * LANE), BF16),
        scratch_shapes=[
            pltpu.VMEM((max(E, 1), M, dk), BF16),
            pltpu.VMEM((M, LANE), F32),
            pltpu.VMEM((M, 2 * LANE), F32),
            pltpu.VMEM((depth, M, tk), BF16),
        ],
        compiler_params=_params("parallel", "parallel", "arbitrary"),
        name="flash_" + mode + ("_x%d" % E if E else ""),
    )(*args)


def _compress_kernel(tok_ref, pe_ref, w1_ref, w2_ref, c2_ref, s2_ref, o_ref, *, ncp):
    half = NSA_CMP_STRIDE
    a = b = None
    for l in range(half):
        x = tok_ref[0, pl.ds(l, ncp, stride=half), :]
        da = jnp.dot((x + pe_ref[0, l:l + 1, :]).astype(BF16), w1_ref[0, l], preferred_element_type=F32)
        db = jnp.dot((x + pe_ref[0, half + l:half + l + 1, :]).astype(BF16), w1_ref[0, half + l],
                     preferred_element_type=F32)
        a = da if a is None else a + da
        b = db if b is None else b + db
    h = jax.nn.gelu(a + pltpu.roll(b, ncp - 1, axis=0))
    y = jnp.dot(h.astype(BF16), w2_ref[0], preferred_element_type=F32)
    is_key = pl.program_id(0) == 0
    c2 = jnp.where(is_key, c2_ref[0], 1.0)
    s2 = jnp.where(is_key, s2_ref[0], 0.0)
    o_ref[0, 0, 0] = _rotate_half(y, c2, s2).astype(o_ref.dtype)


def compress_tokens(tok, pe, w1, w2, c2, s2):
    B, S, _ = tok.shape
    G, dk = NSA_KV_GROUPS, NSA_HEAD_DIM
    ncp = S // NSA_CMP_STRIDE
    hidden = w2.shape[1]
    return pl.pallas_call(
        functools.partial(_compress_kernel, ncp=ncp),
        grid=(2, B, G),
        in_specs=[
            pl.BlockSpec((1, S, dk), lambda s, b, g: (b, 0, s * G + g)),
            pl.BlockSpec((1, NSA_CMP_LEN, dk), lambda s, b, g: (s, 0, 0)),
            pl.BlockSpec((1, NSA_CMP_LEN, dk, hidden), lambda s, b, g: (s, 0, 0, 0)),
            pl.BlockSpec((1, hidden, dk), lambda s, b, g: (s, 0, 0)),
            pl.BlockSpec((1, ncp, dk), lambda s, b, g: (b, 0, 0)),
            pl.BlockSpec((1, ncp, dk), lambda s, b, g: (b, 0, 0)),
        ],
        out_specs=pl.BlockSpec((1, 1, 1, ncp, dk), lambda s, b, g: (s, b, g, 0, 0)),
        out_shape=jax.ShapeDtypeStruct((2, B, G, ncp, dk), BF16),
        compiler_params=_params("parallel", "parallel", "parallel"),
        name="compress_tokens",
    )(tok, pe, w1, w2, c2, s2)


def _cmp_select_kernel(q_ref, kc_ref, vc_ref, ovt_ref, gate_ref, oc_ref, nm_ref, *, R, tq, n_sub, n_cmp, n_top):
    for u in range(n_sub):
        rows = slice(u * tq, (u + 1) * tq)
        _cmp_select_tile(q_ref.at[0, rows], kc_ref, vc_ref, ovt_ref, gate_ref.at[0, rows], oc_ref.at[0, rows],
                         nm_ref.at[0, rows], (pl.program_id(2) * n_sub + u) * tq, R=R, tq=tq, n_cmp=n_cmp, n_top=n_top)


def _cmp_select_tile(q_ref, kc_ref, vc_ref, ovt_ref, gate_ref, oc_ref, nm_ref, t0, *, R, tq, n_cmp, n_top):
    M = R * tq
    q = jnp.concatenate([q_ref[:, r * LANE:(r + 1) * LANE] for r in range(R)], axis=0)
    kc = kc_ref[0, 0, 0]
    ncp = kc.shape[0]
    s = lax.dot_general(q, kc, (((1,), (1,)), ((), ())), preferred_element_type=F32)
    t = t0 + _mod_pow2(lax.broadcasted_iota(jnp.int32, (M, 1), 0), tq)
    lim = jnp.minimum(_div_pow2(t - (NSA_CMP_LEN - 1), NSA_CMP_STRIDE) + 1, n_cmp)
    valid = lax.broadcasted_iota(jnp.int32, (M, ncp), 1) < lim
    sm = jnp.where(valid, s, NEG_INF)
    e = jnp.exp2((sm - jnp.max(sm, axis=1, keepdims=True)).astype(BF16))
    ones_col = jnp.where(lax.broadcasted_iota(jnp.int32, (ncp, LANE), 1) == 0, 1.0, 0.0).astype(BF16)
    oc = jnp.dot(e, jnp.concatenate([vc_ref[0, 0, 0], ones_col], axis=1), preferred_element_type=F32)
    oc = oc[:, 0:LANE] * jnp.where(lim > 0, 1.0 / oc[:, LANE:LANE + 1], 0.0)
    for r in range(R):
        gate = gate_ref[:, NSA_N_BRANCHES * r:NSA_N_BRANCHES * r + 1]
        oc_ref[:, r * LANE:(r + 1) * LANE] = (oc[r * tq:(r + 1) * tq] * gate).astype(oc_ref.dtype)

    nb = ovt_ref.shape[0] - BF16_ROWS
    imp_un = lax.dot_general(ovt_ref[...], e, (((1,), (1,)), ((), ())), preferred_element_type=F32)
    t_l = t0 + _mod_pow2(lax.broadcasted_iota(jnp.int32, (1, M), 1), tq)
    lim_l = jnp.minimum(_div_pow2(t_l - (NSA_CMP_LEN - 1), NSA_CMP_STRIDE) + 1, n_cmp)
    imp_h = imp_un[0:nb] * jnp.where(lim_l > 0, 1.0 / imp_un[nb:nb + 1], 0.0)
    imp = imp_h[:, 0:tq]
    for r in range(1, R):
        imp = imp + imp_h[:, r * tq:(r + 1) * tq]
    j = lax.broadcasted_iota(jnp.int32, (nb, tq), 0)
    jf = j.astype(F32)
    cur = _div_pow2(t0 + lax.broadcasted_iota(jnp.int32, (nb, tq), 1), NSA_SEL_BLOCK)
    forced = jnp.logical_or(j == 0, jnp.logical_or(j == cur, j == cur - 1))
    x = jnp.where(j > cur, NEG_INF, imp + FORCE_BONUS * forced.astype(F32))
    for _ in range(n_top):
        mx = jnp.max(x, axis=0, keepdims=True)
        first = jnp.min(jnp.where(x == mx, jf, float(nb)), axis=0, keepdims=True)
        x = jnp.where(jf == first, jnp.where(mx > 0.5 * NEG_INF, -jnp.inf, NEG_INF), x)
    nm_ref[...] = jnp.where(x == -jnp.inf, 0.0, NEG_INF).T.astype(nm_ref.dtype)


def cmp_attention_select(q, kv_cmp, overlap, gate, *, n_cmp, n_top, tq=128, n_sub=2):
    B, S, _ = q.shape
    G, R = NSA_KV_GROUPS, NSA_R
    NB, NCP = overlap.shape[0] - BF16_ROWS, overlap.shape[1]
    tq = _pick(S, tq)
    tb = n_sub * tq
    assert S % tb == 0
    kern = functools.partial(_cmp_select_kernel, R=R, tq=tq, n_sub=n_sub, n_cmp=n_cmp, n_top=n_top)
    return pl.pallas_call(
        kern,
        grid=(B, G, S // tb),
        in_specs=[
            pl.BlockSpec((1, tb, R * LANE), lambda b, g, i: (b, i, g)),
            pl.BlockSpec((1, 1, 1, NCP, LANE), lambda b, g, i: (0, b, g, 0, 0)),
            pl.BlockSpec((1, 1, 1, NCP, LANE), lambda b, g, i: (1, b, g, 0, 0)),
            pl.BlockSpec((NB + BF16_ROWS, NCP), lambda b, g, i: (0, 0)),
            pl.BlockSpec((1, tb, LANE), lambda b, g, i: (b, i, g)),
        ],
        out_specs=[
            pl.BlockSpec((1, tb, R * LANE), lambda b, g, i: (b, i, g)),
            pl.BlockSpec((1, tb, NB), lambda b, g, i: (b, i, g)),
        ],
        out_shape=[
            jax.ShapeDtypeStruct((B, S, G * R * LANE), BF16),
            jax.ShapeDtypeStruct((B, S, G * NB), BF16),
        ],
        compiler_params=_params("parallel", "parallel", "arbitrary"),
        name="cmp_attention_select",
    )(q, kv_cmp, kv_cmp, overlap, gate)


def _rope_tables(pos, dim):
    inv = 1.0 / (ROPE_THETA ** (jnp.arange(0, dim, 2, dtype=F32) / dim))
    ang = pos.astype(F32)[..., None] * inv
    pad = [(0, 0)] * (ang.ndim - 1) + [(0, (LANE - dim) // 2)]
    c, s = jnp.pad(jnp.cos(ang), pad), jnp.pad(jnp.sin(ang), pad)
    return jnp.concatenate([c, c], axis=-1), jnp.concatenate([-s, s], axis=-1)


def _spread_rope_cols(w):
    dim = w.shape[-1]
    pad = [(0, 0)] * (w.ndim - 1) + [(0, (LANE - dim) // 2)]
    return jnp.concatenate([jnp.pad(w[..., :dim // 2], pad), jnp.pad(w[..., dim // 2:], pad)], axis=-1)


def nsa_mixer(x, B, S, positions, norm_w, w_in, cmp_pe, cmp_w1, cmp_w2, w_out):
    T, D = x.shape
    H, G, R, dk = NSA_HEADS, NSA_KV_GROUPS, NSA_R, NSA_HEAD_DIM
    q_cols, kv_cols = H * dk, NSA_N_BRANCHES * 2 * G * dk
    grp = G * dk
    q_scale = dk ** -0.5 * LOG2E
    tables = _rope_tables(positions.reshape(T), dk)

    w_qkv = w_in[:, :q_cols + kv_cols].astype(BF16)
    nq = q_cols // grp
    qkv = proj(x, norm_w, w_qkv, n_blocks=nq + 4, tn=grp, col_map=lambda j: jnp.where(j < nq, j, j + 2),
               rope=tables, rope_mask=(1 << nq) - 1 | 0b0101 << nq, scale_mask=(1 << nq) - 1, scale=q_scale,
               out_dtype=BF16).reshape(B, S, q_cols + 4 * grp)
    cmp_tok = proj(x, norm_w, w_qkv, n_blocks=1, tn=2 * grp, col_map=lambda j: q_cols // (2 * grp)).reshape(B, S, 2 * grp)
    w_gate = w_in[:, q_cols + kv_cols:].reshape(D, G, R * NSA_N_BRANCHES)
    w_gate = jnp.pad(w_gate, ((0, 0), (0, 0), (0, LANE - R * NSA_N_BRANCHES))).reshape(D, G * LANE).astype(BF16)
    gates = proj(x, norm_w, w_gate, n_blocks=1, tn=G * LANE, sigmoid=True).reshape(B, S, G * LANE)

    ncp = S // NSA_CMP_STRIDE
    n_cmp = (S - NSA_CMP_LEN) // NSA_CMP_STRIDE + 1
    cmp_end = jnp.minimum(jnp.arange(ncp) * NSA_CMP_STRIDE + NSA_CMP_LEN - 1, S - 1)
    c2c, s2c = _rope_tables(positions[:, cmp_end], dk)
    hidden = cmp_w1.shape[-1]
    kv_cmp = compress_tokens(cmp_tok, cmp_pe, cmp_w1.reshape(2, NSA_CMP_LEN, dk, hidden).astype(BF16),
                             cmp_w2.astype(BF16), c2c, s2c)

    n_sel = S // NSA_SEL_BLOCK
    nb = -(-n_sel // LANE) * LANE
    blk_start = jnp.arange(ncp) * NSA_CMP_STRIDE
    sel_start = jnp.arange(nb) * NSA_SEL_BLOCK
    overlap = ((blk_start[None, :] <= sel_start[:, None] + NSA_SEL_BLOCK - 1)
               & (blk_start[None, :] + NSA_CMP_LEN - 1 >= sel_start[:, None])
               & (jnp.arange(ncp)[None, :] < n_cmp) & (jnp.arange(nb)[:, None] < n_sel)).astype(BF16)
    overlap = jnp.concatenate([overlap, jnp.ones((1, ncp), BF16), jnp.zeros((BF16_ROWS - 1, ncp), BF16)], axis=0)
    o_cmp, neg_mask = cmp_attention_select(qkv, kv_cmp, overlap, gates, n_cmp=n_cmp, n_top=min(NSA_N_SELECT, n_sel))

    blk_onehot = (jnp.arange(S)[:, None] // NSA_SEL_BLOCK % LANE == jnp.arange(LANE)[None, :]).astype(BF16)[None]
    kv0 = q_cols // dk
    o_slc = flash_attention(qkv, neg_mask, qkv, blk_onehot, qkv, gates, n_groups=G, R=R, E=nb // LANE,
                            qm_col=lambda g: g, qx_col=lambda g: g, km_col=lambda g: kv0 + g,
                            v_col=lambda g: kv0 + G + g, kx_batched=False,
                            gate_cols=tuple(NSA_N_BRANCHES * r + 1 for r in range(R)), mode="causal",
                            tq=512, tk=512, depth=2, keys_per_version=LANE * NSA_SEL_BLOCK)
    o_win = flash_attention(qkv, None, qkv, None, qkv, gates, n_groups=G, R=R, E=0,
                            qm_col=lambda g: g, km_col=lambda g: kv0 + 2 * G + g, v_col=lambda g: kv0 + 3 * G + g,
                            kx_batched=False, gate_cols=tuple(NSA_N_BRANCHES * r + 2 for r in range(R)),
                            mode="window", window=NSA_WINDOW, tq=512, tk=512)
    return matmul_residual([o.reshape(T, H * dk) for o in (o_cmp, o_slc, o_win)], w_out.astype(BF16), x)


def mla_mixer(x, B, S, positions, norm_w, w_in, q_norm_w, kv_norm_w, w_uq, w_ukv, w_out):
    T, D = x.shape
    H = MLA_HEADS
    q_rank, kv_rank = q_norm_w.shape[0], kv_norm_w.shape[0]
    assert q_rank == kv_rank and q_rank % LANE == 0
    q_scale = MLA_QK_DIM ** -0.5 * LOG2E
    tables = _rope_tables(positions.reshape(T), MLA_ROPE_DIM)

    latent = proj(x, norm_w, w_in[:, :2 * q_rank].astype(BF16), n_blocks=1, tn=2 * q_rank)
    w_kr = _spread_rope_cols(w_in[:, 2 * q_rank:]).astype(BF16)
    k_rope = proj(x, norm_w, w_kr, n_blocks=1, tn=LANE, rope=tables, rope_mask=1, out_dtype=BF16).reshape(B, S, LANE)
    w_uq3 = w_uq.reshape(q_rank, H, MLA_QK_DIM)
    w_uq_p = jnp.concatenate([w_uq3[:, :, :MLA_NOPE_DIM].reshape(q_rank, H * MLA_NOPE_DIM),
                              _spread_rope_cols(w_uq3[:, :, MLA_NOPE_DIM:]).reshape(q_rank, H * LANE)],
                             axis=1).astype(BF16)
    tn = 4 * LANE
    n_nope = H * MLA_NOPE_DIM // tn
    n_all = n_nope + H * LANE // tn
    q = proj(latent, q_norm_w, w_uq_p, n_blocks=n_all, tn=tn, x_col_block=0, rope=tables,
             rope_mask=(1 << n_all) - (1 << n_nope), scale_mask=(1 << n_all) - 1, scale=q_scale,
             out_dtype=BF16).reshape(B, S, n_all * tn)
    kvb = proj(latent, kv_norm_w, w_ukv.astype(BF16), n_blocks=w_ukv.shape[1] // (8 * LANE), tn=8 * LANE,
               x_col_block=1, out_dtype=BF16).reshape(B, S, 2 * H * LANE)

    o = flash_attention(q, q, kvb, k_rope, kvb, n_groups=H, R=1, E=1, qm_col=lambda h: h, qx_col=lambda h: H + h,
                        km_col=lambda h: 2 * h, v_col=lambda h: 2 * h + 1, kx_batched=True,
                        mode="causal", tq=1024, tk=1024, depth=2)
    return matmul_residual([o.reshape(T, H * MLA_V_DIM)], w_out.astype(BF16), x)


def kernel(x, positions, ffn_norm_w, ffn_w_in, ffn_w_out, mix_norm_w, nsa_w_in, nsa_cmp_pe, nsa_cmp_w1, nsa_cmp_w2,
           nsa_w_out, mla_w_in, mla_q_norm_w, mla_kv_norm_w, mla_w_uq, mla_w_ukv, mla_w_out, final_norm_w):
    B, S, D = x.shape
    depth = ffn_norm_w.shape[0]
    n_mixers = 2
    h = x.reshape(B * S, D)
    for i in range(depth):
        h = ffn_half_step(h, ffn_norm_w[i, 0], ffn_w_in[i, 0].astype(BF16), ffn_w_out[i, 0].astype(BF16))
        j = i // n_mixers
        if i % n_mixers == 0:
            h = nsa_mixer(h, B, S, positions, mix_norm_w[i], nsa_w_in[j], nsa_cmp_pe[j], nsa_cmp_w1[j],
                          nsa_cmp_w2[j], nsa_w_out[j])
        else:
            h = mla_mixer(h, B, S, positions, mix_norm_w[i], mla_w_in[j], mla_q_norm_w[j], mla_kv_norm_w[j],
                          mla_w_uq[j], mla_w_ukv[j], mla_w_out[j])
        last = i == depth - 1
        h = ffn_half_step(h, ffn_norm_w[i, 1], ffn_w_in[i, 1].astype(BF16), ffn_w_out[i, 1].astype(BF16),
                          final_w=final_norm_w if last else None)
    return h.reshape(B, S, D)
```

```python
import functools
import math

import jax
import jax.numpy as jnp
from jax import lax
from jax.experimental import pallas as pl
from jax.experimental.pallas import tpu as pltpu

F32 = jnp.float32
BF16 = jnp.bfloat16

LANE = 128
BF16_ROWS = 16
RMS_EPS = 1e-6
ROPE_THETA = 10000.0
NEG_INF = -1e30
MASK_PASS = 3e38
FORCE_BONUS = 1e9
LOG2E = math.log2(math.e)
VMEM_LIMIT = 56 * 1024 * 1024
FFN_VMEM_LIMIT = 61 * 1024 * 1024
FFN_SLAB = 64

NSA_HEADS = 16
NSA_HEAD_DIM = 128
NSA_KV_GROUPS = 4
NSA_R = NSA_HEADS // NSA_KV_GROUPS
NSA_N_BRANCHES = 3
NSA_CMP_LEN = 32
NSA_CMP_STRIDE = 16
NSA_SEL_BLOCK = 64
NSA_N_SELECT = 16
NSA_WINDOW = 512

MLA_HEADS = 16
MLA_NOPE_DIM = 128
MLA_ROPE_DIM = 64
MLA_V_DIM = 128
MLA_QK_DIM = MLA_NOPE_DIM + MLA_ROPE_DIM


def _pick(n, pref):
    if n <= pref:
        return n
    t = (pref // LANE) * LANE
    while t > LANE and n % t:
        t -= LANE
    assert n % t == 0, (n, pref)
    return t


def _div_pow2(x, d):
    assert d & (d - 1) == 0, d
    return jnp.right_shift(x, d.bit_length() - 1)


def _mod_pow2(x, d):
    assert d & (d - 1) == 0, d
    return jnp.bitwise_and(x, d - 1)


def _params(*sem, vmem=VMEM_LIMIT):
    return pltpu.CompilerParams(dimension_semantics=sem, vmem_limit_bytes=vmem)


def _rotate_half(y, c2, s2):
    return y * c2 + pltpu.roll(y, LANE // 2, axis=1) * s2


def _rms_rows(x, w):
    ms = jnp.mean(x * x, axis=-1, keepdims=True)
    return x * lax.rsqrt(ms + RMS_EPS) * w


def _proj_kernel(*refs, has_rope, rope_mask, scale_mask, scale, sigmoid):
    if has_rope:
        x_ref, nw_ref, w_ref, c2_ref, s2_ref, o_ref, xn_scr = refs
    else:
        x_ref, nw_ref, w_ref, o_ref, xn_scr = refs
    j = pl.program_id(1)

    @pl.when(j == 0)
    def _():
        xn_scr[...] = _rms_rows(x_ref[...], nw_ref[...]).astype(BF16)

    y = jnp.dot(xn_scr[...], w_ref[...], preferred_element_type=F32)
    if has_rope:
        rope_on = jnp.bitwise_and(lax.shift_right_logical(jnp.int32(rope_mask), j), 1) == 1
        factor = jnp.where(jnp.bitwise_and(lax.shift_right_logical(jnp.int32(scale_mask), j), 1) == 1, scale, 1.0)
        c2 = jnp.where(rope_on, c2_ref[...], 1.0) * factor
        s2 = jnp.where(rope_on, s2_ref[...], 0.0) * factor
        for h in range(y.shape[1] // LANE):
            o_ref[:, h * LANE:(h + 1) * LANE] = _rotate_half(y[:, h * LANE:(h + 1) * LANE], c2, s2).astype(o_ref.dtype)
    else:
        if sigmoid:
            y = jax.nn.sigmoid(y)
        o_ref[...] = y.astype(o_ref.dtype)


def proj(x, norm_w, w, *, n_blocks, tn, col_map=None, x_col_block=0, rope=None, rope_mask=0, scale_mask=0, scale=1.0,
         sigmoid=False, out_dtype=F32, tm=1024):
    T = x.shape[0]
    K = w.shape[0]
    tm = _pick(T, tm)
    col_map = col_map or (lambda j: j)
    has_rope = rope is not None
    assert has_rope or scale_mask == 0
    in_specs = [
        pl.BlockSpec((tm, K), lambda i, j: (i, x_col_block)),
        pl.BlockSpec((1, K), lambda i, j: (0, 0)),
        pl.BlockSpec((K, tn), lambda i, j: (0, col_map(j))),
    ]
    args = [x, norm_w.reshape(1, K).astype(F32), w]
    if has_rope:
        in_specs += [pl.BlockSpec((tm, LANE), lambda i, j: (i, 0))] * 2
        args += list(rope)
    kern = functools.partial(_proj_kernel, has_rope=has_rope, rope_mask=rope_mask, scale_mask=scale_mask,
                             scale=scale, sigmoid=sigmoid)
    return pl.pallas_call(
        kern,
        grid=(T // tm, n_blocks),
        in_specs=in_specs,
        out_specs=pl.BlockSpec((tm, tn), lambda i, j: (i, j)),
        out_shape=jax.ShapeDtypeStruct((T, n_blocks * tn), out_dtype),
        scratch_shapes=[pltpu.VMEM((tm, K), BF16)],
        compiler_params=_params("parallel", "arbitrary"),
        name="proj",
    )(*args)


def _matmul_residual_kernel(*refs):
    *a_refs, w_ref, r_ref, o_ref = refs
    a = a_refs[0][...]
    if len(a_refs) > 1:
        a = a.astype(F32)
        for a_ref in a_refs[1:]:
            a = a + a_ref[...].astype(F32)
        a = a.astype(BF16)
    o_ref[...] = r_ref[...] + jnp.dot(a, w_ref[...], preferred_element_type=F32)


def matmul_residual(a_list, w, res, *, tm=1024, tn=1024):
    T, K = a_list[0].shape
    N = w.shape[1]
    tm, tn = _pick(T, tm), _pick(N, tn)
    return pl.pallas_call(
        _matmul_residual_kernel,
        grid=(T // tm, N // tn),
        in_specs=[pl.BlockSpec((tm, K), lambda i, j: (i, 0))] * len(a_list) + [
            pl.BlockSpec((K, tn), lambda i, j: (0, j)),
            pl.BlockSpec((tm, tn), lambda i, j: (i, j)),
        ],
        out_specs=pl.BlockSpec((tm, tn), lambda i, j: (i, j)),
        out_shape=jax.ShapeDtypeStruct((T, N), F32),
        compiler_params=_params("parallel", "arbitrary"),
        name="matmul_residual",
    )(*a_list, w, res)


def _ffn_kernel(x_ref, nw_ref, wg_ref, wu_ref, wo_ref, fw_ref, o_ref, xn_scr, *, final_norm):
    f = pl.program_id(1)
    tm = x_ref.shape[0]
    slab = min(FFN_SLAB, tm)

    def over_slabs(fn):
        def body(r, carry):
            fn(pl.ds(pl.multiple_of(r * slab, slab), slab))
            return carry

        lax.fori_loop(0, tm // slab, body, 0)

    @pl.when(f == 0)
    def _():
        def prologue(rows):
            xn_scr[rows, :] = _rms_rows(x_ref[rows, :], nw_ref[...]).astype(BF16)

        over_slabs(prologue)
        o_ref[...] = jnp.zeros_like(o_ref)

    xn = xn_scr[...]
    g = jnp.dot(xn, wg_ref[...], preferred_element_type=F32)
    u = jnp.dot(xn, wu_ref[...], preferred_element_type=F32)
    h = (g * jax.nn.sigmoid(g) * u).astype(BF16)
    o_ref[...] += jnp.dot(h, wo_ref[...], preferred_element_type=F32)

    @pl.when(f == pl.num_programs(1) - 1)
    def _():
        def epilogue(rows):
            y = x_ref[rows, :] + 0.5 * o_ref[rows, :]
            if final_norm:
                y = _rms_rows(y, fw_ref[...])
            o_ref[rows, :] = y

        over_slabs(epilogue)


def ffn_half_step(x, norm_w, w_in, w_out, final_w=None, *, tm=1024, tf=512):
    T, D = x.shape
    DFF = w_out.shape[0]
    tm, tf = _pick(T, tm), _pick(DFF, tf)
    nf = DFF // tf
    final_norm = final_w is not None
    fw = (final_w if final_norm else norm_w).reshape(1, D).astype(F32)
    return pl.pallas_call(
        functools.partial(_ffn_kernel, final_norm=final_norm),
        grid=(T // tm, nf),
        in_specs=[
            pl.BlockSpec((tm, D), lambda i, f: (i, 0)),
            pl.BlockSpec((1, D), lambda i, f: (0, 0)),
            pl.BlockSpec((D, tf), lambda i, f: (0, f)),
            pl.BlockSpec((D, tf), lambda i, f: (0, f + nf)),
            pl.BlockSpec((tf, D), lambda i, f: (f, 0)),
            pl.BlockSpec((1, D), lambda i, f: (0, 0)),
        ],
        out_specs=pl.BlockSpec((tm, D), lambda i, f: (i, 0)),
        out_shape=jax.ShapeDtypeStruct((T, D), F32),
        scratch_shapes=[pltpu.VMEM((tm, D), BF16)],
        compiler_params=_params("parallel", "arbitrary", vmem=FFN_VMEM_LIMIT),
        name="ffn_half_step",
    )(x, norm_w.reshape(1, D).astype(F32), w_in, w_in, w_out, fw)


def _flash_kernel(*refs, R, E, has_kx, gate_cols, tq, tk, mode, chunks_per_version, depth):
    it = iter(refs)
    qm_ref = next(it)
    qx_ref = next(it) if E else None
    km_ref = next(it)
    kx_ref = next(it) if has_kx else None
    v_ref = next(it)
    gate_ref = next(it) if gate_cols else None
    cap_ref = next(it)
    o_ref = next(it)
    q_scr, m_scr, acc_scr, s_scr = it
    M = R * tq
    i = pl.program_id(2)

    for e in range(max(E, 1)):
        for r in range(R):
            q_scr[e, r * tq:(r + 1) * tq, 0:LANE] = qm_ref[0, :, r * LANE:(r + 1) * LANE]
            if E:
                q_scr[e, r * tq:(r + 1) * tq, LANE:2 * LANE] = qx_ref[0, :, e * LANE:(e + 1) * LANE]
    m_scr[...] = jnp.full_like(m_scr, NEG_INF)
    acc_scr[...] = jnp.zeros_like(acc_scr)
    ones_col = jnp.where(lax.broadcasted_iota(jnp.int32, (tk, LANE), 1) == 0, 1.0, 0.0).astype(BF16)

    def scores(c):
        k0 = pl.multiple_of(c * tk, tk)
        k = km_ref[0, pl.ds(k0, tk), :]
        if has_kx:
            k = jnp.concatenate([k, kx_ref[0, pl.ds(k0, tk), :]], axis=1)
        q = q_scr[c // chunks_per_version] if E > 1 else q_scr[0]
        return lax.dot_general(q, k, (((1,), (1,)), ((), ())), preferred_element_type=F32).astype(BF16)

    def softmax_pv(c, s, cap=None):
        k0 = pl.multiple_of(c * tk, tk)
        if cap is not None:
            s = jnp.minimum(s, cap)
        m_prev = m_scr[...]
        m_new = jnp.maximum(m_prev, jnp.max(s, axis=1, keepdims=True).astype(F32))
        alpha = jnp.exp2(m_prev - m_new)
        p = jnp.exp2(s - pltpu.repeat(m_new.astype(BF16), tk // LANE, axis=1))
        v1 = jnp.concatenate([v_ref[0, pl.ds(k0, tk), :], ones_col], axis=1)
        acc_scr[...] = pltpu.repeat(alpha, 2, axis=1) * acc_scr[...] + jnp.dot(p, v1, preferred_element_type=F32)
        m_scr[...] = m_new

    if mode == "causal":
        n_full = (i * tq) // tk
        bufs = [s_scr.at[u] for u in range(depth)]
        bufs[0][...] = scores(0)

        def group(j, carry):
            for u in range(depth):
                bufs[(u + 1) % depth][...] = scores(depth * j + u + 1)
                softmax_pv(depth * j + u, bufs[u][...])
            return carry

        n_groups = n_full // depth
        lax.fori_loop(0, n_groups, group, 0)

        def tail(c, carry):
            softmax_pv(c, bufs[0][...])
            bufs[0][...] = scores(c + 1)
            return carry

        lax.fori_loop(depth * n_groups, n_full, tail, 0)
        softmax_pv(n_full, bufs[0][...], cap_ref[0])
    else:
        prev = jnp.maximum(i - 1, 0)
        s_prev = scores(prev)
        s_diag = scores(i)
        softmax_pv(prev, s_prev, jnp.where(i > 0, cap_ref[1], NEG_INF))
        softmax_pv(i, s_diag, cap_ref[0])

    out = acc_scr[:, 0:LANE] / acc_scr[:, LANE:LANE + 1]
    for r in range(R):
        o_r = out[r * tq:(r + 1) * tq]
        if gate_cols:
            o_r = o_r * gate_ref[0, :, gate_cols[r]:gate_cols[r] + 1]
        o_ref[0, :, r * LANE:(r + 1) * LANE] = o_r.astype(o_ref.dtype)


def flash_attention(qm, qx, km, kx, v, gate=None, *, n_groups, R, E, qm_col, qx_col=None, km_col, v_col, kx_batched,
                    gate_cols=None, mode, window=0, tq, tk, depth=2, keys_per_version=None):
    B, S, _ = qm.shape
    tq, tk = _pick(S, tq), _pick(S, tk)
    assert tq == tk
    assert mode == "causal" or tq == window
    has_kx = kx is not None
    dk = 2 * LANE if has_kx else LANE
    cpv = (keys_per_version // tk) if keys_per_version else 1
    M = R * tq
    in_specs = [pl.BlockSpec((1, tq, R * LANE), lambda b, g, i: (b, i, qm_col(g)))]
    args = [qm]
    if E:
        in_specs.append(pl.BlockSpec((1, tq, E * LANE), lambda b, g, i: (b, i, qx_col(g))))
        args.append(qx)
    in_specs.append(pl.BlockSpec((1, S, LANE), lambda b, g, i: (b, 0, km_col(g))))
    args.append(km)
    if has_kx:
        in_specs.append(pl.BlockSpec((1, S, LANE), lambda b, g, i: (b if kx_batched else 0, 0, 0)))
        args.append(kx)
    in_specs.append(pl.BlockSpec((1, S, LANE), lambda b, g, i: (b, 0, v_col(g))))
    args.append(v)
    if gate_cols:
        in_specs.append(pl.BlockSpec((1, tq, LANE), lambda b, g, i: (b, i, g)))
        args.append(gate)
    q_off = lax.broadcasted_iota(jnp.int32, (M, tk), 0) % tq
    k_off = lax.broadcasted_iota(jnp.int32, (M, tk), 1)
    caps = [k_off <= q_off] + ([k_off > q_off] if mode == "window" else [])
    cap = jnp.stack([jnp.where(c, MASK_PASS, NEG_INF).astype(BF16) for c in caps])
    in_specs.append(pl.BlockSpec(cap.shape, lambda b, g, i: (0, 0, 0)))
    args.append(cap)
    kern = functools.partial(_flash_kernel, R=R, E=E, has_kx=has_kx, gate_cols=gate_cols, tq=tq, tk=tk, mode=mode,
                             chunks_per_version=cpv, depth=depth)
    return pl.pallas_call(
        kern,
        grid=(B, n_groups, S // tq),
        in_specs=in_specs,
        out_specs=pl.BlockSpec((1, tq, R * LANE), lambda b, g, i: (b, i, g)),
        out_shape=jax.ShapeDtypeStruct((B, S, n_groups * R * LANE), BF16),
        scratch_shapes=[
            pltpu.VMEM((max(E, 1), M, dk), BF16),
            pltpu.VMEM((M, LANE), F32),
            pltpu.VMEM((M, 2 * LANE), F32),
            pltpu.VMEM((depth, M, tk), BF16),
        ],
        compiler_params=_params("parallel", "parallel", "arbitrary"),
        name="flash_" + mode + ("_x%d" % E if E else ""),
    )(*args)


def _compress_kernel(tok_ref, pe_ref, w1_ref, w2_ref, c2_ref, s2_ref, o_ref, *, ncp):
    half = NSA_CMP_STRIDE
    a = b = None
    for l in range(half):
        x = tok_ref[0, pl.ds(l, ncp, stride=half), :]
        da = jnp.dot((x + pe_ref[0, l:l + 1, :]).astype(BF16), w1_ref[0, l], preferred_element_type=F32)
        db = jnp.dot((x + pe_ref[0, half + l:half + l + 1, :]).astype(BF16), w1_ref[0, half + l],
                     preferred_element_type=F32)
        a = da if a is None else a + da
        b = db if b is None else b + db
    h = jax.nn.gelu(a + pltpu.roll(b, ncp - 1, axis=0))
    y = jnp.dot(h.astype(BF16), w2_ref[0], preferred_element_type=F32)
    is_key = pl.program_id(0) == 0
    c2 = jnp.where(is_key, c2_ref[0], 1.0)
    s2 = jnp.where(is_key, s2_ref[0], 0.0)
    o_ref[0, 0, 0] = _rotate_half(y, c2, s2).astype(o_ref.dtype)


def compress_tokens(tok, pe, w1, w2, c2, s2):
    B, S, _ = tok.shape
    G, dk = NSA_KV_GROUPS, NSA_HEAD_DIM
    ncp = S // NSA_CMP_STRIDE
    hidden = w2.shape[1]
    return pl.pallas_call(
        functools.partial(_compress_kernel, ncp=ncp),
        grid=(2, B, G),
        in_specs=[
            pl.BlockSpec((1, S, dk), lambda s, b, g: (b, 0, s * G + g)),
            pl.BlockSpec((1, NSA_CMP_LEN, dk), lambda s, b, g: (s, 0, 0)),
            pl.BlockSpec((1, NSA_CMP_LEN, dk, hidden), lambda s, b, g: (s, 0, 0, 0)),
            pl.BlockSpec((1, hidden, dk), lambda s, b, g: (s, 0, 0)),
            pl.BlockSpec((1, ncp, dk), lambda s, b, g: (b, 0, 0)),
            pl.BlockSpec((1, ncp, dk), lambda s, b, g: (b, 0, 0)),
        ],
        out_specs=pl.BlockSpec((1, 1, 1, ncp, dk), lambda s, b, g: (s, b, g, 0, 0)),
        out_shape=jax.ShapeDtypeStruct((2, B, G, ncp, dk), BF16),
        compiler_params=_params("parallel", "parallel", "parallel"),
        name="compress_tokens",
    )(tok, pe, w1, w2, c2, s2)


def _cmp_select_kernel(q_ref, kc_ref, vc_ref, ovt_ref, gate_ref, oc_ref, nm_ref, *, R, tq, n_sub, tile0, n_cmp, n_top):
    for u in range(n_sub):
        rows = slice(u * tq, (u + 1) * tq)
        t0 = (tile0 + pl.program_id(2) * n_sub + u) * tq
        _cmp_select_tile(q_ref.at[0, rows], kc_ref, vc_ref, ovt_ref, gate_ref.at[0, rows], oc_ref.at[0, rows],
                         nm_ref.at[0, rows], t0, R=R, tq=tq, n_cmp=n_cmp, n_top=n_top)


def _cmp_select_tile(q_ref, kc_ref, vc_ref, ovt_ref, gate_ref, oc_ref, nm_ref, t0, *, R, tq, n_cmp, n_top):
    M = R * tq
    q = jnp.concatenate([q_ref[:, r * LANE:(r + 1) * LANE] for r in range(R)], axis=0)
    kc = kc_ref[0, 0, 0]
    ncp = kc.shape[0]
    s = lax.dot_general(q, kc, (((1,), (1,)), ((), ())), preferred_element_type=F32)
    t = t0 + _mod_pow2(lax.broadcasted_iota(jnp.int32, (M, 1), 0), tq)
    lim = jnp.minimum(_div_pow2(t - (NSA_CMP_LEN - 1), NSA_CMP_STRIDE) + 1, n_cmp)
    valid = lax.broadcasted_iota(jnp.int32, (M, ncp), 1) < lim
    sm = jnp.where(valid, s, NEG_INF)
    e = jnp.exp2((sm - jnp.max(sm, axis=1, keepdims=True)).astype(BF16))
    ones_col = jnp.where(lax.broadcasted_iota(jnp.int32, (ncp, LANE), 1) == 0, 1.0, 0.0).astype(BF16)
    oc = jnp.dot(e, jnp.concatenate([vc_ref[0, 0, 0], ones_col], axis=1), preferred_element_type=F32)
    oc = oc[:, 0:LANE] * jnp.where(lim > 0, 1.0 / oc[:, LANE:LANE + 1], 0.0)
    for r in range(R):
        gate = gate_ref[:, NSA_N_BRANCHES * r:NSA_N_BRANCHES * r + 1]
        oc_ref[:, r * LANE:(r + 1) * LANE] = (oc[r * tq:(r + 1) * tq] * gate).astype(oc_ref.dtype)

    nb = ovt_ref.shape[0] - BF16_ROWS
    imp_un = lax.dot_general(ovt_ref[...], e, (((1,), (1,)), ((), ())), preferred_element_type=F32)
    t_l = t0 + _mod_pow2(lax.broadcasted_iota(jnp.int32, (1, M), 1), tq)
    lim_l = jnp.minimum(_div_pow2(t_l - (NSA_CMP_LEN - 1), NSA_CMP_STRIDE) + 1, n_cmp)
    imp_h = imp_un[0:nb] * jnp.where(lim_l > 0, 1.0 / imp_un[nb:nb + 1], 0.0)
    imp = imp_h[:, 0:tq]
    for r in range(1, R):
        imp = imp + imp_h[:, r * tq:(r + 1) * tq]
    j = lax.broadcasted_iota(jnp.int32, (nb, tq), 0)
    jf = j.astype(F32)
    cur = _div_pow2(t0 + lax.broadcasted_iota(jnp.int32, (nb, tq), 1), NSA_SEL_BLOCK)
    forced = jnp.logical_or(j == 0, jnp.logical_or(j == cur, j == cur - 1))
    x = jnp.where(j > cur, NEG_INF, imp + FORCE_BONUS * forced.astype(F32))
    for _ in range(n_top):
        mx = jnp.max(x, axis=0, keepdims=True)
        first = jnp.min(jnp.where(x == mx, jf, float(nb)), axis=0, keepdims=True)
        x = jnp.where(jf == first, jnp.where(mx > 0.5 * NEG_INF, -jnp.inf, NEG_INF), x)
    nm_ref[:, 0:nb] = jnp.where(x == -jnp.inf, 0.0, NEG_INF).T.astype(nm_ref.dtype)
    if nb < nm_ref.shape[1]:
        nm_ref[:, nb:] = jnp.full((tq, nm_ref.shape[1] - nb), NEG_INF, nm_ref.dtype)


def cmp_attention_select(q, kv_cmp, overlap, gate, *, n_cmp, n_top, tq=128, n_sub=2):
    B, S, _ = q.shape
    G, R = NSA_KV_GROUPS, NSA_R
    NB, NCP = overlap.shape
    tq = _pick(S, tq)
    tb = n_sub * tq
    n_parts = next(n for n in (4, 2, 1) if NCP % (n * LANE) == 0 and S % (n * tb) == 0)
    steps = S // (n_parts * tb)
    outs = []
    for part in range(n_parts):
        ncp_w = NCP * (part + 1) // n_parts
        nb_w = min(NB, -(-(S * (part + 1) // n_parts // NSA_SEL_BLOCK) // LANE) * LANE)
        ov = jnp.concatenate([overlap[:nb_w, :ncp_w], jnp.ones((1, ncp_w), BF16),
                              jnp.zeros((BF16_ROWS - 1, ncp_w), BF16)], axis=0)
        kern = functools.partial(_cmp_select_kernel, R=R, tq=tq, n_sub=n_sub, tile0=part * steps * n_sub,
                                 n_cmp=n_cmp, n_top=n_top)
        off = part * steps
        outs.append(pl.pallas_call(
            kern,
            grid=(B, G, steps),
            in_specs=[
                pl.BlockSpec((1, tb, R * LANE), lambda b, g, i, off=off: (b, i + off, g)),
                pl.BlockSpec((1, 1, 1, ncp_w, LANE), lambda b, g, i: (0, b, g, 0, 0)),
                pl.BlockSpec((1, 1, 1, ncp_w, LANE), lambda b, g, i: (1, b, g, 0, 0)),
                pl.BlockSpec((nb_w + BF16_ROWS, ncp_w), lambda b, g, i: (0, 0)),
                pl.BlockSpec((1, tb, LANE), lambda b, g, i, off=off: (b, i + off, g)),
            ],
            out_specs=[
                pl.BlockSpec((1, tb, R * LANE), lambda b, g, i: (b, i, g)),
                pl.BlockSpec((1, tb, NB), lambda b, g, i: (b, i, g)),
            ],
            out_shape=[
                jax.ShapeDtypeStruct((B, S // n_parts, G * R * LANE), BF16),
                jax.ShapeDtypeStruct((B, S // n_parts, G * NB), BF16),
            ],
            compiler_params=_params("parallel", "parallel", "arbitrary"),
            name="cmp_attention_select",
        )(q, kv_cmp, kv_cmp, ov, gate))
    return [jnp.concatenate(parts, axis=1) for parts in zip(*outs)]


def _rope_tables(pos, dim):
    inv = 1.0 / (ROPE_THETA ** (jnp.arange(0, dim, 2, dtype=F32) / dim))
    ang = pos.astype(F32)[..., None] * inv
    pad = [(0, 0)] * (ang.ndim - 1) + [(0, (LANE - dim) // 2)]
    c, s = jnp.pad(jnp.cos(ang), pad), jnp.pad(jnp.sin(ang), pad)
    return jnp.concatenate([c, c], axis=-1), jnp.concatenate([-s, s], axis=-1)


def _spread_rope_cols(w):
    dim = w.shape[-1]
    pad = [(0, 0)] * (w.ndim - 1) + [(0, (LANE - dim) // 2)]
    return jnp.concatenate([jnp.pad(w[..., :dim // 2], pad), jnp.pad(w[..., dim // 2:], pad)], axis=-1)


def nsa_mixer(x, B, S, positions, norm_w, w_in, cmp_pe, cmp_w1, cmp_w2, w_out):
    T, D = x.shape
    H, G, R, dk = NSA_HEADS, NSA_KV_GROUPS, NSA_R, NSA_HEAD_DIM
    q_cols, kv_cols = H * dk, NSA_N_BRANCHES * 2 * G * dk
    grp = G * dk
    q_scale = dk ** -0.5 * LOG2E
    tables = _rope_tables(positions.reshape(T), dk)

    w_qkv = w_in[:, :q_cols + kv_cols].astype(BF16)
    nq = q_cols // grp
    qkv = proj(x, norm_w, w_qkv, n_blocks=nq + 4, tn=grp, col_map=lambda j: jnp.where(j < nq, j, j + 2),
               rope=tables, rope_mask=(1 << nq) - 1 | 0b0101 << nq, scale_mask=(1 << nq) - 1, scale=q_scale,
               out_dtype=BF16).reshape(B, S, q_cols + 4 * grp)
    cmp_tok = proj(x, norm_w, w_qkv, n_blocks=1, tn=2 * grp, col_map=lambda j: q_cols // (2 * grp)).reshape(B, S, 2 * grp)
    w_gate = w_in[:, q_cols + kv_cols:].reshape(D, G, R * NSA_N_BRANCHES)
    w_gate = jnp.pad(w_gate, ((0, 0), (0, 0), (0, LANE - R * NSA_N_BRANCHES))).reshape(D, G * LANE).astype(BF16)
    gates = proj(x, norm_w, w_gate, n_blocks=1, tn=G * LANE, sigmoid=True).reshape(B, S, G * LANE)

    ncp = S // NSA_CMP_STRIDE
    n_cmp = (S - NSA_CMP_LEN) // NSA_CMP_STRIDE + 1
    cmp_end = jnp.minimum(jnp.arange(ncp) * NSA_CMP_STRIDE + NSA_CMP_LEN - 1, S - 1)
    c2c, s2c = _rope_tables(positions[:, cmp_end], dk)
    hidden = cmp_w1.shape[-1]
    kv_cmp = compress_tokens(cmp_tok, cmp_pe, cmp_w1.reshape(2, NSA_CMP_LEN, dk, hidden).astype(BF16),
                             cmp_w2.astype(BF16), c2c, s2c)

    n_sel = S // NSA_SEL_BLOCK
    nb = -(-n_sel // LANE) * LANE
    blk_start = jnp.arange(ncp) * NSA_CMP_STRIDE
    sel_start = jnp.arange(nb) * NSA_SEL_BLOCK
    overlap = ((blk_start[None, :] <= sel_start[:, None] + NSA_SEL_BLOCK - 1)
               & (blk_start[None, :] + NSA_CMP_LEN - 1 >= sel_start[:, None])
               & (jnp.arange(ncp)[None, :] < n_cmp) & (jnp.arange(nb)[:, None] < n_sel)).astype(BF16)
    o_cmp, neg_mask = cmp_attention_select(qkv, kv_cmp, overlap, gates, n_cmp=n_cmp, n_top=min(NSA_N_SELECT, n_sel))

    blk_onehot = (jnp.arange(S)[:, None] // NSA_SEL_BLOCK % LANE == jnp.arange(LANE)[None, :]).astype(BF16)[None]
    kv0 = q_cols // dk
    o_slc = flash_attention(qkv, neg_mask, qkv, blk_onehot, qkv, gates, n_groups=G, R=R, E=nb // LANE,
                            qm_col=lambda g: g, qx_col=lambda g: g, km_col=lambda g: kv0 + g,
                            v_col=lambda g: kv0 + G + g, kx_batched=False,
                            gate_cols=tuple(NSA_N_BRANCHES * r + 1 for r in range(R)), mode="causal",
                            tq=512, tk=512, depth=2, keys_per_version=LANE * NSA_SEL_BLOCK)
    o_win = flash_attention(qkv, None, qkv, None, qkv, gates, n_groups=G, R=R, E=0,
                            qm_col=lambda g: g, km_col=lambda g: kv0 + 2 * G + g, v_col=lambda g: kv0 + 3 * G + g,
                            kx_batched=False, gate_cols=tuple(NSA_N_BRANCHES * r + 2 for r in range(R)),
                            mode="window", window=NSA_WINDOW, tq=512, tk=512)
    return matmul_residual([o.reshape(T, H * dk) for o in (o_cmp, o_slc, o_win)], w_out.astype(BF16), x)


def mla_mixer(x, B, S, positions, norm_w, w_in, q_norm_w, kv_norm_w, w_uq, w_ukv, w_out):
    T, D = x.shape
    H = MLA_HEADS
    q_rank, kv_rank = q_norm_w.shape[0], kv_norm_w.shape[0]
    assert q_rank == kv_rank and q_rank % LANE == 0
    q_scale = MLA_QK_DIM ** -0.5 * LOG2E
    tables = _rope_tables(positions.reshape(T), MLA_ROPE_DIM)

    latent = proj(x, norm_w, w_in[:, :2 * q_rank].astype(BF16), n_blocks=1, tn=2 * q_rank)
    w_kr = _spread_rope_cols(w_in[:, 2 * q_rank:]).astype(BF16)
    k_rope = proj(x, norm_w, w_kr, n_blocks=1, tn=LANE, rope=tables, rope_mask=1, out_dtype=BF16).reshape(B, S, LANE)
    w_uq3 = w_uq.reshape(q_rank, H, MLA_QK_DIM)
    w_uq_p = jnp.concatenate([w_uq3[:, :, :MLA_NOPE_DIM].reshape(q_rank, H * MLA_NOPE_DIM),
                              _spread_rope_cols(w_uq3[:, :, MLA_NOPE_DIM:]).reshape(q_rank, H * LANE)],
                             axis=1).astype(BF16)
    tn = 4 * LANE
    n_nope = H * MLA_NOPE_DIM // tn
    n_all = n_nope + H * LANE // tn
    q = proj(latent, q_norm_w, w_uq_p, n_blocks=n_all, tn=tn, x_col_block=0, rope=tables,
             rope_mask=(1 << n_all) - (1 << n_nope), scale_mask=(1 << n_all) - 1, scale=q_scale,
             out_dtype=BF16).reshape(B, S, n_all * tn)
    kvb = proj(latent, kv_norm_w, w_ukv.astype(BF16), n_blocks=w_ukv.shape[1] // (8 * LANE), tn=8 * LANE,
               x_col_block=1, out_dtype=BF16).reshape(B, S, 2 * H * LANE)

    o = flash_attention(q, q, kvb, k_rope, kvb, n_groups=H, R=1, E=1, qm_col=lambda h: h, qx_col=lambda h: H + h,
                        km_col=lambda h: 2 * h, v_col=lambda h: 2 * h + 1, kx_batched=True,
                        mode="causal", tq=1024, tk=1024, depth=2)
    return matmul_residual([o.reshape(T, H * MLA_V_DIM)], w_out.astype(BF16), x)


def kernel(x, positions, ffn_norm_w, ffn_w_in, ffn_w_out, mix_norm_w, nsa_w_in, nsa_cmp_pe, nsa_cmp_w1, nsa_cmp_w2,
           nsa_w_out, mla_w_in, mla_q_norm_w, mla_kv_norm_w, mla_w_uq, mla_w_ukv, mla_w_out, final_norm_w):
    B, S, D = x.shape
    depth = ffn_norm_w.shape[0]
    n_mixers = 2
    h = x.reshape(B * S, D)
    for i in range(depth):
        h = ffn_half_step(h, ffn_norm_w[i, 0], ffn_w_in[i, 0].astype(BF16), ffn_w_out[i, 0].astype(BF16))
        j = i // n_mixers
        if i % n_mixers == 0:
            h = nsa_mixer(h, B, S, positions, mix_norm_w[i], nsa_w_in[j], nsa_cmp_pe[j], nsa_cmp_w1[j],
                          nsa_cmp_w2[j], nsa_w_out[j])
        else:
            h = mla_mixer(h, B, S, positions, mix_norm_w[i], mla_w_in[j], mla_q_norm_w[j], mla_kv_norm_w[j],
                          mla_w_uq[j], mla_w_ukv[j], mla_w_out[j])
        last = i == depth - 1
        h = ffn_half_step(h, ffn_norm_w[i, 1], ffn_w_in[i, 1].astype(BF16), ffn_w_out[i, 1].astype(BF16),
                          final_w=final_norm_w if last else None)
    return h.reshape(B, S, D)
```

```python
import functools
import math

import jax
import jax.numpy as jnp
from jax import lax
from jax.experimental import pallas as pl
from jax.experimental.pallas import tpu as pltpu

F32 = jnp.float32
BF16 = jnp.bfloat16

LANE = 128
BF16_ROWS = 16
RMS_EPS = 1e-6
ROPE_THETA = 10000.0
NEG_INF = -1e30
MASK_PASS = 3e38
FORCE_BONUS = 1e9
LOG2E = math.log2(math.e)
VMEM_LIMIT = 56 * 1024 * 1024
FFN_VMEM_LIMIT = 61 * 1024 * 1024
FFN_SLAB = 64

NSA_HEADS = 16
NSA_HEAD_DIM = 128
NSA_KV_GROUPS = 4
NSA_R = NSA_HEADS // NSA_KV_GROUPS
NSA_N_BRANCHES = 3
NSA_CMP_LEN = 32
NSA_CMP_STRIDE = 16
NSA_SEL_BLOCK = 64
NSA_N_SELECT = 16
NSA_WINDOW = 512

MLA_HEADS = 16
MLA_NOPE_DIM = 128
MLA_ROPE_DIM = 64
MLA_V_DIM = 128
MLA_QK_DIM = MLA_NOPE_DIM + MLA_ROPE_DIM


def _pick(n, pref):
    if n <= pref:
        return n
    t = (pref // LANE) * LANE
    while t > LANE and n % t:
        t -= LANE
    assert n % t == 0, (n, pref)
    return t


def _div_pow2(x, d):
    assert d & (d - 1) == 0, d
    return jnp.right_shift(x, d.bit_length() - 1)


def _mod_pow2(x, d):
    assert d & (d - 1) == 0, d
    return jnp.bitwise_and(x, d - 1)


def _params(*sem, vmem=VMEM_LIMIT):
    return pltpu.CompilerParams(dimension_semantics=sem, vmem_limit_bytes=vmem)


def _rotate_half(y, c2, s2):
    return y * c2 + pltpu.roll(y, LANE // 2, axis=1) * s2


def _rms_rows(x, w):
    ms = jnp.mean(x * x, axis=-1, keepdims=True)
    return x * lax.rsqrt(ms + RMS_EPS) * w


def _proj_kernel(*refs, has_rope, rope_mask, scale_mask, scale, sigmoid):
    if has_rope:
        x_ref, nw_ref, w_ref, c2_ref, s2_ref, o_ref, xn_scr = refs
    else:
        x_ref, nw_ref, w_ref, o_ref, xn_scr = refs
    j = pl.program_id(1)

    @pl.when(j == 0)
    def _():
        xn_scr[...] = _rms_rows(x_ref[...], nw_ref[...]).astype(BF16)

    y = jnp.dot(xn_scr[...], w_ref[...], preferred_element_type=F32)
    if has_rope:
        rope_on = jnp.bitwise_and(lax.shift_right_logical(jnp.int32(rope_mask), j), 1) == 1
        factor = jnp.where(jnp.bitwise_and(lax.shift_right_logical(jnp.int32(scale_mask), j), 1) == 1, scale, 1.0)
        c2 = jnp.where(rope_on, c2_ref[...], 1.0) * factor
        s2 = jnp.where(rope_on, s2_ref[...], 0.0) * factor
        for h in range(y.shape[1] // LANE):
            o_ref[:, h * LANE:(h + 1) * LANE] = _rotate_half(y[:, h * LANE:(h + 1) * LANE], c2, s2).astype(o_ref.dtype)
    else:
        if sigmoid:
            y = jax.nn.sigmoid(y)
        o_ref[...] = y.astype(o_ref.dtype)


def proj(x, norm_w, w, *, n_blocks, tn, col_map=None, x_col_block=0, rope=None, rope_mask=0, scale_mask=0, scale=1.0,
         sigmoid=False, out_dtype=F32, tm=1024):
    T = x.shape[0]
    K = w.shape[0]
    tm = _pick(T, tm)
    col_map = col_map or (lambda j: j)
    has_rope = rope is not None
    assert has_rope or scale_mask == 0
    in_specs = [
        pl.BlockSpec((tm, K), lambda i, j: (i, x_col_block)),
        pl.BlockSpec((1, K), lambda i, j: (0, 0)),
        pl.BlockSpec((K, tn), lambda i, j: (0, col_map(j))),
    ]
    args = [x, norm_w.reshape(1, K).astype(F32), w]
    if has_rope:
        in_specs += [pl.BlockSpec((tm, LANE), lambda i, j: (i, 0))] * 2
        args += list(rope)
    kern = functools.partial(_proj_kernel, has_rope=has_rope, rope_mask=rope_mask, scale_mask=scale_mask,
                             scale=scale, sigmoid=sigmoid)
    return pl.pallas_call(
        kern,
        grid=(T // tm, n_blocks),
        in_specs=in_specs,
        out_specs=pl.BlockSpec((tm, tn), lambda i, j: (i, j)),
        out_shape=jax.ShapeDtypeStruct((T, n_blocks * tn), out_dtype),
        scratch_shapes=[pltpu.VMEM((tm, K), BF16)],
        compiler_params=_params("parallel", "arbitrary"),
        name="proj",
    )(*args)


def _matmul_residual_kernel(*refs):
    *a_refs, w_ref, r_ref, o_ref = refs
    a = a_refs[0][...]
    if len(a_refs) > 1:
        a = a.astype(F32)
        for a_ref in a_refs[1:]:
            a = a + a_ref[...].astype(F32)
        a = a.astype(BF16)
    o_ref[...] = r_ref[...] + jnp.dot(a, w_ref[...], preferred_element_type=F32)


def matmul_residual(a_list, w, res, *, tm=1024, tn=1024):
    T, K = a_list[0].shape
    N = w.shape[1]
    tm, tn = _pick(T, tm), _pick(N, tn)
    return pl.pallas_call(
        _matmul_residual_kernel,
        grid=(T // tm, N // tn),
        in_specs=[pl.BlockSpec((tm, K), lambda i, j: (i, 0))] * len(a_list) + [
            pl.BlockSpec((K, tn), lambda i, j: (0, j)),
            pl.BlockSpec((tm, tn), lambda i, j: (i, j)),
        ],
        out_specs=pl.BlockSpec((tm, tn), lambda i, j: (i, j)),
        out_shape=jax.ShapeDtypeStruct((T, N), F32),
        compiler_params=_params("parallel", "arbitrary"),
        name="matmul_residual",
    )(*a_list, w, res)


def _ffn_kernel(x_ref, nw_ref, wg_ref, wu_ref, wo_ref, fw_ref, o_ref, xn_scr, *, final_norm):
    f = pl.program_id(1)
    tm = x_ref.shape[0]
    slab = min(FFN_SLAB, tm)

    def over_slabs(fn):
        def body(r, carry):
            fn(pl.ds(pl.multiple_of(r * slab, slab), slab))
            return carry

        lax.fori_loop(0, tm // slab, body, 0)

    @pl.when(f == 0)
    def _():
        def prologue(rows):
            xn_scr[rows, :] = _rms_rows(x_ref[rows, :], nw_ref[...]).astype(BF16)

        over_slabs(prologue)
        o_ref[...] = jnp.zeros_like(o_ref)

    xn = xn_scr[...]
    g = jnp.dot(xn, wg_ref[...], preferred_element_type=F32)
    u = jnp.dot(xn, wu_ref[...], preferred_element_type=F32)
    h = (g * jax.nn.sigmoid(g) * u).astype(BF16)
    o_ref[...] += jnp.dot(h, wo_ref[...], preferred_element_type=F32)

    @pl.when(f == pl.num_programs(1) - 1)
    def _():
        def epilogue(rows):
            y = x_ref[rows, :] + 0.5 * o_ref[rows, :]
            if final_norm:
                y = _rms_rows(y, fw_ref[...])
            o_ref[rows, :] = y

        over_slabs(epilogue)


def ffn_half_step(x, norm_w, w_in, w_out, final_w=None, *, tm=1024, tf=512):
    T, D = x.shape
    DFF = w_out.shape[0]
    tm, tf = _pick(T, tm), _pick(DFF, tf)
    nf = DFF // tf
    final_norm = final_w is not None
    fw = (final_w if final_norm else norm_w).reshape(1, D).astype(F32)
    return pl.pallas_call(
        functools.partial(_ffn_kernel, final_norm=final_norm),
        grid=(T // tm, nf),
        in_specs=[
            pl.BlockSpec((tm, D), lambda i, f: (i, 0)),
            pl.BlockSpec((1, D), lambda i, f: (0, 0)),
            pl.BlockSpec((D, tf), lambda i, f: (0, f)),
            pl.BlockSpec((D, tf), lambda i, f: (0, f + nf)),
            pl.BlockSpec((tf, D), lambda i, f: (f, 0)),
            pl.BlockSpec((1, D), lambda i, f: (0, 0)),
        ],
        out_specs=pl.BlockSpec((tm, D), lambda i, f: (i, 0)),
        out_shape=jax.ShapeDtypeStruct((T, D), F32),
        scratch_shapes=[pltpu.VMEM((tm, D), BF16)],
        compiler_params=_params("parallel", "arbitrary", vmem=FFN_VMEM_LIMIT),
        name="ffn_half_step",
    )(x, norm_w.reshape(1, D).astype(F32), w_in, w_in, w_out, fw)


def _flash_kernel(*refs, R, E, has_kx, gate_cols, tq, tk, mode, chunks_per_version, depth):
    it = iter(refs)
    qm_ref = next(it)
    qx_ref = next(it) if E else None
    km_ref = next(it)
    kx_ref = next(it) if has_kx else None
    v_ref = next(it)
    gate_ref = next(it) if gate_cols else None
    cap_ref = next(it)
    o_ref = next(it)
    q_scr, m_scr, acc_scr, s_scr = it
    M = R * tq
    i = pl.program_id(2)

    for e in range(max(E, 1)):
        for r in range(R):
            q_scr[e, r * tq:(r + 1) * tq, 0:LANE] = qm_ref[0, :, r * LANE:(r + 1) * LANE]
            if E:
                q_scr[e, r * tq:(r + 1) * tq, LANE:2 * LANE] = qx_ref[0, :, e * LANE:(e + 1) * LANE]
    m_scr[...] = jnp.full_like(m_scr, NEG_INF)
    acc_scr[...] = jnp.zeros_like(acc_scr)
    ones_col = jnp.where(lax.broadcasted_iota(jnp.int32, (tk, LANE), 1) == 0, 1.0, 0.0).astype(BF16)

    def scores(c):
        k0 = pl.multiple_of(c * tk, tk)
        kt = km_ref[0, :, pl.ds(k0, tk)]
        if has_kx:
            kt = jnp.concatenate([kt, kx_ref[0, :, pl.ds(k0, tk)]], axis=0)
        q = q_scr[c // chunks_per_version] if E > 1 else q_scr[0]
        return jnp.dot(q, kt, preferred_element_type=F32).astype(BF16)

    def softmax_pv(c, s, cap=None):
        k0 = pl.multiple_of(c * tk, tk)
        if cap is not None:
            s = jnp.minimum(s, cap)
        m_prev = m_scr[...]
        m_new = jnp.maximum(m_prev, jnp.max(s, axis=1, keepdims=True).astype(F32))
        alpha = jnp.exp2(m_prev - m_new)
        p = jnp.exp2(s - pltpu.repeat(m_new.astype(BF16), tk // LANE, axis=1))
        v1 = jnp.concatenate([v_ref[0, pl.ds(k0, tk), :], ones_col], axis=1)
        acc_scr[...] = pltpu.repeat(alpha, 2, axis=1) * acc_scr[...] + jnp.dot(p, v1, preferred_element_type=F32)
        m_scr[...] = m_new

    if mode == "causal":
        n_full = (i * tq) // tk
        bufs = [s_scr.at[u] for u in range(depth)]
        bufs[0][...] = scores(0)

        def group(j, carry):
            for u in range(depth):
                bufs[(u + 1) % depth][...] = scores(depth * j + u + 1)
                softmax_pv(depth * j + u, bufs[u][...])
            return carry

        n_groups = n_full // depth
        lax.fori_loop(0, n_groups, group, 0)

        def tail(c, carry):
            softmax_pv(c, bufs[0][...])
            bufs[0][...] = scores(c + 1)
            return carry

        lax.fori_loop(depth * n_groups, n_full, tail, 0)
        softmax_pv(n_full, bufs[0][...], cap_ref[0])
    else:
        prev = jnp.maximum(i - 1, 0)
        s_prev = scores(prev)
        s_diag = scores(i)
        softmax_pv(prev, s_prev, jnp.where(i > 0, cap_ref[1], NEG_INF))
        softmax_pv(i, s_diag, cap_ref[0])

    out = acc_scr[:, 0:LANE] / acc_scr[:, LANE:LANE + 1]
    for r in range(R):
        o_r = out[r * tq:(r + 1) * tq]
        if gate_cols:
            o_r = o_r * gate_ref[0, :, gate_cols[r]:gate_cols[r] + 1]
        o_ref[0, :, r * LANE:(r + 1) * LANE] = o_r.astype(o_ref.dtype)


def flash_attention(qm, qx, km, kx, v, gate=None, *, n_groups, R, E, qm_col, qx_col=None, km_col, v_col, kx_batched,
                    gate_cols=None, mode, window=0, tq, tk, depth=2, keys_per_version=None):
    B, S, _ = qm.shape
    tq, tk = _pick(S, tq), _pick(S, tk)
    assert tq == tk
    assert mode == "causal" or tq == window
    has_kx = kx is not None
    dk = 2 * LANE if has_kx else LANE
    cpv = (keys_per_version // tk) if keys_per_version else 1
    M = R * tq
    in_specs = [pl.BlockSpec((1, tq, R * LANE), lambda b, g, i: (b, i, qm_col(g)))]
    args = [qm]
    if E:
        in_specs.append(pl.BlockSpec((1, tq, E * LANE), lambda b, g, i: (b, i, qx_col(g))))
        args.append(qx)
    in_specs.append(pl.BlockSpec((1, LANE, S), lambda b, g, i: (b, km_col(g), 0)))
    args.append(km)
    if has_kx:
        in_specs.append(pl.BlockSpec((1, LANE, S), lambda b, g, i: (b if kx_batched else 0, 0, 0)))
        args.append(kx)
    in_specs.append(pl.BlockSpec((1, S, LANE), lambda b, g, i: (b, 0, v_col(g))))
    args.append(v)
    if gate_cols:
        in_specs.append(pl.BlockSpec((1, tq, LANE), lambda b, g, i: (b, i, g)))
        args.append(gate)
    q_off = lax.broadcasted_iota(jnp.int32, (M, tk), 0) % tq
    k_off = lax.broadcasted_iota(jnp.int32, (M, tk), 1)
    caps = [k_off <= q_off] + ([k_off > q_off] if mode == "window" else [])
    cap = jnp.stack([jnp.where(c, MASK_PASS, NEG_INF).astype(BF16) for c in caps])
    in_specs.append(pl.BlockSpec(cap.shape, lambda b, g, i: (0, 0, 0)))
    args.append(cap)
    kern = functools.partial(_flash_kernel, R=R, E=E, has_kx=has_kx, gate_cols=gate_cols, tq=tq, tk=tk, mode=mode,
                             chunks_per_version=cpv, depth=depth)
    return pl.pallas_call(
        kern,
        grid=(B, n_groups, S // tq),
        in_specs=in_specs,
        out_specs=pl.BlockSpec((1, tq, R * LANE), lambda b, g, i: (b, i, g)),
        out_shape=jax.ShapeDtypeStruct((B, S, n_groups * R * LANE), BF16),
        scratch_shapes=[
            pltpu.VMEM((max(E, 1), M, dk), BF16),
            pltpu.VMEM((M, LANE), F32),
            pltpu.VMEM((M, 2 * LANE), F32),
            pltpu.VMEM((depth, M, tk), BF16),
        ],
        compiler_params=_params("parallel", "parallel", "arbitrary"),
        name="flash_" + mode + ("_x%d" % E if E else ""),
    )(*args)


def _compress_kernel(tok_ref, pe_ref, w1_ref, w2_ref, c2_ref, s2_ref, o_ref, *, ncp):
    half = NSA_CMP_STRIDE
    a = b = None
    for l in range(half):
        x = tok_ref[0, pl.ds(l, ncp, stride=half), :]
        da = jnp.dot((x + pe_ref[0, l:l + 1, :]).astype(BF16), w1_ref[0, l], preferred_element_type=F32)
        db = jnp.dot((x + pe_ref[0, half + l:half + l + 1, :]).astype(BF16), w1_ref[0, half + l],
                     preferred_element_type=F32)
        a = da if a is None else a + da
        b = db if b is None else b + db
    h = jax.nn.gelu(a + pltpu.roll(b, ncp - 1, axis=0))
    y = jnp.dot(h.astype(BF16), w2_ref[0], preferred_element_type=F32)
    is_key = pl.program_id(0) == 0
    c2 = jnp.where(is_key, c2_ref[0], 1.0)
    s2 = jnp.where(is_key, s2_ref[0], 0.0)
    o_ref[0, 0, 0] = _rotate_half(y, c2, s2).astype(o_ref.dtype)


def compress_tokens(tok, pe, w1, w2, c2, s2):
    B, S, _ = tok.shape
    G, dk = NSA_KV_GROUPS, NSA_HEAD_DIM
    ncp = S // NSA_CMP_STRIDE
    hidden = w2.shape[1]
    return pl.pallas_call(
        functools.partial(_compress_kernel, ncp=ncp),
        grid=(2, B, G),
        in_specs=[
            pl.BlockSpec((1, S, dk), lambda s, b, g: (b, 0, s * G + g)),
            pl.BlockSpec((1, NSA_CMP_LEN, dk), lambda s, b, g: (s, 0, 0)),
            pl.BlockSpec((1, NSA_CMP_LEN, dk, hidden), lambda s, b, g: (s, 0, 0, 0)),
            pl.BlockSpec((1, hidden, dk), lambda s, b, g: (s, 0, 0)),
            pl.BlockSpec((1, ncp, dk), lambda s, b, g: (b, 0, 0)),
            pl.BlockSpec((1, ncp, dk), lambda s, b, g: (b, 0, 0)),
        ],
        out_specs=pl.BlockSpec((1, 1, 1, ncp, dk), lambda s, b, g: (s, b, g, 0, 0)),
        out_shape=jax.ShapeDtypeStruct((2, B, G, ncp, dk), BF16),
        compiler_params=_params("parallel", "parallel", "parallel"),
        name="compress_tokens",
    )(tok, pe, w1, w2, c2, s2)


def _cmp_select_kernel(q_ref, kc_ref, vc_ref, ovt_ref, gate_ref, *rest, R, tq, n_sub, tile0, n_cmp, n_top):
    oc_ref, nm_ref = rest[-2:]
    for u in range(n_sub):
        rows = slice(u * tq, (u + 1) * tq)
        t0 = (tile0 + pl.program_id(2) * n_sub + u) * tq
        _cmp_select_tile(q_ref.at[0, rows], kc_ref, vc_ref, ovt_ref, gate_ref.at[0, rows], oc_ref.at[0, rows],
                         nm_ref.at[0, rows], t0, R=R, tq=tq, n_cmp=n_cmp, n_top=n_top)


def _cmp_select_tile(q_ref, kc_ref, vc_ref, ovt_ref, gate_ref, oc_ref, nm_ref, t0, *, R, tq, n_cmp, n_top):
    M = R * tq
    q = jnp.concatenate([q_ref[:, r * LANE:(r + 1) * LANE] for r in range(R)], axis=0)
    kc = kc_ref[0, 0, 0]
    ncp = kc.shape[0]
    s = lax.dot_general(q, kc, (((1,), (1,)), ((), ())), preferred_element_type=F32)
    t = t0 + _mod_pow2(lax.broadcasted_iota(jnp.int32, (M, 1), 0), tq)
    lim = jnp.minimum(_div_pow2(t - (NSA_CMP_LEN - 1), NSA_CMP_STRIDE) + 1, n_cmp)
    valid = lax.broadcasted_iota(jnp.int32, (M, ncp), 1) < lim
    sm = jnp.where(valid, s, NEG_INF)
    e = jnp.exp2((sm - jnp.max(sm, axis=1, keepdims=True)).astype(BF16))
    ones_col = jnp.where(lax.broadcasted_iota(jnp.int32, (ncp, LANE), 1) == 0, 1.0, 0.0).astype(BF16)
    oc = jnp.dot(e, jnp.concatenate([vc_ref[0, 0, 0], ones_col], axis=1), preferred_element_type=F32)
    oc = oc[:, 0:LANE] * jnp.where(lim > 0, 1.0 / oc[:, LANE:LANE + 1], 0.0)
    for r in range(R):
        gate = gate_ref[:, NSA_N_BRANCHES * r:NSA_N_BRANCHES * r + 1]
        oc_ref[:, r * LANE:(r + 1) * LANE] = (oc[r * tq:(r + 1) * tq] * gate).astype(oc_ref.dtype)

    nb = ovt_ref.shape[0] - BF16_ROWS
    imp_un = lax.dot_general(ovt_ref[...], e, (((1,), (1,)), ((), ())), preferred_element_type=F32)
    t_l = t0 + _mod_pow2(lax.broadcasted_iota(jnp.int32, (1, M), 1), tq)
    lim_l = jnp.minimum(_div_pow2(t_l - (NSA_CMP_LEN - 1), NSA_CMP_STRIDE) + 1, n_cmp)
    imp_h = imp_un[0:nb] * jnp.where(lim_l > 0, 1.0 / imp_un[nb:nb + 1], 0.0)
    imp = imp_h[:, 0:tq]
    for r in range(1, R):
        imp = imp + imp_h[:, r * tq:(r + 1) * tq]
    j = lax.broadcasted_iota(jnp.int32, (nb, tq), 0)
    jf = j.astype(F32)
    cur = _div_pow2(t0 + lax.broadcasted_iota(jnp.int32, (nb, tq), 1), NSA_SEL_BLOCK)
    forced = jnp.logical_or(j == 0, jnp.logical_or(j == cur, j == cur - 1))
    x = jnp.where(j > cur, NEG_INF, imp + FORCE_BONUS * forced.astype(F32))
    for _ in range(n_top):
        mx = jnp.max(x, axis=0, keepdims=True)
        first = jnp.min(jnp.where(x == mx, jf, float(nb)), axis=0, keepdims=True)
        x = jnp.where(jf == first, jnp.where(mx > 0.5 * NEG_INF, -jnp.inf, NEG_INF), x)
    nm_ref[:, 0:nb] = jnp.where(x == -jnp.inf, 0.0, NEG_INF).T.astype(nm_ref.dtype)
    if nb < nm_ref.shape[1]:
        nm_ref[:, nb:] = jnp.full((tq, nm_ref.shape[1] - nb), NEG_INF, nm_ref.dtype)


def cmp_attention_select(q, kv_cmp, overlap, gate, *, n_cmp, n_top, tq=128, n_sub=2):
    B, S, _ = q.shape
    G, R = NSA_KV_GROUPS, NSA_R
    NB, NCP = overlap.shape
    tq = _pick(S, tq)
    tb = n_sub * tq
    n_parts = next(n for n in (4, 2, 1) if NCP % (n * LANE) == 0 and S % (n * tb) == 0)
    steps = S // (n_parts * tb)
    outs = None
    for part in range(n_parts):
        ncp_w = NCP * (part + 1) // n_parts
        nb_w = min(NB, -(-(S * (part + 1) // n_parts // NSA_SEL_BLOCK) // LANE) * LANE)
        ov = jnp.concatenate([overlap[:nb_w, :ncp_w], jnp.ones((1, ncp_w), BF16),
                              jnp.zeros((BF16_ROWS - 1, ncp_w), BF16)], axis=0)
        kern = functools.partial(_cmp_select_kernel, R=R, tq=tq, n_sub=n_sub, tile0=part * steps * n_sub,
                                 n_cmp=n_cmp, n_top=n_top)
        off = part * steps
        carried = [] if outs is None else list(outs)
        outs = pl.pallas_call(
            kern,
            grid=(B, G, steps),
            in_specs=[
                pl.BlockSpec((1, tb, R * LANE), lambda b, g, i, off=off: (b, i + off, g)),
                pl.BlockSpec((1, 1, 1, ncp_w, LANE), lambda b, g, i: (0, b, g, 0, 0)),
                pl.BlockSpec((1, 1, 1, ncp_w, LANE), lambda b, g, i: (1, b, g, 0, 0)),
                pl.BlockSpec((nb_w + BF16_ROWS, ncp_w), lambda b, g, i: (0, 0)),
                pl.BlockSpec((1, tb, LANE), lambda b, g, i, off=off: (b, i + off, g)),
            ] + [pl.BlockSpec(memory_space=pl.ANY)] * len(carried),
            out_specs=[
                pl.BlockSpec((1, tb, R * LANE), lambda b, g, i, off=off: (b, i + off, g)),
                pl.BlockSpec((1, tb, NB), lambda b, g, i, off=off: (b, i + off, g)),
            ],
            out_shape=[
                jax.ShapeDtypeStruct((B, S, G * R * LANE), BF16),
                jax.ShapeDtypeStruct((B, S, G * NB), BF16),
            ],
            input_output_aliases={5 + k: k for k in range(len(carried))},
            compiler_params=_params("parallel", "parallel", "arbitrary"),
            name="cmp_attention_select",
        )(q, kv_cmp, kv_cmp, ov, gate, *carried)
    return outs


def _rope_tables(pos, dim):
    inv = 1.0 / (ROPE_THETA ** (jnp.arange(0, dim, 2, dtype=F32) / dim))
    ang = pos.astype(F32)[..., None] * inv
    pad = [(0, 0)] * (ang.ndim - 1) + [(0, (LANE - dim) // 2)]
    c, s = jnp.pad(jnp.cos(ang), pad), jnp.pad(jnp.sin(ang), pad)
    return jnp.concatenate([c, c], axis=-1), jnp.concatenate([-s, s], axis=-1)


def _spread_rope_cols(w):
    dim = w.shape[-1]
    pad = [(0, 0)] * (w.ndim - 1) + [(0, (LANE - dim) // 2)]
    return jnp.concatenate([jnp.pad(w[..., :dim // 2], pad), jnp.pad(w[..., dim // 2:], pad)], axis=-1)


def nsa_mixer(x, B, S, positions, norm_w, w_in, cmp_pe, cmp_w1, cmp_w2, w_out):
    T, D = x.shape
    H, G, R, dk = NSA_HEADS, NSA_KV_GROUPS, NSA_R, NSA_HEAD_DIM
    q_cols, kv_cols = H * dk, NSA_N_BRANCHES * 2 * G * dk
    grp = G * dk
    q_scale = dk ** -0.5 * LOG2E
    tables = _rope_tables(positions.reshape(T), dk)

    w_qkv = w_in[:, :q_cols + kv_cols].astype(BF16)
    nq = q_cols // grp
    qkv = proj(x, norm_w, w_qkv, n_blocks=nq + 4, tn=grp, col_map=lambda j: jnp.where(j < nq, j, j + 2),
               rope=tables, rope_mask=(1 << nq) - 1 | 0b0101 << nq, scale_mask=(1 << nq) - 1, scale=q_scale,
               out_dtype=BF16).reshape(B, S, q_cols + 4 * grp)
    cmp_tok = proj(x, norm_w, w_qkv, n_blocks=1, tn=2 * grp, col_map=lambda j: q_cols // (2 * grp)).reshape(B, S, 2 * grp)
    w_gate = w_in[:, q_cols + kv_cols:].reshape(D, G, R * NSA_N_BRANCHES)
    w_gate = jnp.pad(w_gate, ((0, 0), (0, 0), (0, LANE - R * NSA_N_BRANCHES))).reshape(D, G * LANE).astype(BF16)
    gates = proj(x, norm_w, w_gate, n_blocks=1, tn=G * LANE, sigmoid=True).reshape(B, S, G * LANE)

    ncp = S // NSA_CMP_STRIDE
    n_cmp = (S - NSA_CMP_LEN) // NSA_CMP_STRIDE + 1
    cmp_end = jnp.minimum(jnp.arange(ncp) * NSA_CMP_STRIDE + NSA_CMP_LEN - 1, S - 1)
    c2c, s2c = _rope_tables(positions[:, cmp_end], dk)
    hidden = cmp_w1.shape[-1]
    kv_cmp = compress_tokens(cmp_tok, cmp_pe, cmp_w1.reshape(2, NSA_CMP_LEN, dk, hidden).astype(BF16),
                             cmp_w2.astype(BF16), c2c, s2c)

    n_sel = S // NSA_SEL_BLOCK
    nb = -(-n_sel // LANE) * LANE
    blk_start = jnp.arange(ncp) * NSA_CMP_STRIDE
    sel_start = jnp.arange(nb) * NSA_SEL_BLOCK
    overlap = ((blk_start[None, :] <= sel_start[:, None] + NSA_SEL_BLOCK - 1)
               & (blk_start[None, :] + NSA_CMP_LEN - 1 >= sel_start[:, None])
               & (jnp.arange(ncp)[None, :] < n_cmp) & (jnp.arange(nb)[:, None] < n_sel)).astype(BF16)
    o_cmp, neg_mask = cmp_attention_select(qkv, kv_cmp, overlap, gates, n_cmp=n_cmp, n_top=min(NSA_N_SELECT, n_sel))

    blk_onehot = (jnp.arange(LANE)[:, None] == jnp.arange(S)[None, :] // NSA_SEL_BLOCK % LANE).astype(BF16)[None]
    kv0 = q_cols // dk
    k_slc_t = jnp.swapaxes(qkv[:, :, q_cols:q_cols + grp], 1, 2)
    k_win_t = jnp.swapaxes(qkv[:, :, q_cols + 2 * grp:q_cols + 3 * grp], 1, 2)
    o_slc = flash_attention(qkv, neg_mask, k_slc_t, blk_onehot, qkv, gates, n_groups=G, R=R, E=nb // LANE,
                            qm_col=lambda g: g, qx_col=lambda g: g, km_col=lambda g: g,
                            v_col=lambda g: kv0 + G + g, kx_batched=False,
                            gate_cols=tuple(NSA_N_BRANCHES * r + 1 for r in range(R)), mode="causal",
                            tq=512, tk=512, depth=4, keys_per_version=LANE * NSA_SEL_BLOCK)
    o_win = flash_attention(qkv, None, k_win_t, None, qkv, gates, n_groups=G, R=R, E=0,
                            qm_col=lambda g: g, km_col=lambda g: g, v_col=lambda g: kv0 + 3 * G + g,
                            kx_batched=False, gate_cols=tuple(NSA_N_BRANCHES * r + 2 for r in range(R)),
                            mode="window", window=NSA_WINDOW, tq=512, tk=512)
    return matmul_residual([o.reshape(T, H * dk) for o in (o_cmp, o_slc, o_win)], w_out.astype(BF16), x)


def mla_mixer(x, B, S, positions, norm_w, w_in, q_norm_w, kv_norm_w, w_uq, w_ukv, w_out):
    T, D = x.shape
    H = MLA_HEADS
    q_rank, kv_rank = q_norm_w.shape[0], kv_norm_w.shape[0]
    assert q_rank == kv_rank and q_rank % LANE == 0
    q_scale = MLA_QK_DIM ** -0.5 * LOG2E
    tables = _rope_tables(positions.reshape(T), MLA_ROPE_DIM)

    latent = proj(x, norm_w, w_in[:, :2 * q_rank].astype(BF16), n_blocks=1, tn=2 * q_rank)
    w_kr = _spread_rope_cols(w_in[:, 2 * q_rank:]).astype(BF16)
    k_rope = proj(x, norm_w, w_kr, n_blocks=1, tn=LANE, rope=tables, rope_mask=1, out_dtype=BF16).reshape(B, S, LANE)
    w_uq3 = w_uq.reshape(q_rank, H, MLA_QK_DIM)
    w_uq_p = jnp.concatenate([w_uq3[:, :, :MLA_NOPE_DIM].reshape(q_rank, H * MLA_NOPE_DIM),
                              _spread_rope_cols(w_uq3[:, :, MLA_NOPE_DIM:]).reshape(q_rank, H * LANE)],
                             axis=1).astype(BF16)
    tn = 4 * LANE
    n_nope = H * MLA_NOPE_DIM // tn
    n_all = n_nope + H * LANE // tn
    q = proj(latent, q_norm_w, w_uq_p, n_blocks=n_all, tn=tn, x_col_block=0, rope=tables,
             rope_mask=(1 << n_all) - (1 << n_nope), scale_mask=(1 << n_all) - 1, scale=q_scale,
             out_dtype=BF16).reshape(B, S, n_all * tn)
    kvb = proj(latent, kv_norm_w, w_ukv.astype(BF16), n_blocks=w_ukv.shape[1] // (8 * LANE), tn=8 * LANE,
               x_col_block=1, out_dtype=BF16).reshape(B, S, 2 * H * LANE)

    k_nope_t = kvb.reshape(B, S, H, 2, LANE)[:, :, :, 0].transpose(0, 2, 3, 1).reshape(B, H * LANE, S)
    o = flash_attention(q, q, k_nope_t, jnp.swapaxes(k_rope, 1, 2), kvb, n_groups=H, R=1, E=1,
                        qm_col=lambda h: h, qx_col=lambda h: H + h, km_col=lambda h: h, v_col=lambda h: 2 * h + 1,
                        kx_batched=True, mode="causal", tq=1024, tk=1024, depth=4)
    return matmul_residual([o.reshape(T, H * MLA_V_DIM)], w_out.astype(BF16), x)


def kernel(x, positions, ffn_norm_w, ffn_w_in, ffn_w_out, mix_norm_w, nsa_w_in, nsa_cmp_pe, nsa_cmp_w1, nsa_cmp_w2,
           nsa_w_out, mla_w_in, mla_q_norm_w, mla_kv_norm_w, mla_w_uq, mla_w_ukv, mla_w_out, final_norm_w):
    B, S, D = x.shape
    depth = ffn_norm_w.shape[0]
    n_mixers = 2
    h = x.reshape(B * S, D)
    for i in range(depth):
        h = ffn_half_step(h, ffn_norm_w[i, 0], ffn_w_in[i, 0].astype(BF16), ffn_w_out[i, 0].astype(BF16))
        j = i // n_mixers
        if i % n_mixers == 0:
            h = nsa_mixer(h, B, S, positions, mix_norm_w[i], nsa_w_in[j], nsa_cmp_pe[j], nsa_cmp_w1[j],
                          nsa_cmp_w2[j], nsa_w_out[j])
        else:
            h = mla_mixer(h, B, S, positions, mix_norm_w[i], mla_w_in[j], mla_q_norm_w[j], mla_kv_norm_w[j],
                          mla_w_uq[j], mla_w_ukv[j], mla_w_out[j])
        last = i == depth - 1
        h = ffn_half_step(h, ffn_norm_w[i, 1], ffn_w_in[i, 1].astype(BF16), ffn_w_out[i, 1].astype(BF16),
                          final_w=final_norm_w if last else None)
    return h.reshape(B, S, D)
```

```python
import functools
import math

import jax
import jax.numpy as jnp
from jax import lax
from jax.experimental import pallas as pl
from jax.experimental.pallas import tpu as pltpu

F32 = jnp.float32
BF16 = jnp.bfloat16

LANE = 128
BF16_ROWS = 16
RMS_EPS = 1e-6
ROPE_THETA = 10000.0
NEG_INF = -1e30
MASK_PASS = 3e38
FORCE_BONUS = 1e9
LOG2E = math.log2(math.e)
VMEM_LIMIT = 56 * 1024 * 1024
FFN_VMEM_LIMIT = 61 * 1024 * 1024
FFN_SLAB = 64

NSA_HEADS = 16
NSA_HEAD_DIM = 128
NSA_KV_GROUPS = 4
NSA_R = NSA_HEADS // NSA_KV_GROUPS
NSA_N_BRANCHES = 3
NSA_CMP_LEN = 32
NSA_CMP_STRIDE = 16
NSA_SEL_BLOCK = 64
NSA_N_SELECT = 16
NSA_WINDOW = 512

MLA_HEADS = 16
MLA_NOPE_DIM = 128
MLA_ROPE_DIM = 64
MLA_V_DIM = 128
MLA_QK_DIM = MLA_NOPE_DIM + MLA_ROPE_DIM


def _pick(n, pref):
    if n <= pref:
        return n
    t = (pref // LANE) * LANE
    while t > LANE and n % t:
        t -= LANE
    assert n % t == 0, (n, pref)
    return t


def _div_pow2(x, d):
    assert d & (d - 1) == 0, d
    return jnp.right_shift(x, d.bit_length() - 1)


def _mod_pow2(x, d):
    assert d & (d - 1) == 0, d
    return jnp.bitwise_and(x, d - 1)


def _params(*sem, vmem=VMEM_LIMIT):
    return pltpu.CompilerParams(dimension_semantics=sem, vmem_limit_bytes=vmem)


def _rotate_half(y, c2, s2):
    return y * c2 + pltpu.roll(y, LANE // 2, axis=1) * s2


def _rms_rows(x, w):
    ms = jnp.mean(x * x, axis=-1, keepdims=True)
    return x * lax.rsqrt(ms + RMS_EPS) * w


def _proj_kernel(*refs, has_rope, rope_mask, scale_mask, scale, sigmoid, transposed):
    if has_rope:
        x_ref, nw_ref, w_ref, c2_ref, s2_ref, o_ref, xn_scr = refs
    else:
        x_ref, nw_ref, w_ref, o_ref, xn_scr = refs
    j = pl.program_id(1)

    @pl.when(j == 0)
    def _():
        xn_scr[...] = _rms_rows(x_ref[...], nw_ref[...]).astype(BF16)

    y = jnp.dot(xn_scr[...], w_ref[...], preferred_element_type=F32)
    if sigmoid:
        y = jax.nn.sigmoid(y)
    if has_rope:
        rope_on = jnp.bitwise_and(lax.shift_right_logical(jnp.int32(rope_mask), j), 1) == 1
        factor = jnp.where(jnp.bitwise_and(lax.shift_right_logical(jnp.int32(scale_mask), j), 1) == 1, scale, 1.0)
        c2 = jnp.where(rope_on, c2_ref[...], 1.0) * factor
        s2 = jnp.where(rope_on, s2_ref[...], 0.0) * factor
    for h in range(y.shape[1] // LANE):
        slab = y[:, h * LANE:(h + 1) * LANE]
        if has_rope:
            slab = _rotate_half(slab, c2, s2)
        if transposed:
            o_ref[h * LANE:(h + 1) * LANE, :] = slab.T.astype(o_ref.dtype)
        else:
            o_ref[:, h * LANE:(h + 1) * LANE] = slab.astype(o_ref.dtype)


def proj(x, norm_w, w, *, n_blocks, tn, col_map=None, x_col_block=0, rope=None, rope_mask=0, scale_mask=0, scale=1.0,
         sigmoid=False, transposed=False, out_dtype=F32, tm=1024):
    T = x.shape[0]
    K = w.shape[0]
    tm = _pick(T, tm)
    col_map = col_map or (lambda j: j)
    has_rope = rope is not None
    assert has_rope or scale_mask == 0
    in_specs = [
        pl.BlockSpec((tm, K), lambda i, j: (i, x_col_block)),
        pl.BlockSpec((1, K), lambda i, j: (0, 0)),
        pl.BlockSpec((K, tn), lambda i, j: (0, col_map(j))),
    ]
    args = [x, norm_w.reshape(1, K).astype(F32), w]
    if has_rope:
        in_specs += [pl.BlockSpec((tm, LANE), lambda i, j: (i, 0))] * 2
        args += list(rope)
    kern = functools.partial(_proj_kernel, has_rope=has_rope, rope_mask=rope_mask, scale_mask=scale_mask,
                             scale=scale, sigmoid=sigmoid, transposed=transposed)
    if transposed:
        out_spec = pl.BlockSpec((tn, tm), lambda i, j: (j, i))
        out_shape = jax.ShapeDtypeStruct((n_blocks * tn, T), out_dtype)
    else:
        out_spec = pl.BlockSpec((tm, tn), lambda i, j: (i, j))
        out_shape = jax.ShapeDtypeStruct((T, n_blocks * tn), out_dtype)
    return pl.pallas_call(
        kern,
        grid=(T // tm, n_blocks),
        in_specs=in_specs,
        out_specs=out_spec,
        out_shape=out_shape,
        scratch_shapes=[pltpu.VMEM((tm, K), BF16)],
        compiler_params=_params("parallel", "arbitrary"),
        name="proj",
    )(*args)


def _matmul_residual_kernel(*refs):
    *a_refs, w_ref, r_ref, o_ref = refs
    a = a_refs[0][...]
    if len(a_refs) > 1:
        a = a.astype(F32)
        for a_ref in a_refs[1:]:
            a = a + a_ref[...].astype(F32)
        a = a.astype(BF16)
    o_ref[...] = r_ref[...] + jnp.dot(a, w_ref[...], preferred_element_type=F32)


def matmul_residual(a_list, w, res, *, tm=1024, tn=1024):
    T, K = a_list[0].shape
    N = w.shape[1]
    tm, tn = _pick(T, tm), _pick(N, tn)
    return pl.pallas_call(
        _matmul_residual_kernel,
        grid=(T // tm, N // tn),
        in_specs=[pl.BlockSpec((tm, K), lambda i, j: (i, 0))] * len(a_list) + [
            pl.BlockSpec((K, tn), lambda i, j: (0, j)),
            pl.BlockSpec((tm, tn), lambda i, j: (i, j)),
        ],
        out_specs=pl.BlockSpec((tm, tn), lambda i, j: (i, j)),
        out_shape=jax.ShapeDtypeStruct((T, N), F32),
        compiler_params=_params("parallel", "arbitrary"),
        name="matmul_residual",
    )(*a_list, w, res)


def _ffn_kernel(x_ref, nw_ref, wg_ref, wu_ref, wo_ref, fw_ref, o_ref, xn_scr, *, final_norm):
    f = pl.program_id(1)
    tm = x_ref.shape[0]
    slab = min(FFN_SLAB, tm)

    def over_slabs(fn):
        def body(r, carry):
            fn(pl.ds(pl.multiple_of(r * slab, slab), slab))
            return carry

        lax.fori_loop(0, tm // slab, body, 0)

    @pl.when(f == 0)
    def _():
        def prologue(rows):
            xn_scr[rows, :] = _rms_rows(x_ref[rows, :], nw_ref[...]).astype(BF16)

        over_slabs(prologue)
        o_ref[...] = jnp.zeros_like(o_ref)

    xn = xn_scr[...]
    g = jnp.dot(xn, wg_ref[...], preferred_element_type=F32)
    u = jnp.dot(xn, wu_ref[...], preferred_element_type=F32)
    h = (g * jax.nn.sigmoid(g) * u).astype(BF16)
    o_ref[...] += jnp.dot(h, wo_ref[...], preferred_element_type=F32)

    @pl.when(f == pl.num_programs(1) - 1)
    def _():
        def epilogue(rows):
            y = x_ref[rows, :] + 0.5 * o_ref[rows, :]
            if final_norm:
                y = _rms_rows(y, fw_ref[...])
            o_ref[rows, :] = y

        over_slabs(epilogue)


def ffn_half_step(x, norm_w, w_in, w_out, final_w=None, *, tm=1024, tf=512):
    T, D = x.shape
    DFF = w_out.shape[0]
    tm, tf = _pick(T, tm), _pick(DFF, tf)
    nf = DFF // tf
    final_norm = final_w is not None
    fw = (final_w if final_norm else norm_w).reshape(1, D).astype(F32)
    return pl.pallas_call(
        functools.partial(_ffn_kernel, final_norm=final_norm),
        grid=(T // tm, nf),
        in_specs=[
            pl.BlockSpec((tm, D), lambda i, f: (i, 0)),
            pl.BlockSpec((1, D), lambda i, f: (0, 0)),
            pl.BlockSpec((D, tf), lambda i, f: (0, f)),
            pl.BlockSpec((D, tf), lambda i, f: (0, f + nf)),
            pl.BlockSpec((tf, D), lambda i, f: (f, 0)),
            pl.BlockSpec((1, D), lambda i, f: (0, 0)),
        ],
        out_specs=pl.BlockSpec((tm, D), lambda i, f: (i, 0)),
        out_shape=jax.ShapeDtypeStruct((T, D), F32),
        scratch_shapes=[pltpu.VMEM((tm, D), BF16)],
        compiler_params=_params("parallel", "arbitrary", vmem=FFN_VMEM_LIMIT),
        name="ffn_half_step",
    )(x, norm_w.reshape(1, D).astype(F32), w_in, w_in, w_out, fw)


def _flash_kernel(*refs, R, E, has_kx, gate_cols, tq, tk, mode, chunks_per_version, depth):
    it = iter(refs)
    qm_ref = next(it)
    qx_ref = next(it) if E else None
    km_ref = next(it)
    kx_ref = next(it) if has_kx else None
    v_ref = next(it)
    gate_ref = next(it) if gate_cols else None
    cap_ref = next(it)
    o_ref = next(it)
    q_scr, m_scr, acc_scr, s_scr = it
    M = R * tq
    i = pl.program_id(2)

    for e in range(max(E, 1)):
        for r in range(R):
            q_scr[e, r * tq:(r + 1) * tq, 0:LANE] = qm_ref[0, :, r * LANE:(r + 1) * LANE]
            if E:
                q_scr[e, r * tq:(r + 1) * tq, LANE:2 * LANE] = qx_ref[0, :, e * LANE:(e + 1) * LANE]
    m_scr[...] = jnp.full_like(m_scr, NEG_INF)
    acc_scr[...] = jnp.zeros_like(acc_scr)
    ones_col = jnp.where(lax.broadcasted_iota(jnp.int32, (tk, LANE), 1) == 0, 1.0, 0.0).astype(BF16)

    def scores(c):
        k0 = pl.multiple_of(c * tk, tk)
        kt = km_ref[:, pl.ds(k0, tk)]
        if has_kx:
            kt = jnp.concatenate([kt, kx_ref[:, pl.ds(k0, tk)]], axis=0)
        q = q_scr[c // chunks_per_version] if E > 1 else q_scr[0]
        return jnp.dot(q, kt, preferred_element_type=F32).astype(BF16)

    def softmax_pv(c, s, cap=None):
        k0 = pl.multiple_of(c * tk, tk)
        if cap is not None:
            s = jnp.minimum(s, cap)
        m_prev = m_scr[...]
        m_new = jnp.maximum(m_prev, jnp.max(s, axis=1, keepdims=True).astype(F32))
        alpha = jnp.exp2(m_prev - m_new)
        p = jnp.exp2(s - pltpu.repeat(m_new.astype(BF16), tk // LANE, axis=1))
        v1 = jnp.concatenate([v_ref[0, pl.ds(k0, tk), :], ones_col], axis=1)
        acc_scr[...] = pltpu.repeat(alpha, 2, axis=1) * acc_scr[...] + jnp.dot(p, v1, preferred_element_type=F32)
        m_scr[...] = m_new

    if mode == "causal":
        n_full = (i * tq) // tk
        bufs = [s_scr.at[u] for u in range(depth)]
        bufs[0][...] = scores(0)

        def group(j, carry):
            for u in range(depth):
                bufs[(u + 1) % depth][...] = scores(depth * j + u + 1)
                softmax_pv(depth * j + u, bufs[u][...])
            return carry

        n_groups = n_full // depth
        lax.fori_loop(0, n_groups, group, 0)

        def tail(c, carry):
            softmax_pv(c, bufs[0][...])
            bufs[0][...] = scores(c + 1)
            return carry

        lax.fori_loop(depth * n_groups, n_full, tail, 0)
        softmax_pv(n_full, bufs[0][...], cap_ref[0])
    else:
        prev = jnp.maximum(i - 1, 0)
        s_prev = scores(prev)
        s_diag = scores(i)
        softmax_pv(prev, s_prev, jnp.where(i > 0, cap_ref[1], NEG_INF))
        softmax_pv(i, s_diag, cap_ref[0])

    out = acc_scr[:, 0:LANE] / acc_scr[:, LANE:LANE + 1]
    for r in range(R):
        o_r = out[r * tq:(r + 1) * tq]
        if gate_cols:
            o_r = o_r * gate_ref[0, :, gate_cols[r]:gate_cols[r] + 1]
        o_ref[0, :, r * LANE:(r + 1) * LANE] = o_r.astype(o_ref.dtype)


def flash_attention(qm, qx, km, kx, v, gate=None, *, n_groups, R, E, qm_col, qx_col=None, km_col, v_col, kx_batched,
                    gate_cols=None, mode, window=0, tq, tk, depth=2, keys_per_version=None):
    B, S, _ = qm.shape
    tq, tk = _pick(S, tq), _pick(S, tk)
    assert tq == tk
    assert mode == "causal" or tq == window
    has_kx = kx is not None
    dk = 2 * LANE if has_kx else LANE
    cpv = (keys_per_version // tk) if keys_per_version else 1
    M = R * tq
    in_specs = [pl.BlockSpec((1, tq, R * LANE), lambda b, g, i: (b, i, qm_col(g)))]
    args = [qm]
    if E:
        in_specs.append(pl.BlockSpec((1, tq, E * LANE), lambda b, g, i: (b, i, qx_col(g))))
        args.append(qx)
    in_specs.append(pl.BlockSpec((LANE, S), lambda b, g, i: (km_col(g), b)))
    args.append(km)
    if has_kx:
        in_specs.append(pl.BlockSpec((LANE, S), lambda b, g, i: (0, b if kx_batched else 0)))
        args.append(kx)
    in_specs.append(pl.BlockSpec((1, S, LANE), lambda b, g, i: (b, 0, v_col(g))))
    args.append(v)
    if gate_cols:
        in_specs.append(pl.BlockSpec((1, tq, LANE), lambda b, g, i: (b, i, g)))
        args.append(gate)
    q_off = lax.broadcasted_iota(jnp.int32, (M, tk), 0) % tq
    k_off = lax.broadcasted_iota(jnp.int32, (M, tk), 1)
    caps = [k_off <= q_off] + ([k_off > q_off] if mode == "window" else [])
    cap = jnp.stack([jnp.where(c, MASK_PASS, NEG_INF).astype(BF16) for c in caps])
    in_specs.append(pl.BlockSpec(cap.shape, lambda b, g, i: (0, 0, 0)))
    args.append(cap)
    kern = functools.partial(_flash_kernel, R=R, E=E, has_kx=has_kx, gate_cols=gate_cols, tq=tq, tk=tk, mode=mode,
                             chunks_per_version=cpv, depth=depth)
    return pl.pallas_call(
        kern,
        grid=(B, n_groups, S // tq),
        in_specs=in_specs,
        out_specs=pl.BlockSpec((1, tq, R * LANE), lambda b, g, i: (b, i, g)),
        out_shape=jax.ShapeDtypeStruct((B, S, n_groups * R * LANE), BF16),
        scratch_shapes=[
            pltpu.VMEM((max(E, 1), M, dk), BF16),
            pltpu.VMEM((M, LANE), F32),
            pltpu.VMEM((M, 2 * LANE), F32),
            pltpu.VMEM((depth, M, tk), BF16),
        ],
        compiler_params=_params("parallel", "parallel", "arbitrary"),
        name="flash_" + mode + ("_x%d" % E if E else ""),
    )(*args)


def _compress_kernel(tok_ref, pe_ref, w1_ref, w2_ref, c2_ref, s2_ref, o_ref, *, ncp):
    half = NSA_CMP_STRIDE
    a = b = None
    for l in range(half):
        x = tok_ref[0, pl.ds(l, ncp, stride=half), :]
        da = jnp.dot((x + pe_ref[0, l:l + 1, :]).astype(BF16), w1_ref[0, l], preferred_element_type=F32)
        db = jnp.dot((x + pe_ref[0, half + l:half + l + 1, :]).astype(BF16), w1_ref[0, half + l],
                     preferred_element_type=F32)
        a = da if a is None else a + da
        b = db if b is None else b + db
    h = jax.nn.gelu(a + pltpu.roll(b, ncp - 1, axis=0))
    y = jnp.dot(h.astype(BF16), w2_ref[0], preferred_element_type=F32)
    is_key = pl.program_id(0) == 0
    c2 = jnp.where(is_key, c2_ref[0], 1.0)
    s2 = jnp.where(is_key, s2_ref[0], 0.0)
    o_ref[0, 0, 0] = _rotate_half(y, c2, s2).astype(o_ref.dtype)


def compress_tokens(tok, pe, w1, w2, c2, s2):
    B, S, _ = tok.shape
    G, dk = NSA_KV_GROUPS, NSA_HEAD_DIM
    ncp = S // NSA_CMP_STRIDE
    hidden = w2.shape[1]
    return pl.pallas_call(
        functools.partial(_compress_kernel, ncp=ncp),
        grid=(2, B, G),
        in_specs=[
            pl.BlockSpec((1, S, dk), lambda s, b, g: (b, 0, s * G + g)),
            pl.BlockSpec((1, NSA_CMP_LEN, dk), lambda s, b, g: (s, 0, 0)),
            pl.BlockSpec((1, NSA_CMP_LEN, dk, hidden), lambda s, b, g: (s, 0, 0, 0)),
            pl.BlockSpec((1, hidden, dk), lambda s, b, g: (s, 0, 0)),
            pl.BlockSpec((1, ncp, dk), lambda s, b, g: (b, 0, 0)),
            pl.BlockSpec((1, ncp, dk), lambda s, b, g: (b, 0, 0)),
        ],
        out_specs=pl.BlockSpec((1, 1, 1, ncp, dk), lambda s, b, g: (s, b, g, 0, 0)),
        out_shape=jax.ShapeDtypeStruct((2, B, G, ncp, dk), BF16),
        compiler_params=_params("parallel", "parallel", "parallel"),
        name="compress_tokens",
    )(tok, pe, w1, w2, c2, s2)


def _cmp_select_kernel(q_ref, kc_ref, vc_ref, ovt_ref, gate_ref, *rest, R, tq, n_sub, tile0, n_cmp, n_top):
    oc_ref, nm_ref = rest[-2:]
    for u in range(n_sub):
        rows = slice(u * tq, (u + 1) * tq)
        t0 = (tile0 + pl.program_id(2) * n_sub + u) * tq
        _cmp_select_tile(q_ref.at[0, rows], kc_ref, vc_ref, ovt_ref, gate_ref.at[0, rows], oc_ref.at[0, rows],
                         nm_ref.at[0, rows], t0, R=R, tq=tq, n_cmp=n_cmp, n_top=n_top)


def _cmp_select_tile(q_ref, kc_ref, vc_ref, ovt_ref, gate_ref, oc_ref, nm_ref, t0, *, R, tq, n_cmp, n_top):
    M = R * tq
    q = jnp.concatenate([q_ref[:, r * LANE:(r + 1) * LANE] for r in range(R)], axis=0)
    kc = kc_ref[0, 0, 0]
    ncp = kc.shape[0]
    s = lax.dot_general(q, kc, (((1,), (1,)), ((), ())), preferred_element_type=F32)
    t = t0 + _mod_pow2(lax.broadcasted_iota(jnp.int32, (M, 1), 0), tq)
    lim = jnp.minimum(_div_pow2(t - (NSA_CMP_LEN - 1), NSA_CMP_STRIDE) + 1, n_cmp)
    valid = lax.broadcasted_iota(jnp.int32, (M, ncp), 1) < lim
    sm = jnp.where(valid, s, NEG_INF)
    e = jnp.exp2((sm - jnp.max(sm, axis=1, keepdims=True)).astype(BF16))
    ones_col = jnp.where(lax.broadcasted_iota(jnp.int32, (ncp, LANE), 1) == 0, 1.0, 0.0).astype(BF16)
    oc = jnp.dot(e, jnp.concatenate([vc_ref[0, 0, 0], ones_col], axis=1), preferred_element_type=F32)
    oc = oc[:, 0:LANE] * jnp.where(lim > 0, 1.0 / oc[:, LANE:LANE + 1], 0.0)
    for r in range(R):
        gate = gate_ref[:, NSA_N_BRANCHES * r:NSA_N_BRANCHES * r + 1]
        oc_ref[:, r * LANE:(r + 1) * LANE] = (oc[r * tq:(r + 1) * tq] * gate).astype(oc_ref.dtype)

    nb = ovt_ref.shape[0] - BF16_ROWS
    imp_un = lax.dot_general(ovt_ref[...], e, (((1,), (1,)), ((), ())), preferred_element_type=F32)
    t_l = t0 + _mod_pow2(lax.broadcasted_iota(jnp.int32, (1, M), 1), tq)
    lim_l = jnp.minimum(_div_pow2(t_l - (NSA_CMP_LEN - 1), NSA_CMP_STRIDE) + 1, n_cmp)
    imp_h = imp_un[0:nb] * jnp.where(lim_l > 0, 1.0 / imp_un[nb:nb + 1], 0.0)
    imp = imp_h[:, 0:tq]
    for r in range(1, R):
        imp = imp + imp_h[:, r * tq:(r + 1) * tq]
    j = lax.broadcasted_iota(jnp.int32, (nb, tq), 0)
    jf = j.astype(F32)
    cur = _div_pow2(t0 + lax.broadcasted_iota(jnp.int32, (nb, tq), 1), NSA_SEL_BLOCK)
    forced = jnp.logical_or(j == 0, jnp.logical_or(j == cur, j == cur - 1))
    x = jnp.where(j > cur, NEG_INF, imp + FORCE_BONUS * forced.astype(F32))
    for _ in range(n_top):
        mx = jnp.max(x, axis=0, keepdims=True)
        first = jnp.min(jnp.where(x == mx, jf, float(nb)), axis=0, keepdims=True)
        x = jnp.where(jf == first, jnp.where(mx > 0.5 * NEG_INF, -jnp.inf, NEG_INF), x)
    nm_ref[:, 0:nb] = jnp.where(x == -jnp.inf, 0.0, NEG_INF).T.astype(nm_ref.dtype)
    if nb < nm_ref.shape[1]:
        nm_ref[:, nb:] = jnp.full((tq, nm_ref.shape[1] - nb), NEG_INF, nm_ref.dtype)


def cmp_attention_select(q, kv_cmp, overlap, gate, *, n_cmp, n_top, tq=128, n_sub=2):
    B, S, _ = q.shape
    G, R = NSA_KV_GROUPS, NSA_R
    NB, NCP = overlap.shape
    tq = _pick(S, tq)
    tb = n_sub * tq
    n_parts = next(n for n in (4, 2, 1) if NCP % (n * LANE) == 0 and S % (n * tb) == 0)
    steps = S // (n_parts * tb)
    outs = None
    for part in range(n_parts):
        ncp_w = NCP * (part + 1) // n_parts
        nb_w = min(NB, -(-(S * (part + 1) // n_parts // NSA_SEL_BLOCK) // LANE) * LANE)
        ov = jnp.concatenate([overlap[:nb_w, :ncp_w], jnp.ones((1, ncp_w), BF16),
                              jnp.zeros((BF16_ROWS - 1, ncp_w), BF16)], axis=0)
        kern = functools.partial(_cmp_select_kernel, R=R, tq=tq, n_sub=n_sub, tile0=part * steps * n_sub,
                                 n_cmp=n_cmp, n_top=n_top)
        off = part * steps
        carried = [] if outs is None else list(outs)
        outs = pl.pallas_call(
            kern,
            grid=(B, G, steps),
            in_specs=[
                pl.BlockSpec((1, tb, R * LANE), lambda b, g, i, off=off: (b, i + off, g)),
                pl.BlockSpec((1, 1, 1, ncp_w, LANE), lambda b, g, i: (0, b, g, 0, 0)),
                pl.BlockSpec((1, 1, 1, ncp_w, LANE), lambda b, g, i: (1, b, g, 0, 0)),
                pl.BlockSpec((nb_w + BF16_ROWS, ncp_w), lambda b, g, i: (0, 0)),
                pl.BlockSpec((1, tb, LANE), lambda b, g, i, off=off: (b, i + off, g)),
            ] + [pl.BlockSpec(memory_space=pl.ANY)] * len(carried),
            out_specs=[
                pl.BlockSpec((1, tb, R * LANE), lambda b, g, i, off=off: (b, i + off, g)),
                pl.BlockSpec((1, tb, NB), lambda b, g, i, off=off: (b, i + off, g)),
            ],
            out_shape=[
                jax.ShapeDtypeStruct((B, S, G * R * LANE), BF16),
                jax.ShapeDtypeStruct((B, S, G * NB), BF16),
            ],
            input_output_aliases={5 + k: k for k in range(len(carried))},
            compiler_params=_params("parallel", "parallel", "arbitrary"),
            name="cmp_attention_select",
        )(q, kv_cmp, kv_cmp, ov, gate, *carried)
    return outs


def _rope_tables(pos, dim):
    inv = 1.0 / (ROPE_THETA ** (jnp.arange(0, dim, 2, dtype=F32) / dim))
    ang = pos.astype(F32)[..., None] * inv
    pad = [(0, 0)] * (ang.ndim - 1) + [(0, (LANE - dim) // 2)]
    c, s = jnp.pad(jnp.cos(ang), pad), jnp.pad(jnp.sin(ang), pad)
    return jnp.concatenate([c, c], axis=-1), jnp.concatenate([-s, s], axis=-1)


def _spread_rope_cols(w):
    dim = w.shape[-1]
    pad = [(0, 0)] * (w.ndim - 1) + [(0, (LANE - dim) // 2)]
    return jnp.concatenate([jnp.pad(w[..., :dim // 2], pad), jnp.pad(w[..., dim // 2:], pad)], axis=-1)


def nsa_mixer(x, B, S, positions, norm_w, w_in, cmp_pe, cmp_w1, cmp_w2, w_out):
    T, D = x.shape
    H, G, R, dk = NSA_HEADS, NSA_KV_GROUPS, NSA_R, NSA_HEAD_DIM
    q_cols, kv_cols = H * dk, NSA_N_BRANCHES * 2 * G * dk
    grp = G * dk
    q_scale = dk ** -0.5 * LOG2E
    tables = _rope_tables(positions.reshape(T), dk)

    w_qkv = w_in[:, :q_cols + kv_cols].astype(BF16)
    nq = q_cols // grp
    qv = proj(x, norm_w, w_qkv, n_blocks=nq + 2, tn=grp, col_map=lambda j: jnp.where(j < nq, j, 2 * j - 1),
              rope=tables, rope_mask=(1 << nq) - 1, scale_mask=(1 << nq) - 1, scale=q_scale,
              out_dtype=BF16).reshape(B, S, q_cols + 2 * grp)
    k_t = proj(x, norm_w, w_qkv, n_blocks=2, tn=grp, col_map=lambda j: nq + 2 + 2 * j, rope=tables, rope_mask=0b11,
               transposed=True, out_dtype=BF16)
    cmp_tok = proj(x, norm_w, w_qkv, n_blocks=1, tn=2 * grp, col_map=lambda j: q_cols // (2 * grp)).reshape(B, S, 2 * grp)
    w_gate = w_in[:, q_cols + kv_cols:].reshape(D, G, R * NSA_N_BRANCHES)
    w_gate = jnp.pad(w_gate, ((0, 0), (0, 0), (0, LANE - R * NSA_N_BRANCHES))).reshape(D, G * LANE).astype(BF16)
    gates = proj(x, norm_w, w_gate, n_blocks=1, tn=G * LANE, sigmoid=True).reshape(B, S, G * LANE)

    ncp = S // NSA_CMP_STRIDE
    n_cmp = (S - NSA_CMP_LEN) // NSA_CMP_STRIDE + 1
    cmp_end = jnp.minimum(jnp.arange(ncp) * NSA_CMP_STRIDE + NSA_CMP_LEN - 1, S - 1)
    c2c, s2c = _rope_tables(positions[:, cmp_end], dk)
    hidden = cmp_w1.shape[-1]
    kv_cmp = compress_tokens(cmp_tok, cmp_pe, cmp_w1.reshape(2, NSA_CMP_LEN, dk, hidden).astype(BF16),
                             cmp_w2.astype(BF16), c2c, s2c)

    n_sel = S // NSA_SEL_BLOCK
    nb = -(-n_sel // LANE) * LANE
    blk_start = jnp.arange(ncp) * NSA_CMP_STRIDE
    sel_start = jnp.arange(nb) * NSA_SEL_BLOCK
    overlap = ((blk_start[None, :] <= sel_start[:, None] + NSA_SEL_BLOCK - 1)
               & (blk_start[None, :] + NSA_CMP_LEN - 1 >= sel_start[:, None])
               & (jnp.arange(ncp)[None, :] < n_cmp) & (jnp.arange(nb)[:, None] < n_sel)).astype(BF16)
    o_cmp, neg_mask = cmp_attention_select(qv, kv_cmp, overlap, gates, n_cmp=n_cmp, n_top=min(NSA_N_SELECT, n_sel))

    blk_onehot = (jnp.arange(LANE)[:, None] == jnp.arange(S)[None, :] // NSA_SEL_BLOCK % LANE).astype(BF16)
    v0 = q_cols // dk
    o_slc = flash_attention(qv, neg_mask, k_t, blk_onehot, qv, gates, n_groups=G, R=R, E=nb // LANE,
                            qm_col=lambda g: g, qx_col=lambda g: g, km_col=lambda g: g,
                            v_col=lambda g: v0 + g, kx_batched=False,
                            gate_cols=tuple(NSA_N_BRANCHES * r + 1 for r in range(R)), mode="causal",
                            tq=512, tk=512, depth=4, keys_per_version=LANE * NSA_SEL_BLOCK)
    o_win = flash_attention(qv, None, k_t, None, qv, gates, n_groups=G, R=R, E=0,
                            qm_col=lambda g: g, km_col=lambda g: G + g, v_col=lambda g: v0 + G + g,
                            kx_batched=False, gate_cols=tuple(NSA_N_BRANCHES * r + 2 for r in range(R)),
                            mode="window", window=NSA_WINDOW, tq=512, tk=512)
    return matmul_residual([o.reshape(T, H * dk) for o in (o_cmp, o_slc, o_win)], w_out.astype(BF16), x)


def mla_mixer(x, B, S, positions, norm_w, w_in, q_norm_w, kv_norm_w, w_uq, w_ukv, w_out):
    T, D = x.shape
    H = MLA_HEADS
    q_rank, kv_rank = q_norm_w.shape[0], kv_norm_w.shape[0]
    assert q_rank == kv_rank and q_rank % LANE == 0
    q_scale = MLA_QK_DIM ** -0.5 * LOG2E
    tables = _rope_tables(positions.reshape(T), MLA_ROPE_DIM)

    latent = proj(x, norm_w, w_in[:, :2 * q_rank].astype(BF16), n_blocks=1, tn=2 * q_rank)
    w_kr = _spread_rope_cols(w_in[:, 2 * q_rank:]).astype(BF16)
    k_rope_t = proj(x, norm_w, w_kr, n_blocks=1, tn=LANE, rope=tables, rope_mask=1, transposed=True, out_dtype=BF16)
    w_uq3 = w_uq.reshape(q_rank, H, MLA_QK_DIM)
    w_uq_p = jnp.concatenate([w_uq3[:, :, :MLA_NOPE_DIM].reshape(q_rank, H * MLA_NOPE_DIM),
                              _spread_rope_cols(w_uq3[:, :, MLA_NOPE_DIM:]).reshape(q_rank, H * LANE)],
                             axis=1).astype(BF16)
    tn = 4 * LANE
    n_nope = H * MLA_NOPE_DIM // tn
    n_all = n_nope + H * LANE // tn
    q = proj(latent, q_norm_w, w_uq_p, n_blocks=n_all, tn=tn, x_col_block=0, rope=tables,
             rope_mask=(1 << n_all) - (1 << n_nope), scale_mask=(1 << n_all) - 1, scale=q_scale,
             out_dtype=BF16).reshape(B, S, n_all * tn)
    w_ukv3 = w_ukv.reshape(kv_rank, H, MLA_NOPE_DIM + MLA_V_DIM)
    w_uk = w_ukv3[:, :, :MLA_NOPE_DIM].reshape(kv_rank, H * MLA_NOPE_DIM).astype(BF16)
    w_uv = w_ukv3[:, :, MLA_NOPE_DIM:].reshape(kv_rank, H * MLA_V_DIM).astype(BF16)
    k_nope_t = proj(latent, kv_norm_w, w_uk, n_blocks=H * MLA_NOPE_DIM // tn, tn=tn, x_col_block=1, transposed=True,
                    out_dtype=BF16)
    v = proj(latent, kv_norm_w, w_uv, n_blocks=H * MLA_V_DIM // tn, tn=tn, x_col_block=1,
             out_dtype=BF16).reshape(B, S, H * MLA_V_DIM)
    o = flash_attention(q, q, k_nope_t, k_rope_t, v, n_groups=H, R=1, E=1,
                        qm_col=lambda h: h, qx_col=lambda h: H + h, km_col=lambda h: h, v_col=lambda h: h,
                        kx_batched=True, mode="causal", tq=1024, tk=1024, depth=4)
    return matmul_residual([o.reshape(T, H * MLA_V_DIM)], w_out.astype(BF16), x)


def kernel(x, positions, ffn_norm_w, ffn_w_in, ffn_w_out, mix_norm_w, nsa_w_in, nsa_cmp_pe, nsa_cmp_w1, nsa_cmp_w2,
           nsa_w_out, mla_w_in, mla_q_norm_w, mla_kv_norm_w, mla_w_uq, mla_w_ukv, mla_w_out, final_norm_w):
    B, S, D = x.shape
    depth = ffn_norm_w.shape[0]
    n_mixers = 2
    h = x.reshape(B * S, D)
    for i in range(depth):
        h = ffn_half_step(h, ffn_norm_w[i, 0], ffn_w_in[i, 0].astype(BF16), ffn_w_out[i, 0].astype(BF16))
        j = i // n_mixers
        if i % n_mixers == 0:
            h = nsa_mixer(h, B, S, positions, mix_norm_w[i], nsa_w_in[j], nsa_cmp_pe[j], nsa_cmp_w1[j],
                          nsa_cmp_w2[j], nsa_w_out[j])
        else:
            h = mla_mixer(h, B, S, positions, mix_norm_w[i], mla_w_in[j], mla_q_norm_w[j], mla_kv_norm_w[j],
                          mla_w_uq[j], mla_w_ukv[j], mla_w_out[j])
        last = i == depth - 1
        h = ffn_half_step(h, ffn_norm_w[i, 1], ffn_w_in[i, 1].astype(BF16), ffn_w_out[i, 1].astype(BF16),
                          final_w=final_norm_w if last else None)
    return h.reshape(B, S, D)
```

```python
import functools
import math

import jax
import jax.numpy as jnp
from jax import lax
from jax.experimental import pallas as pl
from jax.experimental.pallas import tpu as pltpu

F32 = jnp.float32
BF16 = jnp.bfloat16

LANE = 128
BF16_ROWS = 16
RMS_EPS = 1e-6
ROPE_THETA = 10000.0
NEG_INF = -1e30
MASK_PASS = 3e38
FORCE_BONUS = 1e9
LOG2E = math.log2(math.e)
VMEM_LIMIT = 56 * 1024 * 1024
FFN_VMEM_LIMIT = 61 * 1024 * 1024
FFN_SLAB = 64

NSA_HEADS = 16
NSA_HEAD_DIM = 128
NSA_KV_GROUPS = 4
NSA_R = NSA_HEADS // NSA_KV_GROUPS
NSA_N_BRANCHES = 3
NSA_CMP_LEN = 32
NSA_CMP_STRIDE = 16
NSA_SEL_BLOCK = 64
NSA_N_SELECT = 16
NSA_WINDOW = 512

MLA_HEADS = 16
MLA_NOPE_DIM = 128
MLA_ROPE_DIM = 64
MLA_V_DIM = 128
MLA_QK_DIM = MLA_NOPE_DIM + MLA_ROPE_DIM


def _pick(n, pref):
    if n <= pref:
        return n
    t = (pref // LANE) * LANE
    while t > LANE and n % t:
        t -= LANE
    assert n % t == 0, (n, pref)
    return t


def _div_pow2(x, d):
    assert d & (d - 1) == 0, d
    return jnp.right_shift(x, d.bit_length() - 1)


def _mod_pow2(x, d):
    assert d & (d - 1) == 0, d
    return jnp.bitwise_and(x, d - 1)


def _params(*sem, vmem=VMEM_LIMIT):
    return pltpu.CompilerParams(dimension_semantics=sem, vmem_limit_bytes=vmem)


def _repeat_lanes(x, n):
    return jnp.concatenate([x] * n, axis=1)


def _rotate_half(y, c2, s2):
    return y * c2 + pltpu.roll(y, LANE // 2, axis=1) * s2


def _rms_rows(x, w):
    ms = jnp.mean(x * x, axis=-1, keepdims=True)
    return x * lax.rsqrt(ms + RMS_EPS) * w


def _proj_kernel(*refs, has_rope, rope_mask, scale_mask, scale, sigmoid, transposed):
    if has_rope:
        x_ref, nw_ref, w_ref, c2_ref, s2_ref, o_ref, xn_scr = refs
    else:
        x_ref, nw_ref, w_ref, o_ref, xn_scr = refs
    j = pl.program_id(1)

    @pl.when(j == 0)
    def _():
        xn_scr[...] = _rms_rows(x_ref[...], nw_ref[...]).astype(BF16)

    y = jnp.dot(xn_scr[...], w_ref[...], preferred_element_type=F32)
    if sigmoid:
        y = jax.nn.sigmoid(y)
    if has_rope:
        rope_on = jnp.bitwise_and(lax.shift_right_logical(jnp.int32(rope_mask), j), 1) == 1
        factor = jnp.where(jnp.bitwise_and(lax.shift_right_logical(jnp.int32(scale_mask), j), 1) == 1, scale, 1.0)
        c2 = jnp.where(rope_on, c2_ref[...], 1.0) * factor
        s2 = jnp.where(rope_on, s2_ref[...], 0.0) * factor
    for h in range(y.shape[1] // LANE):
        slab = y[:, h * LANE:(h + 1) * LANE]
        if has_rope:
            slab = _rotate_half(slab, c2, s2)
        if transposed:
            o_ref[h * LANE:(h + 1) * LANE, :] = slab.T.astype(o_ref.dtype)
        else:
            o_ref[:, h * LANE:(h + 1) * LANE] = slab.astype(o_ref.dtype)


def proj(x, norm_w, w, *, n_blocks, tn, col_map=None, x_col_block=0, rope=None, rope_mask=0, scale_mask=0, scale=1.0,
         sigmoid=False, transposed=False, out_dtype=F32, tm=1024):
    T = x.shape[0]
    K = w.shape[0]
    tm = _pick(T, tm)
    col_map = col_map or (lambda j: j)
    has_rope = rope is not None
    assert has_rope or scale_mask == 0
    in_specs = [
        pl.BlockSpec((tm, K), lambda i, j: (i, x_col_block)),
        pl.BlockSpec((1, K), lambda i, j: (0, 0)),
        pl.BlockSpec((K, tn), lambda i, j: (0, col_map(j))),
    ]
    args = [x, norm_w.reshape(1, K).astype(F32), w]
    if has_rope:
        in_specs += [pl.BlockSpec((tm, LANE), lambda i, j: (i, 0))] * 2
        args += list(rope)
    kern = functools.partial(_proj_kernel, has_rope=has_rope, rope_mask=rope_mask, scale_mask=scale_mask,
                             scale=scale, sigmoid=sigmoid, transposed=transposed)
    if transposed:
        out_spec = pl.BlockSpec((tn, tm), lambda i, j: (j, i))
        out_shape = jax.ShapeDtypeStruct((n_blocks * tn, T), out_dtype)
    else:
        out_spec = pl.BlockSpec((tm, tn), lambda i, j: (i, j))
        out_shape = jax.ShapeDtypeStruct((T, n_blocks * tn), out_dtype)
    return pl.pallas_call(
        kern,
        grid=(T // tm, n_blocks),
        in_specs=in_specs,
        out_specs=out_spec,
        out_shape=out_shape,
        scratch_shapes=[pltpu.VMEM((tm, K), BF16)],
        compiler_params=_params("parallel", "arbitrary"),
        name="proj",
    )(*args)


def _matmul_residual_kernel(*refs):
    *a_refs, w_ref, r_ref, o_ref = refs
    a = a_refs[0][...]
    if len(a_refs) > 1:
        a = a.astype(F32)
        for a_ref in a_refs[1:]:
            a = a + a_ref[...].astype(F32)
        a = a.astype(BF16)
    o_ref[...] = r_ref[...] + jnp.dot(a, w_ref[...], preferred_element_type=F32)


def matmul_residual(a_list, w, res, *, tm=1024, tn=1024):
    T, K = a_list[0].shape
    N = w.shape[1]
    tm, tn = _pick(T, tm), _pick(N, tn)
    return pl.pallas_call(
        _matmul_residual_kernel,
        grid=(T // tm, N // tn),
        in_specs=[pl.BlockSpec((tm, K), lambda i, j: (i, 0))] * len(a_list) + [
            pl.BlockSpec((K, tn), lambda i, j: (0, j)),
            pl.BlockSpec((tm, tn), lambda i, j: (i, j)),
        ],
        out_specs=pl.BlockSpec((tm, tn), lambda i, j: (i, j)),
        out_shape=jax.ShapeDtypeStruct((T, N), F32),
        compiler_params=_params("parallel", "arbitrary"),
        name="matmul_residual",
    )(*a_list, w, res)


def _ffn_kernel(x_ref, nw_ref, wg_ref, wu_ref, wo_ref, fw_ref, o_ref, xn_scr, *, final_norm):
    f = pl.program_id(1)
    tm = x_ref.shape[0]
    slab = min(FFN_SLAB, tm)

    def over_slabs(fn):
        def body(r, carry):
            fn(pl.ds(pl.multiple_of(r * slab, slab), slab))
            return carry

        lax.fori_loop(0, tm // slab, body, 0)

    @pl.when(f == 0)
    def _():
        def prologue(rows):
            xn_scr[rows, :] = _rms_rows(x_ref[rows, :], nw_ref[...]).astype(BF16)

        over_slabs(prologue)
        o_ref[...] = jnp.zeros_like(o_ref)

    xn = xn_scr[...]
    g = jnp.dot(xn, wg_ref[...], preferred_element_type=F32)
    u = jnp.dot(xn, wu_ref[...], preferred_element_type=F32)
    h = (g * jax.nn.sigmoid(g) * u).astype(BF16)
    o_ref[...] += jnp.dot(h, wo_ref[...], preferred_element_type=F32)

    @pl.when(f == pl.num_programs(1) - 1)
    def _():
        def epilogue(rows):
            y = x_ref[rows, :] + 0.5 * o_ref[rows, :]
            if final_norm:
                y = _rms_rows(y, fw_ref[...])
            o_ref[rows, :] = y

        over_slabs(epilogue)


def ffn_half_step(x, norm_w, w_in, w_out, final_w=None, *, tm=1024, tf=512):
    T, D = x.shape
    DFF = w_out.shape[0]
    tm, tf = _pick(T, tm), _pick(DFF, tf)
    nf = DFF // tf
    final_norm = final_w is not None
    fw = (final_w if final_norm else norm_w).reshape(1, D).astype(F32)
    return pl.pallas_call(
        functools.partial(_ffn_kernel, final_norm=final_norm),
        grid=(T // tm, nf),
        in_specs=[
            pl.BlockSpec((tm, D), lambda i, f: (i, 0)),
            pl.BlockSpec((1, D), lambda i, f: (0, 0)),
            pl.BlockSpec((D, tf), lambda i, f: (0, f)),
            pl.BlockSpec((D, tf), lambda i, f: (0, f + nf)),
            pl.BlockSpec((tf, D), lambda i, f: (f, 0)),
            pl.BlockSpec((1, D), lambda i, f: (0, 0)),
        ],
        out_specs=pl.BlockSpec((tm, D), lambda i, f: (i, 0)),
        out_shape=jax.ShapeDtypeStruct((T, D), F32),
        scratch_shapes=[pltpu.VMEM((tm, D), BF16)],
        compiler_params=_params("parallel", "arbitrary", vmem=FFN_VMEM_LIMIT),
        name="ffn_half_step",
    )(x, norm_w.reshape(1, D).astype(F32), w_in, w_in, w_out, fw)


def _flash_kernel(*refs, R, E, has_kx, gate_cols, tq, tk, mode, chunks_per_version, depth):
    it = iter(refs)
    qm_ref = next(it)
    qx_ref = next(it) if E else None
    km_ref = next(it)
    kx_ref = next(it) if has_kx else None
    v_ref = next(it)
    gate_ref = next(it) if gate_cols else None
    cap_ref = next(it)
    o_ref = next(it)
    q_scr, m_scr, acc_scr, s_scr = it
    M = R * tq
    i = pl.program_id(2)

    for e in range(max(E, 1)):
        for r in range(R):
            q_scr[e, r * tq:(r + 1) * tq, 0:LANE] = qm_ref[0, :, r * LANE:(r + 1) * LANE]
            if E:
                q_scr[e, r * tq:(r + 1) * tq, LANE:2 * LANE] = qx_ref[0, :, e * LANE:(e + 1) * LANE]
    m_scr[...] = jnp.full_like(m_scr, NEG_INF)
    acc_scr[...] = jnp.zeros_like(acc_scr)
    ones_col = jnp.where(lax.broadcasted_iota(jnp.int32, (tk, LANE), 1) == 0, 1.0, 0.0).astype(BF16)

    def scores(c):
        k0 = pl.multiple_of(c * tk, tk)
        kt = km_ref[:, pl.ds(k0, tk)]
        if has_kx:
            kt = jnp.concatenate([kt, kx_ref[:, pl.ds(k0, tk)]], axis=0)
        q = q_scr[c // chunks_per_version] if E > 1 else q_scr[0]
        return jnp.dot(q, kt, preferred_element_type=F32).astype(BF16)

    def softmax_pv(c, s, cap=None):
        k0 = pl.multiple_of(c * tk, tk)
        if cap is not None:
            s = jnp.minimum(s, cap)
        m_prev = m_scr[...]
        m_new = jnp.maximum(m_prev, jnp.max(s, axis=1, keepdims=True).astype(F32))
        alpha = jnp.exp2(m_prev - m_new)
        p = jnp.exp2(s - _repeat_lanes(m_new.astype(BF16), tk // LANE))
        v1 = jnp.concatenate([v_ref[0, pl.ds(k0, tk), :], ones_col], axis=1)
        acc_scr[...] = _repeat_lanes(alpha, 2) * acc_scr[...] + jnp.dot(p, v1, preferred_element_type=F32)
        m_scr[...] = m_new

    if mode == "causal":
        n_full = (i * tq) // tk
        bufs = [s_scr.at[u] for u in range(depth)]
        bufs[0][...] = scores(0)

        def group(j, carry):
            for u in range(depth):
                bufs[(u + 1) % depth][...] = scores(depth * j + u + 1)
                softmax_pv(depth * j + u, bufs[u][...])
            return carry

        n_groups = n_full // depth
        lax.fori_loop(0, n_groups, group, 0)

        def tail(c, carry):
            softmax_pv(c, bufs[0][...])
            bufs[0][...] = scores(c + 1)
            return carry

        lax.fori_loop(depth * n_groups, n_full, tail, 0)
        softmax_pv(n_full, bufs[0][...], cap_ref[0])
    else:
        prev = jnp.maximum(i - 1, 0)
        s_prev = scores(prev)
        s_diag = scores(i)
        softmax_pv(prev, s_prev, jnp.where(i > 0, cap_ref[1], NEG_INF))
        softmax_pv(i, s_diag, cap_ref[0])

    out = acc_scr[:, 0:LANE] / acc_scr[:, LANE:LANE + 1]
    for r in range(R):
        o_r = out[r * tq:(r + 1) * tq]
        if gate_cols:
            o_r = o_r * gate_ref[0, :, gate_cols[r]:gate_cols[r] + 1]
        o_ref[0, :, r * LANE:(r + 1) * LANE] = o_r.astype(o_ref.dtype)


def flash_attention(qm, qx, km, kx, v, gate=None, *, n_groups, R, E, qm_col, qx_col=None, km_col, v_col, kx_batched,
                    gate_cols=None, mode, window=0, tq, tk, depth=2, keys_per_version=None):
    B, S, _ = qm.shape
    tq, tk = _pick(S, tq), _pick(S, tk)
    assert tq == tk
    assert mode == "causal" or tq == window
    has_kx = kx is not None
    dk = 2 * LANE if has_kx else LANE
    cpv = (keys_per_version // tk) if keys_per_version else 1
    M = R * tq
    in_specs = [pl.BlockSpec((1, tq, R * LANE), lambda b, g, i: (b, i, qm_col(g)))]
    args = [qm]
    if E:
        in_specs.append(pl.BlockSpec((1, tq, E * LANE), lambda b, g, i: (b, i, qx_col(g))))
        args.append(qx)
    in_specs.append(pl.BlockSpec((LANE, S), lambda b, g, i: (km_col(g), b)))
    args.append(km)
    if has_kx:
        in_specs.append(pl.BlockSpec((LANE, S), lambda b, g, i: (0, b if kx_batched else 0)))
        args.append(kx)
    in_specs.append(pl.BlockSpec((1, S, LANE), lambda b, g, i: (b, 0, v_col(g))))
    args.append(v)
    if gate_cols:
        in_specs.append(pl.BlockSpec((1, tq, LANE), lambda b, g, i: (b, i, g)))
        args.append(gate)
    q_off = lax.broadcasted_iota(jnp.int32, (M, tk), 0) % tq
    k_off = lax.broadcasted_iota(jnp.int32, (M, tk), 1)
    caps = [k_off <= q_off] + ([k_off > q_off] if mode == "window" else [])
    cap = jnp.stack([jnp.where(c, MASK_PASS, NEG_INF).astype(BF16) for c in caps])
    in_specs.append(pl.BlockSpec(cap.shape, lambda b, g, i: (0, 0, 0)))
    args.append(cap)
    kern = functools.partial(_flash_kernel, R=R, E=E, has_kx=has_kx, gate_cols=gate_cols, tq=tq, tk=tk, mode=mode,
                             chunks_per_version=cpv, depth=depth)
    return pl.pallas_call(
        kern,
        grid=(B, n_groups, S // tq),
        in_specs=in_specs,
        out_specs=pl.BlockSpec((1, tq, R * LANE), lambda b, g, i: (b, i, g)),
        out_shape=jax.ShapeDtypeStruct((B, S, n_groups * R * LANE), BF16),
        scratch_shapes=[
            pltpu.VMEM((max(E, 1), M, dk), BF16),
            pltpu.VMEM((M, LANE), F32),
            pltpu.VMEM((M, 2 * LANE), F32),
            pltpu.VMEM((depth, M, tk), BF16),
        ],
        compiler_params=_params("parallel", "parallel", "arbitrary"),
        name="flash_" + mode + ("_x%d" % E if E else ""),
    )(*args)


def _compress_kernel(tok_ref, pe_ref, w1_ref, w2_ref, c2_ref, s2_ref, o_ref, *, ncp):
    half = NSA_CMP_STRIDE
    a = b = None
    for l in range(half):
        x = tok_ref[0, pl.ds(l, ncp, stride=half), :]
        da = jnp.dot((x + pe_ref[0, l:l + 1, :]).astype(BF16), w1_ref[0, l], preferred_element_type=F32)
        db = jnp.dot((x + pe_ref[0, half + l:half + l + 1, :]).astype(BF16), w1_ref[0, half + l],
                     preferred_element_type=F32)
        a = da if a is None else a + da
        b = db if b is None else b + db
    h = jax.nn.gelu(a + pltpu.roll(b, ncp - 1, axis=0))
    y = jnp.dot(h.astype(BF16), w2_ref[0], preferred_element_type=F32)
    is_key = pl.program_id(0) == 0
    c2 = jnp.where(is_key, c2_ref[0], 1.0)
    s2 = jnp.where(is_key, s2_ref[0], 0.0)
    o_ref[0, 0, 0] = _rotate_half(y, c2, s2).astype(o_ref.dtype)


def compress_tokens(tok, pe, w1, w2, c2, s2):
    B, S, _ = tok.shape
    G, dk = NSA_KV_GROUPS, NSA_HEAD_DIM
    ncp = S // NSA_CMP_STRIDE
    hidden = w2.shape[1]
    return pl.pallas_call(
        functools.partial(_compress_kernel, ncp=ncp),
        grid=(2, B, G),
        in_specs=[
            pl.BlockSpec((1, S, dk), lambda s, b, g: (b, 0, s * G + g)),
            pl.BlockSpec((1, NSA_CMP_LEN, dk), lambda s, b, g: (s, 0, 0)),
            pl.BlockSpec((1, NSA_CMP_LEN, dk, hidden), lambda s, b, g: (s, 0, 0, 0)),
            pl.BlockSpec((1, hidden, dk), lambda s, b, g: (s, 0, 0)),
            pl.BlockSpec((1, ncp, dk), lambda s, b, g: (b, 0, 0)),
            pl.BlockSpec((1, ncp, dk), lambda s, b, g: (b, 0, 0)),
        ],
        out_specs=pl.BlockSpec((1, 1, 1, ncp, dk), lambda s, b, g: (s, b, g, 0, 0)),
        out_shape=jax.ShapeDtypeStruct((2, B, G, ncp, dk), BF16),
        compiler_params=_params("parallel", "parallel", "parallel"),
        name="compress_tokens",
    )(tok, pe, w1, w2, c2, s2)


def _cmp_select_kernel(q_ref, kc_ref, vc_ref, ovt_ref, gate_ref, oc_all_ref, nm_all_ref, oc_ref, nm_ref, *,
                       R, tq, n_sub, tile0, n_cmp, n_top):
    del oc_all_ref, nm_all_ref
    for u in range(n_sub):
        rows = slice(u * tq, (u + 1) * tq)
        t0 = (tile0 + pl.program_id(2) * n_sub + u) * tq
        _cmp_select_tile(q_ref.at[0, rows], kc_ref, vc_ref, ovt_ref, gate_ref.at[0, rows], oc_ref.at[0, rows],
                         nm_ref.at[0, rows], t0, R=R, tq=tq, n_cmp=n_cmp, n_top=n_top)


def _cmp_select_tile(q_ref, kc_ref, vc_ref, ovt_ref, gate_ref, oc_ref, nm_ref, t0, *, R, tq, n_cmp, n_top):
    M = R * tq
    q = jnp.concatenate([q_ref[:, r * LANE:(r + 1) * LANE] for r in range(R)], axis=0)
    kc = kc_ref[0, 0, 0]
    ncp = kc.shape[0]
    s = lax.dot_general(q, kc, (((1,), (1,)), ((), ())), preferred_element_type=F32)
    t = t0 + _mod_pow2(lax.broadcasted_iota(jnp.int32, (M, 1), 0), tq)
    lim = jnp.minimum(_div_pow2(t - (NSA_CMP_LEN - 1), NSA_CMP_STRIDE) + 1, n_cmp)
    valid = lax.broadcasted_iota(jnp.int32, (M, ncp), 1) < lim
    sm = jnp.where(valid, s, NEG_INF)
    e = jnp.exp2((sm - jnp.max(sm, axis=1, keepdims=True)).astype(BF16))
    ones_col = jnp.where(lax.broadcasted_iota(jnp.int32, (ncp, LANE), 1) == 0, 1.0, 0.0).astype(BF16)
    oc = jnp.dot(e, jnp.concatenate([vc_ref[0, 0, 0], ones_col], axis=1), preferred_element_type=F32)
    oc = oc[:, 0:LANE] * jnp.where(lim > 0, 1.0 / oc[:, LANE:LANE + 1], 0.0)
    for r in range(R):
        gate = gate_ref[:, NSA_N_BRANCHES * r:NSA_N_BRANCHES * r + 1]
        oc_ref[:, r * LANE:(r + 1) * LANE] = (oc[r * tq:(r + 1) * tq] * gate).astype(oc_ref.dtype)

    nb = ovt_ref.shape[0] - BF16_ROWS
    imp_un = lax.dot_general(ovt_ref[...], e, (((1,), (1,)), ((), ())), preferred_element_type=F32)
    t_l = t0 + _mod_pow2(lax.broadcasted_iota(jnp.int32, (1, M), 1), tq)
    lim_l = jnp.minimum(_div_pow2(t_l - (NSA_CMP_LEN - 1), NSA_CMP_STRIDE) + 1, n_cmp)
    imp_h = imp_un[0:nb] * jnp.where(lim_l > 0, 1.0 / imp_un[nb:nb + 1], 0.0)
    imp = imp_h[:, 0:tq]
    for r in range(1, R):
        imp = imp + imp_h[:, r * tq:(r + 1) * tq]
    j = lax.broadcasted_iota(jnp.int32, (nb, tq), 0)
    jf = j.astype(F32)
    cur = _div_pow2(t0 + lax.broadcasted_iota(jnp.int32, (nb, tq), 1), NSA_SEL_BLOCK)
    forced = jnp.logical_or(j == 0, jnp.logical_or(j == cur, j == cur - 1))
    x = jnp.where(j > cur, NEG_INF, imp + FORCE_BONUS * forced.astype(F32))
    for _ in range(n_top):
        mx = jnp.max(x, axis=0, keepdims=True)
        first = jnp.min(jnp.where(x == mx, jf, float(nb)), axis=0, keepdims=True)
        x = jnp.where(jf == first, jnp.where(mx > 0.5 * NEG_INF, -jnp.inf, NEG_INF), x)
    nm_ref[:, 0:nb] = jnp.where(x == -jnp.inf, 0.0, NEG_INF).T.astype(nm_ref.dtype)
    if nb < nm_ref.shape[1]:
        nm_ref[:, nb:] = jnp.full((tq, nm_ref.shape[1] - nb), NEG_INF, nm_ref.dtype)


def cmp_attention_select(q, kv_cmp, overlap, gate, *, n_cmp, n_top, tq=128, n_sub=2):
    B, S, _ = q.shape
    G, R = NSA_KV_GROUPS, NSA_R
    NB, NCP = overlap.shape
    tq = _pick(S, tq)
    tb = n_sub * tq
    n_parts = next(n for n in (4, 2, 1) if NCP % (n * LANE) == 0 and S % (n * tb) == 0)
    steps = S // (n_parts * tb)
    outs = [jnp.zeros((B, S, G * R * LANE), BF16), jnp.zeros((B, S, G * NB), BF16)]
    for part in range(n_parts):
        ncp_w = NCP * (part + 1) // n_parts
        nb_w = min(NB, -(-(S * (part + 1) // n_parts // NSA_SEL_BLOCK) // LANE) * LANE)
        ov = jnp.concatenate([overlap[:nb_w, :ncp_w], jnp.ones((1, ncp_w), BF16),
                              jnp.zeros((BF16_ROWS - 1, ncp_w), BF16)], axis=0)
        kern = functools.partial(_cmp_select_kernel, R=R, tq=tq, n_sub=n_sub, tile0=part * steps * n_sub,
                                 n_cmp=n_cmp, n_top=n_top)
        off = part * steps
        carried = list(outs)
        outs = pl.pallas_call(
            kern,
            grid=(B, G, steps),
            in_specs=[
                pl.BlockSpec((1, tb, R * LANE), lambda b, g, i, off=off: (b, i + off, g)),
                pl.BlockSpec((1, 1, 1, ncp_w, LANE), lambda b, g, i: (0, b, g, 0, 0)),
                pl.BlockSpec((1, 1, 1, ncp_w, LANE), lambda b, g, i: (1, b, g, 0, 0)),
                pl.BlockSpec((nb_w + BF16_ROWS, ncp_w), lambda b, g, i: (0, 0)),
                pl.BlockSpec((1, tb, LANE), lambda b, g, i, off=off: (b, i + off, g)),
            ] + [pl.BlockSpec(memory_space=pl.ANY)] * len(carried),
            out_specs=[
                pl.BlockSpec((1, tb, R * LANE), lambda b, g, i, off=off: (b, i + off, g)),
                pl.BlockSpec((1, tb, NB), lambda b, g, i, off=off: (b, i + off, g)),
            ],
            out_shape=[
                jax.ShapeDtypeStruct((B, S, G * R * LANE), BF16),
                jax.ShapeDtypeStruct((B, S, G * NB), BF16),
            ],
            input_output_aliases={5 + k: k for k in range(len(carried))},
            compiler_params=_params("parallel", "parallel", "arbitrary"),
            name="cmp_attention_select",
        )(q, kv_cmp, kv_cmp, ov, gate, *carried)
    return outs


def _rope_tables(pos, dim):
    inv = 1.0 / (ROPE_THETA ** (jnp.arange(0, dim, 2, dtype=F32) / dim))
    ang = pos.astype(F32)[..., None] * inv
    pad = [(0, 0)] * (ang.ndim - 1) + [(0, (LANE - dim) // 2)]
    c, s = jnp.pad(jnp.cos(ang), pad), jnp.pad(jnp.sin(ang), pad)
    return jnp.concatenate([c, c], axis=-1), jnp.concatenate([-s, s], axis=-1)


def _spread_rope_cols(w):
    dim = w.shape[-1]
    pad = [(0, 0)] * (w.ndim - 1) + [(0, (LANE - dim) // 2)]
    return jnp.concatenate([jnp.pad(w[..., :dim // 2], pad), jnp.pad(w[..., dim // 2:], pad)], axis=-1)


def nsa_mixer(x, B, S, positions, norm_w, w_in, cmp_pe, cmp_w1, cmp_w2, w_out):
    T, D = x.shape
    H, G, R, dk = NSA_HEADS, NSA_KV_GROUPS, NSA_R, NSA_HEAD_DIM
    q_cols, kv_cols = H * dk, NSA_N_BRANCHES * 2 * G * dk
    grp = G * dk
    q_scale = dk ** -0.5 * LOG2E
    tables = _rope_tables(positions.reshape(T), dk)

    w_qkv = w_in[:, :q_cols + kv_cols].astype(BF16)
    nq = q_cols // grp
    qv = proj(x, norm_w, w_qkv, n_blocks=nq + 2, tn=grp, col_map=lambda j: jnp.where(j < nq, j, 2 * j - 1),
              rope=tables, rope_mask=(1 << nq) - 1, scale_mask=(1 << nq) - 1, scale=q_scale,
              out_dtype=BF16).reshape(B, S, q_cols + 2 * grp)
    k_t = proj(x, norm_w, w_qkv, n_blocks=2, tn=grp, col_map=lambda j: nq + 2 + 2 * j, rope=tables, rope_mask=0b11,
               transposed=True, out_dtype=BF16)
    cmp_tok = proj(x, norm_w, w_qkv, n_blocks=1, tn=2 * grp, col_map=lambda j: q_cols // (2 * grp)).reshape(B, S, 2 * grp)
    w_gate = w_in[:, q_cols + kv_cols:].reshape(D, G, R * NSA_N_BRANCHES)
    w_gate = jnp.pad(w_gate, ((0, 0), (0, 0), (0, LANE - R * NSA_N_BRANCHES))).reshape(D, G * LANE).astype(BF16)
    gates = proj(x, norm_w, w_gate, n_blocks=1, tn=G * LANE, sigmoid=True).reshape(B, S, G * LANE)

    ncp = S // NSA_CMP_STRIDE
    n_cmp = (S - NSA_CMP_LEN) // NSA_CMP_STRIDE + 1
    cmp_end = jnp.minimum(jnp.arange(ncp) * NSA_CMP_STRIDE + NSA_CMP_LEN - 1, S - 1)
    c2c, s2c = _rope_tables(positions[:, cmp_end], dk)
    hidden = cmp_w1.shape[-1]
    kv_cmp = compress_tokens(cmp_tok, cmp_pe, cmp_w1.reshape(2, NSA_CMP_LEN, dk, hidden).astype(BF16),
                             cmp_w2.astype(BF16), c2c, s2c)

    n_sel = S // NSA_SEL_BLOCK
    nb = -(-n_sel // LANE) * LANE
    blk_start = jnp.arange(ncp) * NSA_CMP_STRIDE
    sel_start = jnp.arange(nb) * NSA_SEL_BLOCK
    overlap = ((blk_start[None, :] <= sel_start[:, None] + NSA_SEL_BLOCK - 1)
               & (blk_start[None, :] + NSA_CMP_LEN - 1 >= sel_start[:, None])
               & (jnp.arange(ncp)[None, :] < n_cmp) & (jnp.arange(nb)[:, None] < n_sel)).astype(BF16)
    o_cmp, neg_mask = cmp_attention_select(qv, kv_cmp, overlap, gates, n_cmp=n_cmp, n_top=min(NSA_N_SELECT, n_sel))

    blk_onehot = (jnp.arange(LANE)[:, None] == jnp.arange(S)[None, :] // NSA_SEL_BLOCK % LANE).astype(BF16)
    v0 = q_cols // dk
    o_slc = flash_attention(qv, neg_mask, k_t, blk_onehot, qv, gates, n_groups=G, R=R, E=nb // LANE,
                            qm_col=lambda g: g, qx_col=lambda g: g, km_col=lambda g: g,
                            v_col=lambda g: v0 + g, kx_batched=False,
                            gate_cols=tuple(NSA_N_BRANCHES * r + 1 for r in range(R)), mode="causal",
                            tq=512, tk=512, depth=4, keys_per_version=LANE * NSA_SEL_BLOCK)
    o_win = flash_attention(qv, None, k_t, None, qv, gates, n_groups=G, R=R, E=0,
                            qm_col=lambda g: g, km_col=lambda g: G + g, v_col=lambda g: v0 + G + g,
                            kx_batched=False, gate_cols=tuple(NSA_N_BRANCHES * r + 2 for r in range(R)),
                            mode="window", window=NSA_WINDOW, tq=512, tk=512)
    return matmul_residual([o.reshape(T, H * dk) for o in (o_cmp, o_slc, o_win)], w_out.astype(BF16), x)


def mla_mixer(x, B, S, positions, norm_w, w_in, q_norm_w, kv_norm_w, w_uq, w_ukv, w_out):
    T, D = x.shape
    H = MLA_HEADS
    q_rank, kv_rank = q_norm_w.shape[0], kv_norm_w.shape[0]
    assert q_rank == kv_rank and q_rank % LANE == 0
    q_scale = MLA_QK_DIM ** -0.5 * LOG2E
    tables = _rope_tables(positions.reshape(T), MLA_ROPE_DIM)

    latent = proj(x, norm_w, w_in[:, :2 * q_rank].astype(BF16), n_blocks=1, tn=2 * q_rank)
    w_kr = _spread_rope_cols(w_in[:, 2 * q_rank:]).astype(BF16)
    k_rope_t = proj(x, norm_w, w_kr, n_blocks=1, tn=LANE, rope=tables, rope_mask=1, transposed=True, out_dtype=BF16)
    w_uq3 = w_uq.reshape(q_rank, H, MLA_QK_DIM)
    w_uq_p = jnp.concatenate([w_uq3[:, :, :MLA_NOPE_DIM].reshape(q_rank, H * MLA_NOPE_DIM),
                              _spread_rope_cols(w_uq3[:, :, MLA_NOPE_DIM:]).reshape(q_rank, H * LANE)],
                             axis=1).astype(BF16)
    tn = 4 * LANE
    n_nope = H * MLA_NOPE_DIM // tn
    n_all = n_nope + H * LANE // tn
    q = proj(latent, q_norm_w, w_uq_p, n_blocks=n_all, tn=tn, x_col_block=0, rope=tables,
             rope_mask=(1 << n_all) - (1 << n_nope), scale_mask=(1 << n_all) - 1, scale=q_scale,
             out_dtype=BF16).reshape(B, S, n_all * tn)
    w_ukv3 = w_ukv.reshape(kv_rank, H, MLA_NOPE_DIM + MLA_V_DIM)
    w_uk = w_ukv3[:, :, :MLA_NOPE_DIM].reshape(kv_rank, H * MLA_NOPE_DIM).astype(BF16)
    w_uv = w_ukv3[:, :, MLA_NOPE_DIM:].reshape(kv_rank, H * MLA_V_DIM).astype(BF16)
    k_nope_t = proj(latent, kv_norm_w, w_uk, n_blocks=H * MLA_NOPE_DIM // tn, tn=tn, x_col_block=1, transposed=True,
                    out_dtype=BF16)
    v = proj(latent, kv_norm_w, w_uv, n_blocks=H * MLA_V_DIM // tn, tn=tn, x_col_block=1,
             out_dtype=BF16).reshape(B, S, H * MLA_V_DIM)
    o = flash_attention(q, q, k_nope_t, k_rope_t, v, n_groups=H, R=1, E=1,
                        qm_col=lambda h: h, qx_col=lambda h: H + h, km_col=lambda h: h, v_col=lambda h: h,
                        kx_batched=True, mode="causal", tq=1024, tk=1024, depth=4)
    return matmul_residual([o.reshape(T, H * MLA_V_DIM)], w_out.astype(BF16), x)


def kernel(x, positions, ffn_norm_w, ffn_w_in, ffn_w_out, mix_norm_w, nsa_w_in, nsa_cmp_pe, nsa_cmp_w1, nsa_cmp_w2,
           nsa_w_out, mla_w_in, mla_q_norm_w, mla_kv_norm_w, mla_w_uq, mla_w_ukv, mla_w_out, final_norm_w):
    B, S, D = x.shape
    depth = ffn_norm_w.shape[0]
    n_mixers = 2
    h = x.reshape(B * S, D)
    for i in range(depth):
        h = ffn_half_step(h, ffn_norm_w[i, 0], ffn_w_in[i, 0].astype(BF16), ffn_w_out[i, 0].astype(BF16))
        j = i // n_mixers
        if i % n_mixers == 0:
            h = nsa_mixer(h, B, S, positions, mix_norm_w[i], nsa_w_in[j], nsa_cmp_pe[j], nsa_cmp_w1[j],
                          nsa_cmp_w2[j], nsa_w_out[j])
        else:
            h = mla_mixer(h, B, S, positions, mix_norm_w[i], mla_w_in[j], mla_q_norm_w[j], mla_kv_norm_w[j],
                          mla_w_uq[j], mla_w_ukv[j], mla_w_out[j])
        last = i == depth - 1
        h = ffn_half_step(h, ffn_norm_w[i, 1], ffn_w_in[i, 1].astype(BF16), ffn_w_out[i, 1].astype(BF16),
                          final_w=final_norm_w if last else None)
    return h.reshape(B, S, D)
```

```python
import functools
import math

import jax
import jax.numpy as jnp
from jax import lax
from jax.experimental import pallas as pl
from jax.experimental.pallas import tpu as pltpu

F32 = jnp.float32
BF16 = jnp.bfloat16

LANE = 128
BF16_ROWS = 16
RMS_EPS = 1e-6
ROPE_THETA = 10000.0
NEG_INF = -1e30
MASK_PASS = 3e38
FORCE_BONUS = 1e9
LOG2E = math.log2(math.e)
VMEM_LIMIT = 56 * 1024 * 1024
FFN_VMEM_LIMIT = 61 * 1024 * 1024
FFN_SLAB = 64

NSA_HEADS = 16
NSA_HEAD_DIM = 128
NSA_KV_GROUPS = 4
NSA_R = NSA_HEADS // NSA_KV_GROUPS
NSA_N_BRANCHES = 3
NSA_CMP_LEN = 32
NSA_CMP_STRIDE = 16
NSA_SEL_BLOCK = 64
NSA_N_SELECT = 16
NSA_WINDOW = 512

MLA_HEADS = 16
MLA_NOPE_DIM = 128
MLA_ROPE_DIM = 64
MLA_V_DIM = 128
MLA_QK_DIM = MLA_NOPE_DIM + MLA_ROPE_DIM


def _pick(n, pref):
    if n <= pref:
        return n
    t = (pref // LANE) * LANE
    while t > LANE and n % t:
        t -= LANE
    assert n % t == 0, (n, pref)
    return t


def _div_pow2(x, d):
    assert d & (d - 1) == 0, d
    return jnp.right_shift(x, d.bit_length() - 1)


def _mod_pow2(x, d):
    assert d & (d - 1) == 0, d
    return jnp.bitwise_and(x, d - 1)


def _params(*sem, vmem=VMEM_LIMIT):
    return pltpu.CompilerParams(dimension_semantics=sem, vmem_limit_bytes=vmem)


def _repeat_lanes(x, n):
    return jnp.concatenate([x] * n, axis=1)


def _rotate_half(y, c2, s2):
    return y * c2 + pltpu.roll(y, LANE // 2, axis=1) * s2


def _rms_rows(x, w):
    ms = jnp.mean(x * x, axis=-1, keepdims=True)
    return x * lax.rsqrt(ms + RMS_EPS) * w


def _proj_kernel(*refs, has_rope, rope_mask, scale_mask, scale, sigmoid, transposed):
    if has_rope:
        x_ref, nw_ref, w_ref, c2_ref, s2_ref, o_ref, xn_scr = refs
    else:
        x_ref, nw_ref, w_ref, o_ref, xn_scr = refs
    j = pl.program_id(1)

    @pl.when(j == 0)
    def _():
        xn_scr[...] = _rms_rows(x_ref[...], nw_ref[...]).astype(BF16)

    y = jnp.dot(xn_scr[...], w_ref[...], preferred_element_type=F32)
    if sigmoid:
        y = jax.nn.sigmoid(y)
    if has_rope:
        rope_on = jnp.bitwise_and(lax.shift_right_logical(jnp.int32(rope_mask), j), 1) == 1
        factor = jnp.where(jnp.bitwise_and(lax.shift_right_logical(jnp.int32(scale_mask), j), 1) == 1, scale, 1.0)
        c2 = jnp.where(rope_on, c2_ref[...], 1.0) * factor
        s2 = jnp.where(rope_on, s2_ref[...], 0.0) * factor
    for h in range(y.shape[1] // LANE):
        slab = y[:, h * LANE:(h + 1) * LANE]
        if has_rope:
            slab = _rotate_half(slab, c2, s2)
        if transposed:
            o_ref[h * LANE:(h + 1) * LANE, :] = slab.T.astype(o_ref.dtype)
        else:
            o_ref[:, h * LANE:(h + 1) * LANE] = slab.astype(o_ref.dtype)


def proj(x, norm_w, w, *, n_blocks, tn, col_map=None, x_col_block=0, rope=None, rope_mask=0, scale_mask=0, scale=1.0,
         sigmoid=False, transposed=False, out_dtype=F32, tm=1024):
    T = x.shape[0]
    K = w.shape[0]
    tm = _pick(T, tm)
    col_map = col_map or (lambda j: j)
    has_rope = rope is not None
    assert has_rope or scale_mask == 0
    in_specs = [
        pl.BlockSpec((tm, K), lambda i, j: (i, x_col_block)),
        pl.BlockSpec((1, K), lambda i, j: (0, 0)),
        pl.BlockSpec((K, tn), lambda i, j: (0, col_map(j))),
    ]
    args = [x, norm_w.reshape(1, K).astype(F32), w]
    if has_rope:
        in_specs += [pl.BlockSpec((tm, LANE), lambda i, j: (i, 0))] * 2
        args += list(rope)
    kern = functools.partial(_proj_kernel, has_rope=has_rope, rope_mask=rope_mask, scale_mask=scale_mask,
                             scale=scale, sigmoid=sigmoid, transposed=transposed)
    if transposed:
        out_spec = pl.BlockSpec((tn, tm), lambda i, j: (j, i))
        out_shape = jax.ShapeDtypeStruct((n_blocks * tn, T), out_dtype)
    else:
        out_spec = pl.BlockSpec((tm, tn), lambda i, j: (i, j))
        out_shape = jax.ShapeDtypeStruct((T, n_blocks * tn), out_dtype)
    return pl.pallas_call(
        kern,
        grid=(T // tm, n_blocks),
        in_specs=in_specs,
        out_specs=out_spec,
        out_shape=out_shape,
        scratch_shapes=[pltpu.VMEM((tm, K), BF16)],
        compiler_params=_params("parallel", "arbitrary"),
        name="proj",
    )(*args)


def _matmul_residual_kernel(*refs):
    *a_refs, w_ref, r_ref, o_ref = refs
    a = a_refs[0][...]
    if len(a_refs) > 1:
        a = a.astype(F32)
        for a_ref in a_refs[1:]:
            a = a + a_ref[...].astype(F32)
        a = a.astype(BF16)
    o_ref[...] = r_ref[...] + jnp.dot(a, w_ref[...], preferred_element_type=F32)


def matmul_residual(a_list, w, res, *, tm=1024, tn=1024):
    T, K = a_list[0].shape
    N = w.shape[1]
    tm, tn = _pick(T, tm), _pick(N, tn)
    return pl.pallas_call(
        _matmul_residual_kernel,
        grid=(T // tm, N // tn),
        in_specs=[pl.BlockSpec((tm, K), lambda i, j: (i, 0))] * len(a_list) + [
            pl.BlockSpec((K, tn), lambda i, j: (0, j)),
            pl.BlockSpec((tm, tn), lambda i, j: (i, j)),
        ],
        out_specs=pl.BlockSpec((tm, tn), lambda i, j: (i, j)),
        out_shape=jax.ShapeDtypeStruct((T, N), F32),
        compiler_params=_params("parallel", "arbitrary"),
        name="matmul_residual",
    )(*a_list, w, res)


def _ffn_kernel(x_ref, nw_ref, wg_ref, wu_ref, wo_ref, fw_ref, o_ref, xn_scr, *, final_norm):
    f = pl.program_id(1)
    tm = x_ref.shape[0]
    slab = min(FFN_SLAB, tm)

    def over_slabs(fn):
        def body(r, carry):
            fn(pl.ds(pl.multiple_of(r * slab, slab), slab))
            return carry

        lax.fori_loop(0, tm // slab, body, 0)

    @pl.when(f == 0)
    def _():
        def prologue(rows):
            xn_scr[rows, :] = _rms_rows(x_ref[rows, :], nw_ref[...]).astype(BF16)

        over_slabs(prologue)
        o_ref[...] = jnp.zeros_like(o_ref)

    xn = xn_scr[...]
    g = jnp.dot(xn, wg_ref[...], preferred_element_type=F32)
    u = jnp.dot(xn, wu_ref[...], preferred_element_type=F32)
    h = (g * jax.nn.sigmoid(g) * u).astype(BF16)
    o_ref[...] += jnp.dot(h, wo_ref[...], preferred_element_type=F32)

    @pl.when(f == pl.num_programs(1) - 1)
    def _():
        def epilogue(rows):
            y = x_ref[rows, :] + 0.5 * o_ref[rows, :]
            if final_norm:
                y = _rms_rows(y, fw_ref[...])
            o_ref[rows, :] = y

        over_slabs(epilogue)


def ffn_half_step(x, norm_w, w_in, w_out, final_w=None, *, tm=1024, tf=512):
    T, D = x.shape
    DFF = w_out.shape[0]
    tm, tf = _pick(T, tm), _pick(DFF, tf)
    nf = DFF // tf
    final_norm = final_w is not None
    fw = (final_w if final_norm else norm_w).reshape(1, D).astype(F32)
    return pl.pallas_call(
        functools.partial(_ffn_kernel, final_norm=final_norm),
        grid=(T // tm, nf),
        in_specs=[
            pl.BlockSpec((tm, D), lambda i, f: (i, 0)),
            pl.BlockSpec((1, D), lambda i, f: (0, 0)),
            pl.BlockSpec((D, tf), lambda i, f: (0, f)),
            pl.BlockSpec((D, tf), lambda i, f: (0, f + nf)),
            pl.BlockSpec((tf, D), lambda i, f: (f, 0)),
            pl.BlockSpec((1, D), lambda i, f: (0, 0)),
        ],
        out_specs=pl.BlockSpec((tm, D), lambda i, f: (i, 0)),
        out_shape=jax.ShapeDtypeStruct((T, D), F32),
        scratch_shapes=[pltpu.VMEM((tm, D), BF16)],
        compiler_params=_params("parallel", "arbitrary", vmem=FFN_VMEM_LIMIT),
        name="ffn_half_step",
    )(x, norm_w.reshape(1, D).astype(F32), w_in, w_in, w_out, fw)


def _flash_kernel(*refs, R, E, has_kx, gate_cols, tq, tk, mode, chunks_per_version, depth):
    it = iter(refs)
    qm_ref = next(it)
    qx_ref = next(it) if E else None
    km_ref = next(it)
    kx_ref = next(it) if has_kx else None
    v_ref = next(it)
    gate_ref = next(it) if gate_cols else None
    cap_ref = next(it)
    o_ref = next(it)
    q_scr, m_scr, acc_scr, s_scr = it
    M = R * tq
    i = pl.program_id(2)

    for e in range(max(E, 1)):
        for r in range(R):
            q_scr[e, r * tq:(r + 1) * tq, 0:LANE] = qm_ref[0, :, r * LANE:(r + 1) * LANE]
            if E:
                q_scr[e, r * tq:(r + 1) * tq, LANE:2 * LANE] = qx_ref[0, :, e * LANE:(e + 1) * LANE]
    m_scr[...] = jnp.full_like(m_scr, NEG_INF)
    acc_scr[...] = jnp.zeros_like(acc_scr)
    ones_col = jnp.where(lax.broadcasted_iota(jnp.int32, (tk, LANE), 1) == 0, 1.0, 0.0).astype(BF16)

    def scores(c):
        k0 = pl.multiple_of(c * tk, tk)
        kt = km_ref[:, pl.ds(k0, tk)]
        if has_kx:
            kt = jnp.concatenate([kt, kx_ref[:, pl.ds(k0, tk)]], axis=0)
        q = q_scr[c // chunks_per_version] if E > 1 else q_scr[0]
        return jnp.dot(q, kt, preferred_element_type=F32).astype(BF16)

    def softmax_pv(c, s, cap=None):
        k0 = pl.multiple_of(c * tk, tk)
        if cap is not None:
            s = jnp.minimum(s, cap)
        m_prev = m_scr[...]
        m_new = jnp.maximum(m_prev, jnp.max(s, axis=1, keepdims=True).astype(F32))
        alpha = jnp.exp2(m_prev - m_new)
        p = jnp.exp2(s - _repeat_lanes(m_new.astype(BF16), tk // LANE))
        v1 = jnp.concatenate([v_ref[0, pl.ds(k0, tk), :], ones_col], axis=1)
        acc_scr[...] = _repeat_lanes(alpha, 2) * acc_scr[...] + jnp.dot(p, v1, preferred_element_type=F32)
        m_scr[...] = m_new

    if mode == "causal":
        n_full = (i * tq) // tk
        bufs = [s_scr.at[u] for u in range(depth)]
        bufs[0][...] = scores(0)

        def group(j, carry):
            for u in range(depth):
                bufs[(u + 1) % depth][...] = scores(depth * j + u + 1)
                softmax_pv(depth * j + u, bufs[u][...])
            return carry

        n_groups = n_full // depth
        lax.fori_loop(0, n_groups, group, 0)

        def tail(c, carry):
            softmax_pv(c, bufs[0][...])
            bufs[0][...] = scores(c + 1)
            return carry

        lax.fori_loop(depth * n_groups, n_full, tail, 0)
        softmax_pv(n_full, bufs[0][...], cap_ref[0])
    else:
        prev = jnp.maximum(i - 1, 0)
        s_prev = scores(prev)
        s_diag = scores(i)
        softmax_pv(prev, s_prev, jnp.where(i > 0, cap_ref[1], NEG_INF))
        softmax_pv(i, s_diag, cap_ref[0])

    out = acc_scr[:, 0:LANE] / acc_scr[:, LANE:LANE + 1]
    for r in range(R):
        o_r = out[r * tq:(r + 1) * tq]
        if gate_cols:
            o_r = o_r * gate_ref[0, :, gate_cols[r]:gate_cols[r] + 1]
        o_ref[0, :, r * LANE:(r + 1) * LANE] = o_r.astype(o_ref.dtype)


def flash_attention(qm, qx, km, kx, v, gate=None, *, n_groups, R, E, qm_col, qx_col=None, km_col, v_col, kx_batched,
                    gate_cols=None, mode, window=0, tq, tk, depth=2, keys_per_version=None):
    B, S, _ = qm.shape
    tq, tk = _pick(S, tq), _pick(S, tk)
    assert tq == tk
    assert mode == "causal" or tq == window
    has_kx = kx is not None
    dk = 2 * LANE if has_kx else LANE
    cpv = (keys_per_version // tk) if keys_per_version else 1
    M = R * tq
    in_specs = [pl.BlockSpec((1, tq, R * LANE), lambda b, g, i: (b, i, qm_col(g)))]
    args = [qm]
    if E:
        in_specs.append(pl.BlockSpec((1, tq, E * LANE), lambda b, g, i: (b, i, qx_col(g))))
        args.append(qx)
    in_specs.append(pl.BlockSpec((LANE, S), lambda b, g, i: (km_col(g), b)))
    args.append(km)
    if has_kx:
        in_specs.append(pl.BlockSpec((LANE, S), lambda b, g, i: (0, b if kx_batched else 0)))
        args.append(kx)
    in_specs.append(pl.BlockSpec((1, S, LANE), lambda b, g, i: (b, 0, v_col(g))))
    args.append(v)
    if gate_cols:
        in_specs.append(pl.BlockSpec((1, tq, LANE), lambda b, g, i: (b, i, g)))
        args.append(gate)
    q_off = lax.broadcasted_iota(jnp.int32, (M, tk), 0) % tq
    k_off = lax.broadcasted_iota(jnp.int32, (M, tk), 1)
    caps = [k_off <= q_off] + ([k_off > q_off] if mode == "window" else [])
    cap = jnp.stack([jnp.where(c, MASK_PASS, NEG_INF).astype(BF16) for c in caps])
    in_specs.append(pl.BlockSpec(cap.shape, lambda b, g, i: (0, 0, 0)))
    args.append(cap)
    kern = functools.partial(_flash_kernel, R=R, E=E, has_kx=has_kx, gate_cols=gate_cols, tq=tq, tk=tk, mode=mode,
                             chunks_per_version=cpv, depth=depth)
    return pl.pallas_call(
        kern,
        grid=(B, n_groups, S // tq),
        in_specs=in_specs,
        out_specs=pl.BlockSpec((1, tq, R * LANE), lambda b, g, i: (b, i, g)),
        out_shape=jax.ShapeDtypeStruct((B, S, n_groups * R * LANE), BF16),
        scratch_shapes=[
            pltpu.VMEM((max(E, 1), M, dk), BF16),
            pltpu.VMEM((M, LANE), F32),
            pltpu.VMEM((M, 2 * LANE), F32),
            pltpu.VMEM((depth, M, tk), BF16),
        ],
        compiler_params=_params("parallel", "parallel", "arbitrary"),
        name="flash_" + mode + ("_x%d" % E if E else ""),
    )(*args)


def _compress_kernel(tok_ref, pe_ref, w1_ref, w2_ref, c2_ref, s2_ref, o_ref, *, ncp):
    half = NSA_CMP_STRIDE
    a = b = None
    for l in range(half):
        x = tok_ref[0, pl.ds(l, ncp, stride=half), :]
        da = jnp.dot((x + pe_ref[0, l:l + 1, :]).astype(BF16), w1_ref[0, l], preferred_element_type=F32)
        db = jnp.dot((x + pe_ref[0, half + l:half + l + 1, :]).astype(BF16), w1_ref[0, half + l],
                     preferred_element_type=F32)
        a = da if a is None else a + da
        b = db if b is None else b + db
    h = jax.nn.gelu(a + pltpu.roll(b, ncp - 1, axis=0))
    y = jnp.dot(h.astype(BF16), w2_ref[0], preferred_element_type=F32)
    is_key = pl.program_id(0) == 0
    c2 = jnp.where(is_key, c2_ref[0], 1.0)
    s2 = jnp.where(is_key, s2_ref[0], 0.0)
    o_ref[0, 0, 0] = _rotate_half(y, c2, s2).astype(o_ref.dtype)


def compress_tokens(tok, pe, w1, w2, c2, s2):
    B, S, _ = tok.shape
    G, dk = NSA_KV_GROUPS, NSA_HEAD_DIM
    ncp = S // NSA_CMP_STRIDE
    hidden = w2.shape[1]
    return pl.pallas_call(
        functools.partial(_compress_kernel, ncp=ncp),
        grid=(2, B, G),
        in_specs=[
            pl.BlockSpec((1, S, dk), lambda s, b, g: (b, 0, s * G + g)),
            pl.BlockSpec((1, NSA_CMP_LEN, dk), lambda s, b, g: (s, 0, 0)),
            pl.BlockSpec((1, NSA_CMP_LEN, dk, hidden), lambda s, b, g: (s, 0, 0, 0)),
            pl.BlockSpec((1, hidden, dk), lambda s, b, g: (s, 0, 0)),
            pl.BlockSpec((1, ncp, dk), lambda s, b, g: (b, 0, 0)),
            pl.BlockSpec((1, ncp, dk), lambda s, b, g: (b, 0, 0)),
        ],
        out_specs=pl.BlockSpec((1, 1, 1, ncp, dk), lambda s, b, g: (s, b, g, 0, 0)),
        out_shape=jax.ShapeDtypeStruct((2, B, G, ncp, dk), BF16),
        compiler_params=_params("parallel", "parallel", "parallel"),
        name="compress_tokens",
    )(tok, pe, w1, w2, c2, s2)


def _cmp_select_kernel(q_ref, kc_ref, vc_ref, ovt_ref, gate_ref, oc_all_ref, nm_all_ref, oc_ref, nm_ref, *,
                       R, tq, n_sub, tile0, n_cmp, n_top):
    del oc_all_ref, nm_all_ref
    for u in range(n_sub):
        rows = slice(u * tq, (u + 1) * tq)
        t0 = (tile0 + pl.program_id(2) * n_sub + u) * tq
        _cmp_select_tile(q_ref.at[0, rows], kc_ref, vc_ref, ovt_ref, gate_ref.at[0, rows], oc_ref.at[0, rows],
                         nm_ref.at[0, rows], t0, R=R, tq=tq, n_cmp=n_cmp, n_top=n_top)


def _cmp_select_tile(q_ref, kc_ref, vc_ref, ovt_ref, gate_ref, oc_ref, nm_ref, t0, *, R, tq, n_cmp, n_top):
    M = R * tq
    q = jnp.concatenate([q_ref[:, r * LANE:(r + 1) * LANE] for r in range(R)], axis=0)
    kc = kc_ref[0, 0, 0]
    ncp = kc.shape[0]
    s = lax.dot_general(q, kc, (((1,), (1,)), ((), ())), preferred_element_type=F32)
    t = t0 + _mod_pow2(lax.broadcasted_iota(jnp.int32, (M, 1), 0), tq)
    lim = jnp.minimum(_div_pow2(t - (NSA_CMP_LEN - 1), NSA_CMP_STRIDE) + 1, n_cmp)
    valid = lax.broadcasted_iota(jnp.int32, (M, ncp), 1) < lim
    sm = jnp.where(valid, s, NEG_INF)
    e = jnp.exp2((sm - jnp.max(sm, axis=1, keepdims=True)).astype(BF16))
    ones_col = jnp.where(lax.broadcasted_iota(jnp.int32, (ncp, LANE), 1) == 0, 1.0, 0.0).astype(BF16)
    oc = jnp.dot(e, jnp.concatenate([vc_ref[0, 0, 0], ones_col], axis=1), preferred_element_type=F32)
    oc = oc[:, 0:LANE] * jnp.where(lim > 0, 1.0 / oc[:, LANE:LANE + 1], 0.0)
    for r in range(R):
        gate = gate_ref[:, NSA_N_BRANCHES * r:NSA_N_BRANCHES * r + 1]
        oc_ref[:, r * LANE:(r + 1) * LANE] = (oc[r * tq:(r + 1) * tq] * gate).astype(oc_ref.dtype)

    nb = ovt_ref.shape[0] - BF16_ROWS
    imp_un = lax.dot_general(ovt_ref[...], e, (((1,), (1,)), ((), ())), preferred_element_type=F32)
    t_l = t0 + _mod_pow2(lax.broadcasted_iota(jnp.int32, (1, M), 1), tq)
    lim_l = jnp.minimum(_div_pow2(t_l - (NSA_CMP_LEN - 1), NSA_CMP_STRIDE) + 1, n_cmp)
    imp_h = imp_un[0:nb] * jnp.where(lim_l > 0, 1.0 / imp_un[nb:nb + 1], 0.0)
    imp = imp_h[:, 0:tq]
    for r in range(1, R):
        imp = imp + imp_h[:, r * tq:(r + 1) * tq]
    j = lax.broadcasted_iota(jnp.int32, (nb, tq), 0)
    jf = j.astype(F32)
    cur = _div_pow2(t0 + lax.broadcasted_iota(jnp.int32, (nb, tq), 1), NSA_SEL_BLOCK)
    forced = jnp.logical_or(j == 0, jnp.logical_or(j == cur, j == cur - 1))
    x = jnp.where(j > cur, NEG_INF, imp + FORCE_BONUS * forced.astype(F32))
    for _ in range(n_top):
        mx = jnp.max(x, axis=0, keepdims=True)
        first = jnp.min(jnp.where(x == mx, jf, float(nb)), axis=0, keepdims=True)
        x = jnp.where(jf == first, jnp.where(mx > 0.5 * NEG_INF, -jnp.inf, NEG_INF), x)
    nm_ref[:, 0:nb] = jnp.where(x == -jnp.inf, 0.0, NEG_INF).T.astype(nm_ref.dtype)
    if nb < nm_ref.shape[1]:
        nm_ref[:, nb:] = jnp.full((tq, nm_ref.shape[1] - nb), NEG_INF, nm_ref.dtype)


def cmp_attention_select(q, kv_cmp, overlap, gate, *, n_cmp, n_top, tq=128, n_sub=4):
    B, S, _ = q.shape
    G, R = NSA_KV_GROUPS, NSA_R
    NB, NCP = overlap.shape
    tq = _pick(S, tq)
    tb = n_sub * tq
    n_parts = next(n for n in (4, 2, 1) if NCP % (n * LANE) == 0 and S % (n * tb) == 0)
    steps = S // (n_parts * tb)
    outs = [jnp.zeros((B, S, G * R * LANE), BF16), jnp.zeros((B, S, G * NB), BF16)]
    for part in range(n_parts):
        ncp_w = NCP * (part + 1) // n_parts
        nb_w = min(NB, -(-(S * (part + 1) // n_parts // NSA_SEL_BLOCK) // LANE) * LANE)
        ov = jnp.concatenate([overlap[:nb_w, :ncp_w], jnp.ones((1, ncp_w), BF16),
                              jnp.zeros((BF16_ROWS - 1, ncp_w), BF16)], axis=0)
        kern = functools.partial(_cmp_select_kernel, R=R, tq=tq, n_sub=n_sub, tile0=part * steps * n_sub,
                                 n_cmp=n_cmp, n_top=n_top)
        off = part * steps
        carried = list(outs)
        outs = pl.pallas_call(
            kern,
            grid=(B, G, steps),
            in_specs=[
                pl.BlockSpec((1, tb, R * LANE), lambda b, g, i, off=off: (b, i + off, g)),
                pl.BlockSpec((1, 1, 1, ncp_w, LANE), lambda b, g, i: (0, b, g, 0, 0)),
                pl.BlockSpec((1, 1, 1, ncp_w, LANE), lambda b, g, i: (1, b, g, 0, 0)),
                pl.BlockSpec((nb_w + BF16_ROWS, ncp_w), lambda b, g, i: (0, 0)),
                pl.BlockSpec((1, tb, LANE), lambda b, g, i, off=off: (b, i + off, g)),
            ] + [pl.BlockSpec(memory_space=pl.ANY)] * len(carried),
            out_specs=[
                pl.BlockSpec((1, tb, R * LANE), lambda b, g, i, off=off: (b, i + off, g)),
                pl.BlockSpec((1, tb, NB), lambda b, g, i, off=off: (b, i + off, g)),
            ],
            out_shape=[
                jax.ShapeDtypeStruct((B, S, G * R * LANE), BF16),
                jax.ShapeDtypeStruct((B, S, G * NB), BF16),
            ],
            input_output_aliases={5 + k: k for k in range(len(carried))},
            compiler_params=_params("parallel", "parallel", "arbitrary"),
            name="cmp_attention_select",
        )(q, kv_cmp, kv_cmp, ov, gate, *carried)
    return outs


def _rope_tables(pos, dim):
    inv = 1.0 / (ROPE_THETA ** (jnp.arange(0, dim, 2, dtype=F32) / dim))
    ang = pos.astype(F32)[..., None] * inv
    pad = [(0, 0)] * (ang.ndim - 1) + [(0, (LANE - dim) // 2)]
    c, s = jnp.pad(jnp.cos(ang), pad), jnp.pad(jnp.sin(ang), pad)
    return jnp.concatenate([c, c], axis=-1), jnp.concatenate([-s, s], axis=-1)


def _spread_rope_cols(w):
    dim = w.shape[-1]
    pad = [(0, 0)] * (w.ndim - 1) + [(0, (LANE - dim) // 2)]
    return jnp.concatenate([jnp.pad(w[..., :dim // 2], pad), jnp.pad(w[..., dim // 2:], pad)], axis=-1)


def nsa_mixer(x, B, S, positions, norm_w, w_in, cmp_pe, cmp_w1, cmp_w2, w_out):
    T, D = x.shape
    H, G, R, dk = NSA_HEADS, NSA_KV_GROUPS, NSA_R, NSA_HEAD_DIM
    q_cols, kv_cols = H * dk, NSA_N_BRANCHES * 2 * G * dk
    grp = G * dk
    q_scale = dk ** -0.5 * LOG2E
    tables = _rope_tables(positions.reshape(T), dk)

    w_qkv = w_in[:, :q_cols + kv_cols].astype(BF16)
    nq = q_cols // grp
    qv = proj(x, norm_w, w_qkv, n_blocks=nq + 2, tn=grp, col_map=lambda j: jnp.where(j < nq, j, 2 * j - 1),
              rope=tables, rope_mask=(1 << nq) - 1, scale_mask=(1 << nq) - 1, scale=q_scale,
              out_dtype=BF16).reshape(B, S, q_cols + 2 * grp)
    k_t = proj(x, norm_w, w_qkv, n_blocks=2, tn=grp, col_map=lambda j: nq + 2 + 2 * j, rope=tables, rope_mask=0b11,
               transposed=True, out_dtype=BF16)
    cmp_tok = proj(x, norm_w, w_qkv, n_blocks=1, tn=2 * grp, col_map=lambda j: q_cols // (2 * grp)).reshape(B, S, 2 * grp)
    w_gate = w_in[:, q_cols + kv_cols:].reshape(D, G, R * NSA_N_BRANCHES)
    w_gate = jnp.pad(w_gate, ((0, 0), (0, 0), (0, LANE - R * NSA_N_BRANCHES))).reshape(D, G * LANE).astype(BF16)
    gates = proj(x, norm_w, w_gate, n_blocks=1, tn=G * LANE, sigmoid=True).reshape(B, S, G * LANE)

    ncp = S // NSA_CMP_STRIDE
    n_cmp = (S - NSA_CMP_LEN) // NSA_CMP_STRIDE + 1
    cmp_end = jnp.minimum(jnp.arange(ncp) * NSA_CMP_STRIDE + NSA_CMP_LEN - 1, S - 1)
    c2c, s2c = _rope_tables(positions[:, cmp_end], dk)
    hidden = cmp_w1.shape[-1]
    kv_cmp = compress_tokens(cmp_tok, cmp_pe, cmp_w1.reshape(2, NSA_CMP_LEN, dk, hidden).astype(BF16),
                             cmp_w2.astype(BF16), c2c, s2c)

    n_sel = S // NSA_SEL_BLOCK
    nb = -(-n_sel // LANE) * LANE
    blk_start = jnp.arange(ncp) * NSA_CMP_STRIDE
    sel_start = jnp.arange(nb) * NSA_SEL_BLOCK
    overlap = ((blk_start[None, :] <= sel_start[:, None] + NSA_SEL_BLOCK - 1)
               & (blk_start[None, :] + NSA_CMP_LEN - 1 >= sel_start[:, None])
               & (jnp.arange(ncp)[None, :] < n_cmp) & (jnp.arange(nb)[:, None] < n_sel)).astype(BF16)
    o_cmp, neg_mask = cmp_attention_select(qv, kv_cmp, overlap, gates, n_cmp=n_cmp, n_top=min(NSA_N_SELECT, n_sel))

    blk_onehot = (jnp.arange(LANE)[:, None] == jnp.arange(S)[None, :] // NSA_SEL_BLOCK % LANE).astype(BF16)
    v0 = q_cols // dk
    o_slc = flash_attention(qv, neg_mask, k_t, blk_onehot, qv, gates, n_groups=G, R=R, E=nb // LANE,
                            qm_col=lambda g: g, qx_col=lambda g: g, km_col=lambda g: g,
                            v_col=lambda g: v0 + g, kx_batched=False,
                            gate_cols=tuple(NSA_N_BRANCHES * r + 1 for r in range(R)), mode="causal",
                            tq=512, tk=512, depth=4, keys_per_version=LANE * NSA_SEL_BLOCK)
    o_win = flash_attention(qv, None, k_t, None, qv, gates, n_groups=G, R=R, E=0,
                            qm_col=lambda g: g, km_col=lambda g: G + g, v_col=lambda g: v0 + G + g,
                            kx_batched=False, gate_cols=tuple(NSA_N_BRANCHES * r + 2 for r in range(R)),
                            mode="window", window=NSA_WINDOW, tq=512, tk=512)
    return matmul_residual([o.reshape(T, H * dk) for o in (o_cmp, o_slc, o_win)], w_out.astype(BF16), x)


def mla_mixer(x, B, S, positions, norm_w, w_in, q_norm_w, kv_norm_w, w_uq, w_ukv, w_out):
    T, D = x.shape
    H = MLA_HEADS
    q_rank, kv_rank = q_norm_w.shape[0], kv_norm_w.shape[0]
    assert q_rank == kv_rank and q_rank % LANE == 0
    q_scale = MLA_QK_DIM ** -0.5 * LOG2E
    tables = _rope_tables(positions.reshape(T), MLA_ROPE_DIM)

    latent = proj(x, norm_w, w_in[:, :2 * q_rank].astype(BF16), n_blocks=1, tn=2 * q_rank)
    w_kr = _spread_rope_cols(w_in[:, 2 * q_rank:]).astype(BF16)
    k_rope_t = proj(x, norm_w, w_kr, n_blocks=1, tn=LANE, rope=tables, rope_mask=1, transposed=True, out_dtype=BF16)
    w_uq3 = w_uq.reshape(q_rank, H, MLA_QK_DIM)
    w_uq_p = jnp.concatenate([w_uq3[:, :, :MLA_NOPE_DIM].reshape(q_rank, H * MLA_NOPE_DIM),
                              _spread_rope_cols(w_uq3[:, :, MLA_NOPE_DIM:]).reshape(q_rank, H * LANE)],
                             axis=1).astype(BF16)
    tn = 4 * LANE
    n_nope = H * MLA_NOPE_DIM // tn
    n_all = n_nope + H * LANE // tn
    q = proj(latent, q_norm_w, w_uq_p, n_blocks=n_all, tn=tn, x_col_block=0, rope=tables,
             rope_mask=(1 << n_all) - (1 << n_nope), scale_mask=(1 << n_all) - 1, scale=q_scale,
             out_dtype=BF16).reshape(B, S, n_all * tn)
    w_ukv3 = w_ukv.reshape(kv_rank, H, MLA_NOPE_DIM + MLA_V_DIM)
    w_uk = w_ukv3[:, :, :MLA_NOPE_DIM].reshape(kv_rank, H * MLA_NOPE_DIM).astype(BF16)
    w_uv = w_ukv3[:, :, MLA_NOPE_DIM:].reshape(kv_rank, H * MLA_V_DIM).astype(BF16)
    k_nope_t = proj(latent, kv_norm_w, w_uk, n_blocks=H * MLA_NOPE_DIM // tn, tn=tn, x_col_block=1, transposed=True,
                    out_dtype=BF16)
    v = proj(latent, kv_norm_w, w_uv, n_blocks=H * MLA_V_DIM // tn, tn=tn, x_col_block=1,
             out_dtype=BF16).reshape(B, S, H * MLA_V_DIM)
    o = flash_attention(q, q, k_nope_t, k_rope_t, v, n_groups=H, R=1, E=1,
                        qm_col=lambda h: h, qx_col=lambda h: H + h, km_col=lambda h: h, v_col=lambda h: h,
                        kx_batched=True, mode="causal", tq=1024, tk=1024, depth=4)
    return matmul_residual([o.reshape(T, H * MLA_V_DIM)], w_out.astype(BF16), x)


def kernel(x, positions, ffn_norm_w, ffn_w_in, ffn_w_out, mix_norm_w, nsa_w_in, nsa_cmp_pe, nsa_cmp_w1, nsa_cmp_w2,
           nsa_w_out, mla_w_in, mla_q_norm_w, mla_kv_norm_w, mla_w_uq, mla_w_ukv, mla_w_out, final_norm_w):
    B, S, D = x.shape
    depth = ffn_norm_w.shape[0]
    n_mixers = 2
    h = x.reshape(B * S, D)
    for i in range(depth):
        h = ffn_half_step(h, ffn_norm_w[i, 0], ffn_w_in[i, 0].astype(BF16), ffn_w_out[i, 0].astype(BF16))
        j = i // n_mixers
        if i % n_mixers == 0:
            h = nsa_mixer(h, B, S, positions, mix_norm_w[i], nsa_w_in[j], nsa_cmp_pe[j], nsa_cmp_w1[j],
                          nsa_cmp_w2[j], nsa_w_out[j])
        else:
            h = mla_mixer(h, B, S, positions, mix_norm_w[i], mla_w_in[j], mla_q_norm_w[j], mla_kv_norm_w[j],
                          mla_w_uq[j], mla_w_ukv[j], mla_w_out[j])
        last = i == depth - 1
        h = ffn_half_step(h, ffn_norm_w[i, 1], ffn_w_in[i, 1].astype(BF16), ffn_w_out[i, 1].astype(BF16),
                          final_w=final_norm_w if last else None)
    return h.reshape(B, S, D)
```

```python
import functools
import math

import jax
import jax.numpy as jnp
from jax import lax
from jax.experimental import pallas as pl
from jax.experimental.pallas import tpu as pltpu

F32 = jnp.float32
BF16 = jnp.bfloat16

LANE = 128
BF16_ROWS = 16
RMS_EPS = 1e-6
ROPE_THETA = 10000.0
NEG_INF = -1e30
MASK_PASS = 3e38
FORCE_BONUS = 1e9
LOG2E = math.log2(math.e)
VMEM_LIMIT = 56 * 1024 * 1024
FFN_VMEM_LIMIT = 61 * 1024 * 1024
FFN_SLAB = 64

NSA_HEADS = 16
NSA_HEAD_DIM = 128
NSA_KV_GROUPS = 4
NSA_R = NSA_HEADS // NSA_KV_GROUPS
NSA_N_BRANCHES = 3
NSA_CMP_LEN = 32
NSA_CMP_STRIDE = 16
NSA_SEL_BLOCK = 64
NSA_N_SELECT = 16
NSA_WINDOW = 512

MLA_HEADS = 16
MLA_NOPE_DIM = 128
MLA_ROPE_DIM = 64
MLA_V_DIM = 128
MLA_QK_DIM = MLA_NOPE_DIM + MLA_ROPE_DIM


def _pick(n, pref):
    if n <= pref:
        return n
    t = (pref // LANE) * LANE
    while t > LANE and n % t:
        t -= LANE
    assert n % t == 0, (n, pref)
    return t


def _div_pow2(x, d):
    assert d & (d - 1) == 0, d
    return jnp.right_shift(x, d.bit_length() - 1)


def _mod_pow2(x, d):
    assert d & (d - 1) == 0, d
    return jnp.bitwise_and(x, d - 1)


def _params(*sem, vmem=VMEM_LIMIT):
    return pltpu.CompilerParams(dimension_semantics=sem, vmem_limit_bytes=vmem)


def _repeat_lanes(x, n):
    return jnp.concatenate([x] * n, axis=1)


def _rotate_half(y, c2, s2):
    return y * c2 + pltpu.roll(y, LANE // 2, axis=1) * s2


def _rms_rows(x, w):
    ms = jnp.mean(x * x, axis=-1, keepdims=True)
    return x * lax.rsqrt(ms + RMS_EPS) * w


def _proj_kernel(*refs, has_rope, rope_mask, scale_mask, scale, sigmoid, transposed):
    if has_rope:
        x_ref, nw_ref, w_ref, c2_ref, s2_ref, o_ref, xn_scr = refs
    else:
        x_ref, nw_ref, w_ref, o_ref, xn_scr = refs
    j = pl.program_id(1)

    @pl.when(j == 0)
    def _():
        xn_scr[...] = _rms_rows(x_ref[...], nw_ref[...]).astype(BF16)

    y = jnp.dot(xn_scr[...], w_ref[...], preferred_element_type=F32)
    if sigmoid:
        y = jax.nn.sigmoid(y)
    if has_rope:
        rope_on = jnp.bitwise_and(lax.shift_right_logical(jnp.int32(rope_mask), j), 1) == 1
        factor = jnp.where(jnp.bitwise_and(lax.shift_right_logical(jnp.int32(scale_mask), j), 1) == 1, scale, 1.0)
        c2 = jnp.where(rope_on, c2_ref[...], 1.0) * factor
        s2 = jnp.where(rope_on, s2_ref[...], 0.0) * factor
    for h in range(y.shape[1] // LANE):
        slab = y[:, h * LANE:(h + 1) * LANE]
        if has_rope:
            slab = _rotate_half(slab, c2, s2)
        if transposed:
            o_ref[h * LANE:(h + 1) * LANE, :] = slab.T.astype(o_ref.dtype)
        else:
            o_ref[:, h * LANE:(h + 1) * LANE] = slab.astype(o_ref.dtype)


def proj(x, norm_w, w, *, n_blocks, tn, col_map=None, x_col_block=0, rope=None, rope_mask=0, scale_mask=0, scale=1.0,
         sigmoid=False, transposed=False, out_dtype=F32, tm=1024):
    T = x.shape[0]
    K = w.shape[0]
    tm = _pick(T, tm)
    col_map = col_map or (lambda j: j)
    has_rope = rope is not None
    assert has_rope or scale_mask == 0
    in_specs = [
        pl.BlockSpec((tm, K), lambda i, j: (i, x_col_block)),
        pl.BlockSpec((1, K), lambda i, j: (0, 0)),
        pl.BlockSpec((K, tn), lambda i, j: (0, col_map(j))),
    ]
    args = [x, norm_w.reshape(1, K).astype(F32), w]
    if has_rope:
        in_specs += [pl.BlockSpec((tm, LANE), lambda i, j: (i, 0))] * 2
        args += list(rope)
    kern = functools.partial(_proj_kernel, has_rope=has_rope, rope_mask=rope_mask, scale_mask=scale_mask,
                             scale=scale, sigmoid=sigmoid, transposed=transposed)
    if transposed:
        out_spec = pl.BlockSpec((tn, tm), lambda i, j: (j, i))
        out_shape = jax.ShapeDtypeStruct((n_blocks * tn, T), out_dtype)
    else:
        out_spec = pl.BlockSpec((tm, tn), lambda i, j: (i, j))
        out_shape = jax.ShapeDtypeStruct((T, n_blocks * tn), out_dtype)
    return pl.pallas_call(
        kern,
        grid=(T // tm, n_blocks),
        in_specs=in_specs,
        out_specs=out_spec,
        out_shape=out_shape,
        scratch_shapes=[pltpu.VMEM((tm, K), BF16)],
        compiler_params=_params("parallel", "arbitrary"),
        name="proj",
    )(*args)


def _matmul_residual_kernel(*refs):
    *a_refs, w_ref, r_ref, o_ref = refs
    a = a_refs[0][...]
    if len(a_refs) > 1:
        a = a.astype(F32)
        for a_ref in a_refs[1:]:
            a = a + a_ref[...].astype(F32)
        a = a.astype(BF16)
    o_ref[...] = r_ref[...] + jnp.dot(a, w_ref[...], preferred_element_type=F32)


def matmul_residual(a_list, w, res, *, tm=1024, tn=1024):
    T, K = a_list[0].shape
    N = w.shape[1]
    tm, tn = _pick(T, tm), _pick(N, tn)
    return pl.pallas_call(
        _matmul_residual_kernel,
        grid=(T // tm, N // tn),
        in_specs=[pl.BlockSpec((tm, K), lambda i, j: (i, 0))] * len(a_list) + [
            pl.BlockSpec((K, tn), lambda i, j: (0, j)),
            pl.BlockSpec((tm, tn), lambda i, j: (i, j)),
        ],
        out_specs=pl.BlockSpec((tm, tn), lambda i, j: (i, j)),
        out_shape=jax.ShapeDtypeStruct((T, N), F32),
        compiler_params=_params("parallel", "arbitrary"),
        name="matmul_residual",
    )(*a_list, w, res)


def _ffn_kernel(x_ref, nw_ref, wg_ref, wu_ref, wo_ref, fw_ref, o_ref, xn_scr, *, final_norm):
    f = pl.program_id(1)
    tm = x_ref.shape[0]
    slab = min(FFN_SLAB, tm)

    def over_slabs(fn):
        def body(r, carry):
            fn(pl.ds(pl.multiple_of(r * slab, slab), slab))
            return carry

        lax.fori_loop(0, tm // slab, body, 0)

    @pl.when(f == 0)
    def _():
        def prologue(rows):
            xn_scr[rows, :] = _rms_rows(x_ref[rows, :], nw_ref[...]).astype(BF16)

        over_slabs(prologue)
        o_ref[...] = jnp.zeros_like(o_ref)

    xn = xn_scr[...]
    g = jnp.dot(xn, wg_ref[...], preferred_element_type=F32)
    u = jnp.dot(xn, wu_ref[...], preferred_element_type=F32)
    h = (g * jax.nn.sigmoid(g) * u).astype(BF16)
    o_ref[...] += jnp.dot(h, wo_ref[...], preferred_element_type=F32)

    @pl.when(f == pl.num_programs(1) - 1)
    def _():
        def epilogue(rows):
            y = x_ref[rows, :] + 0.5 * o_ref[rows, :]
            if final_norm:
                y = _rms_rows(y, fw_ref[...])
            o_ref[rows, :] = y

        over_slabs(epilogue)


def ffn_half_step(x, norm_w, w_in, w_out, final_w=None, *, tm=1024, tf=512):
    T, D = x.shape
    DFF = w_out.shape[0]
    tm, tf = _pick(T, tm), _pick(DFF, tf)
    nf = DFF // tf
    final_norm = final_w is not None
    fw = (final_w if final_norm else norm_w).reshape(1, D).astype(F32)
    return pl.pallas_call(
        functools.partial(_ffn_kernel, final_norm=final_norm),
        grid=(T // tm, nf),
        in_specs=[
            pl.BlockSpec((tm, D), lambda i, f: (i, 0)),
            pl.BlockSpec((1, D), lambda i, f: (0, 0)),
            pl.BlockSpec((D, tf), lambda i, f: (0, f)),
            pl.BlockSpec((D, tf), lambda i, f: (0, f + nf)),
            pl.BlockSpec((tf, D), lambda i, f: (f, 0)),
            pl.BlockSpec((1, D), lambda i, f: (0, 0)),
        ],
        out_specs=pl.BlockSpec((tm, D), lambda i, f: (i, 0)),
        out_shape=jax.ShapeDtypeStruct((T, D), F32),
        scratch_shapes=[pltpu.VMEM((tm, D), BF16)],
        compiler_params=_params("parallel", "arbitrary", vmem=FFN_VMEM_LIMIT),
        name="ffn_half_step",
    )(x, norm_w.reshape(1, D).astype(F32), w_in, w_in, w_out, fw)


def _flash_kernel(*refs, R, E, has_kx, gate_cols, tq, tk, mode, chunks_per_version, depth):
    it = iter(refs)
    qm_ref = next(it)
    qx_ref = next(it) if E else None
    km_ref = next(it)
    kx_ref = next(it) if has_kx else None
    v_ref = next(it)
    gate_ref = next(it) if gate_cols else None
    cap_ref = next(it)
    o_ref = next(it)
    q_scr, m_scr, acc_scr, s_scr = it
    M = R * tq
    i = pl.program_id(2)

    for e in range(max(E, 1)):
        for r in range(R):
            q_scr[e, r * tq:(r + 1) * tq, 0:LANE] = qm_ref[0, :, r * LANE:(r + 1) * LANE]
            if E:
                q_scr[e, r * tq:(r + 1) * tq, LANE:2 * LANE] = qx_ref[0, :, e * LANE:(e + 1) * LANE]
    m_scr[...] = jnp.full_like(m_scr, NEG_INF)
    acc_scr[...] = jnp.zeros_like(acc_scr)
    ones_col = jnp.where(lax.broadcasted_iota(jnp.int32, (tk, LANE), 1) == 0, 1.0, 0.0).astype(BF16)

    def scores(c):
        k0 = pl.multiple_of(c * tk, tk)
        kt = km_ref[:, pl.ds(k0, tk)]
        if has_kx:
            kt = jnp.concatenate([kt, kx_ref[:, pl.ds(k0, tk)]], axis=0)
        q = q_scr[c // chunks_per_version] if E > 1 else q_scr[0]
        return jnp.dot(q, kt, preferred_element_type=F32).astype(BF16)

    def softmax_pv(c, s, cap=None):
        k0 = pl.multiple_of(c * tk, tk)
        if cap is not None:
            s = jnp.minimum(s, cap)
        m_prev = m_scr[...]
        m_new = jnp.maximum(m_prev, jnp.max(s, axis=1, keepdims=True).astype(F32))
        alpha = jnp.exp2(m_prev - m_new)
        p = jnp.exp2(s - _repeat_lanes(m_new.astype(BF16), tk // LANE))
        v1 = jnp.concatenate([v_ref[0, pl.ds(k0, tk), :], ones_col], axis=1)
        acc_scr[...] = _repeat_lanes(alpha, 2) * acc_scr[...] + jnp.dot(p, v1, preferred_element_type=F32)
        m_scr[...] = m_new

    if mode == "causal":
        n_full = (i * tq) // tk
        bufs = [s_scr.at[u] for u in range(depth)]
        bufs[0][...] = scores(0)

        def group(j, carry):
            for u in range(depth):
                bufs[(u + 1) % depth][...] = scores(depth * j + u + 1)
                softmax_pv(depth * j + u, bufs[u][...])
            return carry

        n_groups = n_full // depth
        lax.fori_loop(0, n_groups, group, 0)

        def tail(c, carry):
            softmax_pv(c, bufs[0][...])
            bufs[0][...] = scores(c + 1)
            return carry

        lax.fori_loop(depth * n_groups, n_full, tail, 0)
        softmax_pv(n_full, bufs[0][...], cap_ref[0])
    else:
        prev = jnp.maximum(i - 1, 0)
        s_prev = scores(prev)
        s_diag = scores(i)
        softmax_pv(prev, s_prev, jnp.where(i > 0, cap_ref[1], NEG_INF))
        softmax_pv(i, s_diag, cap_ref[0])

    out = acc_scr[:, 0:LANE] / acc_scr[:, LANE:LANE + 1]
    for r in range(R):
        o_r = out[r * tq:(r + 1) * tq]
        if gate_cols:
            o_r = o_r * gate_ref[0, :, gate_cols[r]:gate_cols[r] + 1]
        o_ref[0, :, r * LANE:(r + 1) * LANE] = o_r.astype(o_ref.dtype)


def flash_attention(qm, qx, km, kx, v, gate=None, *, n_groups, R, E, qm_col, qx_col=None, km_col, v_col, kx_batched,
                    gate_cols=None, mode, window=0, tq, tk, depth=2, keys_per_version=None):
    B, S, _ = qm.shape
    tq, tk = _pick(S, tq), _pick(S, tk)
    assert tq == tk
    assert mode == "causal" or tq == window
    has_kx = kx is not None
    dk = 2 * LANE if has_kx else LANE
    cpv = (keys_per_version // tk) if keys_per_version else 1
    M = R * tq
    in_specs = [pl.BlockSpec((1, tq, R * LANE), lambda b, g, i: (b, i, qm_col(g)))]
    args = [qm]
    if E:
        in_specs.append(pl.BlockSpec((1, tq, E * LANE), lambda b, g, i: (b, i, qx_col(g))))
        args.append(qx)
    in_specs.append(pl.BlockSpec((LANE, S), lambda b, g, i: (km_col(g), b)))
    args.append(km)
    if has_kx:
        in_specs.append(pl.BlockSpec((LANE, S), lambda b, g, i: (0, b if kx_batched else 0)))
        args.append(kx)
    in_specs.append(pl.BlockSpec((1, S, LANE), lambda b, g, i: (b, 0, v_col(g))))
    args.append(v)
    if gate_cols:
        in_specs.append(pl.BlockSpec((1, tq, LANE), lambda b, g, i: (b, i, g)))
        args.append(gate)
    q_off = lax.broadcasted_iota(jnp.int32, (M, tk), 0) % tq
    k_off = lax.broadcasted_iota(jnp.int32, (M, tk), 1)
    caps = [k_off <= q_off] + ([k_off > q_off] if mode == "window" else [])
    cap = jnp.stack([jnp.where(c, MASK_PASS, NEG_INF).astype(BF16) for c in caps])
    in_specs.append(pl.BlockSpec(cap.shape, lambda b, g, i: (0, 0, 0)))
    args.append(cap)
    kern = functools.partial(_flash_kernel, R=R, E=E, has_kx=has_kx, gate_cols=gate_cols, tq=tq, tk=tk, mode=mode,
                             chunks_per_version=cpv, depth=depth)
    return pl.pallas_call(
        kern,
        grid=(B, n_groups, S // tq),
        in_specs=in_specs,
        out_specs=pl.BlockSpec((1, tq, R * LANE), lambda b, g, i: (b, i, g)),
        out_shape=jax.ShapeDtypeStruct((B, S, n_groups * R * LANE), BF16),
        scratch_shapes=[
            pltpu.VMEM((max(E, 1), M, dk), BF16),
            pltpu.VMEM((M, LANE), F32),
            pltpu.VMEM((M, 2 * LANE), F32),
            pltpu.VMEM((depth, M, tk), BF16),
        ],
        compiler_params=_params("parallel", "parallel", "arbitrary"),
        name="flash_" + mode + ("_x%d" % E if E else ""),
    )(*args)


def _compress_kernel(tok_ref, pe_ref, w1_ref, w2_ref, c2_ref, s2_ref, o_ref, *, ncp):
    half = NSA_CMP_STRIDE
    a = b = None
    for l in range(half):
        x = tok_ref[0, pl.ds(l, ncp, stride=half), :]
        da = jnp.dot((x + pe_ref[0, l:l + 1, :]).astype(BF16), w1_ref[0, l], preferred_element_type=F32)
        db = jnp.dot((x + pe_ref[0, half + l:half + l + 1, :]).astype(BF16), w1_ref[0, half + l],
                     preferred_element_type=F32)
        a = da if a is None else a + da
        b = db if b is None else b + db
    h = jax.nn.gelu(a + pltpu.roll(b, ncp - 1, axis=0))
    y = jnp.dot(h.astype(BF16), w2_ref[0], preferred_element_type=F32)
    is_key = pl.program_id(0) == 0
    c2 = jnp.where(is_key, c2_ref[0], 1.0)
    s2 = jnp.where(is_key, s2_ref[0], 0.0)
    o_ref[0, 0, 0] = _rotate_half(y, c2, s2).astype(o_ref.dtype)


def compress_tokens(tok, pe, w1, w2, c2, s2):
    B, S, _ = tok.shape
    G, dk = NSA_KV_GROUPS, NSA_HEAD_DIM
    ncp = S // NSA_CMP_STRIDE
    hidden = w2.shape[1]
    return pl.pallas_call(
        functools.partial(_compress_kernel, ncp=ncp),
        grid=(2, B, G),
        in_specs=[
            pl.BlockSpec((1, S, dk), lambda s, b, g: (b, 0, s * G + g)),
            pl.BlockSpec((1, NSA_CMP_LEN, dk), lambda s, b, g: (s, 0, 0)),
            pl.BlockSpec((1, NSA_CMP_LEN, dk, hidden), lambda s, b, g: (s, 0, 0, 0)),
            pl.BlockSpec((1, hidden, dk), lambda s, b, g: (s, 0, 0)),
            pl.BlockSpec((1, ncp, dk), lambda s, b, g: (b, 0, 0)),
            pl.BlockSpec((1, ncp, dk), lambda s, b, g: (b, 0, 0)),
        ],
        out_specs=pl.BlockSpec((1, 1, 1, ncp, dk), lambda s, b, g: (s, b, g, 0, 0)),
        out_shape=jax.ShapeDtypeStruct((2, B, G, ncp, dk), BF16),
        compiler_params=_params("parallel", "parallel", "parallel"),
        name="compress_tokens",
    )(tok, pe, w1, w2, c2, s2)


def _cmp_select_kernel(q_ref, kc_ref, vc_ref, ovt_ref, gate_ref, oc_all_ref, nm_all_ref, oc_ref, nm_ref, *,
                       R, tq, n_sub, tile0, n_cmp, n_top):
    del oc_all_ref, nm_all_ref
    for u in range(n_sub):
        rows = slice(u * tq, (u + 1) * tq)
        t0 = (tile0 + pl.program_id(2) * n_sub + u) * tq
        _cmp_select_tile(q_ref.at[0, rows], kc_ref, vc_ref, ovt_ref, gate_ref.at[0, rows], oc_ref.at[0, rows],
                         nm_ref.at[0, rows], t0, R=R, tq=tq, n_cmp=n_cmp, n_top=n_top)


def _cmp_select_tile(q_ref, kc_ref, vc_ref, ovt_ref, gate_ref, oc_ref, nm_ref, t0, *, R, tq, n_cmp, n_top):
    M = R * tq
    q = jnp.concatenate([q_ref[:, r * LANE:(r + 1) * LANE] for r in range(R)], axis=0)
    kc = kc_ref[0, 0, 0]
    ncp = kc.shape[0]
    s = lax.dot_general(q, kc, (((1,), (1,)), ((), ())), preferred_element_type=F32)
    t = t0 + _mod_pow2(lax.broadcasted_iota(jnp.int32, (M, 1), 0), tq)
    lim = jnp.minimum(_div_pow2(t - (NSA_CMP_LEN - 1), NSA_CMP_STRIDE) + 1, n_cmp)
    valid = lax.broadcasted_iota(jnp.int32, (M, ncp), 1) < lim
    sm = jnp.where(valid, s, NEG_INF)
    e = jnp.exp2((sm - jnp.max(sm, axis=1, keepdims=True)).astype(BF16))
    ones_col = jnp.where(lax.broadcasted_iota(jnp.int32, (ncp, LANE), 1) == 0, 1.0, 0.0).astype(BF16)
    oc = jnp.dot(e, jnp.concatenate([vc_ref[0, 0, 0], ones_col], axis=1), preferred_element_type=F32)
    oc = oc[:, 0:LANE] * jnp.where(lim > 0, 1.0 / oc[:, LANE:LANE + 1], 0.0)
    for r in range(R):
        gate = gate_ref[:, NSA_N_BRANCHES * r:NSA_N_BRANCHES * r + 1]
        oc_ref[:, r * LANE:(r + 1) * LANE] = (oc[r * tq:(r + 1) * tq] * gate).astype(oc_ref.dtype)

    nb = ovt_ref.shape[0] - BF16_ROWS
    imp_un = lax.dot_general(ovt_ref[...], e, (((1,), (1,)), ((), ())), preferred_element_type=F32)
    t_l = t0 + _mod_pow2(lax.broadcasted_iota(jnp.int32, (1, M), 1), tq)
    lim_l = jnp.minimum(_div_pow2(t_l - (NSA_CMP_LEN - 1), NSA_CMP_STRIDE) + 1, n_cmp)
    imp_h = imp_un[0:nb] * jnp.where(lim_l > 0, 1.0 / imp_un[nb:nb + 1], 0.0)
    imp = imp_h[:, 0:tq]
    for r in range(1, R):
        imp = imp + imp_h[:, r * tq:(r + 1) * tq]
    j = lax.broadcasted_iota(jnp.int32, (nb, tq), 0)
    jf = j.astype(F32)
    cur = _div_pow2(t0 + lax.broadcasted_iota(jnp.int32, (nb, tq), 1), NSA_SEL_BLOCK)
    forced = jnp.logical_or(j == 0, jnp.logical_or(j == cur, j == cur - 1))
    x = jnp.where(j > cur, NEG_INF, imp + FORCE_BONUS * forced.astype(F32))
    for _ in range(n_top):
        mx = jnp.max(x, axis=0, keepdims=True)
        first = jnp.min(jnp.where(x == mx, jf, float(nb)), axis=0, keepdims=True)
        x = jnp.where(jf == first, jnp.where(mx > 0.5 * NEG_INF, -jnp.inf, NEG_INF), x)
    nm_ref[:, 0:nb] = jnp.where(x == -jnp.inf, 0.0, NEG_INF).T.astype(nm_ref.dtype)
    if nb < nm_ref.shape[1]:
        nm_ref[:, nb:] = jnp.full((tq, nm_ref.shape[1] - nb), NEG_INF, nm_ref.dtype)


def cmp_attention_select(q, kv_cmp, overlap, gate, *, n_cmp, n_top, tq=128, n_sub=4):
    B, S, _ = q.shape
    G, R = NSA_KV_GROUPS, NSA_R
    NB, NCP = overlap.shape
    tq = _pick(S, tq)
    tb = n_sub * tq
    n_parts = next(n for n in (4, 2, 1) if NCP % (n * LANE) == 0 and S % (n * tb) == 0)
    steps = S // (n_parts * tb)
    outs = [jnp.zeros((B, S, G * R * LANE), BF16), jnp.zeros((B, S, G * NB), BF16)]
    for part in range(n_parts):
        ncp_w = NCP * (part + 1) // n_parts
        nb_w = min(NB, -(-(S * (part + 1) // n_parts // NSA_SEL_BLOCK) // LANE) * LANE)
        ov = jnp.concatenate([overlap[:nb_w, :ncp_w], jnp.ones((1, ncp_w), BF16),
                              jnp.zeros((BF16_ROWS - 1, ncp_w), BF16)], axis=0)
        kern = functools.partial(_cmp_select_kernel, R=R, tq=tq, n_sub=n_sub, tile0=part * steps * n_sub,
                                 n_cmp=n_cmp, n_top=n_top)
        off = part * steps
        carried = list(outs)
        outs = pl.pallas_call(
            kern,
            grid=(B, G, steps),
            in_specs=[
                pl.BlockSpec((1, tb, R * LANE), lambda b, g, i, off=off: (b, i + off, g)),
                pl.BlockSpec((1, 1, 1, ncp_w, LANE), lambda b, g, i: (0, b, g, 0, 0)),
                pl.BlockSpec((1, 1, 1, ncp_w, LANE), lambda b, g, i: (1, b, g, 0, 0)),
                pl.BlockSpec((nb_w + BF16_ROWS, ncp_w), lambda b, g, i: (0, 0)),
                pl.BlockSpec((1, tb, LANE), lambda b, g, i, off=off: (b, i + off, g)),
            ] + [pl.BlockSpec(memory_space=pl.ANY)] * len(carried),
            out_specs=[
                pl.BlockSpec((1, tb, R * LANE), lambda b, g, i, off=off: (b, i + off, g)),
                pl.BlockSpec((1, tb, NB), lambda b, g, i, off=off: (b, i + off, g)),
            ],
            out_shape=[
                jax.ShapeDtypeStruct((B, S, G * R * LANE), BF16),
                jax.ShapeDtypeStruct((B, S, G * NB), BF16),
            ],
            input_output_aliases={5 + k: k for k in range(len(carried))},
            compiler_params=_params("parallel", "parallel", "arbitrary"),
            name="cmp_attention_select",
        )(q, kv_cmp, kv_cmp, ov, gate, *carried)
    return outs


def _rope_tables(pos, dim):
    inv = 1.0 / (ROPE_THETA ** (jnp.arange(0, dim, 2, dtype=F32) / dim))
    ang = pos.astype(F32)[..., None] * inv
    pad = [(0, 0)] * (ang.ndim - 1) + [(0, (LANE - dim) // 2)]
    c, s = jnp.pad(jnp.cos(ang), pad), jnp.pad(jnp.sin(ang), pad)
    return jnp.concatenate([c, c], axis=-1), jnp.concatenate([-s, s], axis=-1)


def _spread_rope_cols(w):
    dim = w.shape[-1]
    pad = [(0, 0)] * (w.ndim - 1) + [(0, (LANE - dim) // 2)]
    return jnp.concatenate([jnp.pad(w[..., :dim // 2], pad), jnp.pad(w[..., dim // 2:], pad)], axis=-1)


def nsa_mixer(x, B, S, positions, norm_w, w_in, cmp_pe, cmp_w1, cmp_w2, w_out):
    T, D = x.shape
    H, G, R, dk = NSA_HEADS, NSA_KV_GROUPS, NSA_R, NSA_HEAD_DIM
    q_cols, kv_cols = H * dk, NSA_N_BRANCHES * 2 * G * dk
    grp = G * dk
    q_scale = dk ** -0.5 * LOG2E
    tables = _rope_tables(positions.reshape(T), dk)

    w_qkv = w_in[:, :q_cols + kv_cols].astype(BF16)
    nq = q_cols // grp
    qv = proj(x, norm_w, w_qkv, n_blocks=nq + 2, tn=grp, col_map=lambda j: jnp.where(j < nq, j, 2 * j - 1),
              rope=tables, rope_mask=(1 << nq) - 1, scale_mask=(1 << nq) - 1, scale=q_scale,
              out_dtype=BF16).reshape(B, S, q_cols + 2 * grp)
    k_t = proj(x, norm_w, w_qkv, n_blocks=2, tn=grp, col_map=lambda j: nq + 2 + 2 * j, rope=tables, rope_mask=0b11,
               transposed=True, out_dtype=BF16)
    cmp_tok = proj(x, norm_w, w_qkv, n_blocks=1, tn=2 * grp, col_map=lambda j: q_cols // (2 * grp)).reshape(B, S, 2 * grp)
    w_gate = w_in[:, q_cols + kv_cols:].reshape(D, G, R * NSA_N_BRANCHES)
    w_gate = jnp.pad(w_gate, ((0, 0), (0, 0), (0, LANE - R * NSA_N_BRANCHES))).reshape(D, G * LANE).astype(BF16)
    gates = proj(x, norm_w, w_gate, n_blocks=1, tn=G * LANE, sigmoid=True).reshape(B, S, G * LANE)

    ncp = S // NSA_CMP_STRIDE
    n_cmp = (S - NSA_CMP_LEN) // NSA_CMP_STRIDE + 1
    cmp_end = jnp.minimum(jnp.arange(ncp) * NSA_CMP_STRIDE + NSA_CMP_LEN - 1, S - 1)
    c2c, s2c = _rope_tables(positions[:, cmp_end], dk)
    hidden = cmp_w1.shape[-1]
    kv_cmp = compress_tokens(cmp_tok, cmp_pe, cmp_w1.reshape(2, NSA_CMP_LEN, dk, hidden).astype(BF16),
                             cmp_w2.astype(BF16), c2c, s2c)

    n_sel = S // NSA_SEL_BLOCK
    nb = -(-n_sel // LANE) * LANE
    blk_start = jnp.arange(ncp) * NSA_CMP_STRIDE
    sel_start = jnp.arange(nb) * NSA_SEL_BLOCK
    overlap = ((blk_start[None, :] <= sel_start[:, None] + NSA_SEL_BLOCK - 1)
               & (blk_start[None, :] + NSA_CMP_LEN - 1 >= sel_start[:, None])
               & (jnp.arange(ncp)[None, :] < n_cmp) & (jnp.arange(nb)[:, None] < n_sel)).astype(BF16)
    o_cmp, neg_mask = cmp_attention_select(qv, kv_cmp, overlap, gates, n_cmp=n_cmp, n_top=min(NSA_N_SELECT, n_sel))

    blk_onehot = (jnp.arange(LANE)[:, None] == jnp.arange(S)[None, :] // NSA_SEL_BLOCK % LANE).astype(BF16)
    v0 = q_cols // dk
    o_slc = flash_attention(qv, neg_mask, k_t, blk_onehot, qv, gates, n_groups=G, R=R, E=nb // LANE,
                            qm_col=lambda g: g, qx_col=lambda g: g, km_col=lambda g: g,
                            v_col=lambda g: v0 + g, kx_batched=False,
                            gate_cols=tuple(NSA_N_BRANCHES * r + 1 for r in range(R)), mode="causal",
                            tq=512, tk=512, depth=4, keys_per_version=LANE * NSA_SEL_BLOCK)
    o_win = flash_attention(qv, None, k_t, None, qv, gates, n_groups=G, R=R, E=0,
                            qm_col=lambda g: g, km_col=lambda g: G + g, v_col=lambda g: v0 + G + g,
                            kx_batched=False, gate_cols=tuple(NSA_N_BRANCHES * r + 2 for r in range(R)),
                            mode="window", window=NSA_WINDOW, tq=512, tk=512)
    return matmul_residual([o.reshape(T, H * dk) for o in (o_cmp, o_slc, o_win)], w_out.astype(BF16), x)


def mla_mixer(x, B, S, positions, norm_w, w_in, q_norm_w, kv_norm_w, w_uq, w_ukv, w_out):
    T, D = x.shape
    H = MLA_HEADS
    q_rank, kv_rank = q_norm_w.shape[0], kv_norm_w.shape[0]
    assert q_rank == kv_rank and q_rank % LANE == 0
    q_scale = MLA_QK_DIM ** -0.5 * LOG2E
    tables = _rope_tables(positions.reshape(T), MLA_ROPE_DIM)

    latent = proj(x, norm_w, w_in[:, :2 * q_rank].astype(BF16), n_blocks=1, tn=2 * q_rank)
    w_kr = _spread_rope_cols(w_in[:, 2 * q_rank:]).astype(BF16)
    k_rope_t = proj(x, norm_w, w_kr, n_blocks=1, tn=LANE, rope=tables, rope_mask=1, transposed=True, out_dtype=BF16)
    w_uq3 = w_uq.reshape(q_rank, H, MLA_QK_DIM)
    w_uq_p = jnp.concatenate([w_uq3[:, :, :MLA_NOPE_DIM].reshape(q_rank, H * MLA_NOPE_DIM),
                              _spread_rope_cols(w_uq3[:, :, MLA_NOPE_DIM:]).reshape(q_rank, H * LANE)],
                             axis=1).astype(BF16)
    tn = H * LANE
    n_nope = H * MLA_NOPE_DIM // tn
    n_all = n_nope + H * LANE // tn
    q = proj(latent, q_norm_w, w_uq_p, n_blocks=n_all, tn=tn, x_col_block=0, rope=tables,
             rope_mask=(1 << n_all) - (1 << n_nope), scale_mask=(1 << n_all) - 1, scale=q_scale,
             out_dtype=BF16).reshape(B, S, n_all * tn)
    w_ukv3 = w_ukv.reshape(kv_rank, H, MLA_NOPE_DIM + MLA_V_DIM)
    w_uk = w_ukv3[:, :, :MLA_NOPE_DIM].reshape(kv_rank, H * MLA_NOPE_DIM).astype(BF16)
    w_uv = w_ukv3[:, :, MLA_NOPE_DIM:].reshape(kv_rank, H * MLA_V_DIM).astype(BF16)
    k_nope_t = proj(latent, kv_norm_w, w_uk, n_blocks=H * MLA_NOPE_DIM // tn, tn=tn, x_col_block=1, transposed=True,
                    out_dtype=BF16)
    v = proj(latent, kv_norm_w, w_uv, n_blocks=H * MLA_V_DIM // tn, tn=tn, x_col_block=1,
             out_dtype=BF16).reshape(B, S, H * MLA_V_DIM)
    o = flash_attention(q, q, k_nope_t, k_rope_t, v, n_groups=H, R=1, E=1,
                        qm_col=lambda h: h, qx_col=lambda h: H + h, km_col=lambda h: h, v_col=lambda h: h,
                        kx_batched=True, mode="causal", tq=1024, tk=1024, depth=4)
    return matmul_residual([o.reshape(T, H * MLA_V_DIM)], w_out.astype(BF16), x)


def kernel(x, positions, ffn_norm_w, ffn_w_in, ffn_w_out, mix_norm_w, nsa_w_in, nsa_cmp_pe, nsa_cmp_w1, nsa_cmp_w2,
           nsa_w_out, mla_w_in, mla_q_norm_w, mla_kv_norm_w, mla_w_uq, mla_w_ukv, mla_w_out, final_norm_w):
    B, S, D = x.shape
    depth = ffn_norm_w.shape[0]
    n_mixers = 2
    h = x.reshape(B * S, D)
    for i in range(depth):
        h = ffn_half_step(h, ffn_norm_w[i, 0], ffn_w_in[i, 0].astype(BF16), ffn_w_out[i, 0].astype(BF16))
        j = i // n_mixers
        if i % n_mixers == 0:
            h = nsa_mixer(h, B, S, positions, mix_norm_w[i], nsa_w_in[j], nsa_cmp_pe[j], nsa_cmp_w1[j],
                          nsa_cmp_w2[j], nsa_w_out[j])
        else:
            h = mla_mixer(h, B, S, positions, mix_norm_w[i], mla_w_in[j], mla_q_norm_w[j], mla_kv_norm_w[j],
                          mla_w_uq[j], mla_w_ukv[j], mla_w_out[j])
        last = i == depth - 1
        h = ffn_half_step(h, ffn_norm_w[i, 1], ffn_w_in[i, 1].astype(BF16), ffn_w_out[i, 1].astype(BF16),
                          final_w=final_norm_w if last else None)
    return h.reshape(B, S, D)
```

```python
import functools
import math

import jax
import jax.numpy as jnp
from jax import lax
from jax.experimental import pallas as pl
from jax.experimental.pallas import tpu as pltpu

F32 = jnp.float32
BF16 = jnp.bfloat16

LANE = 128
BF16_ROWS = 16
RMS_EPS = 1e-6
ROPE_THETA = 10000.0
NEG_INF = -1e30
MASK_PASS = 3e38
FORCE_BONUS = 1e9
LOG2E = math.log2(math.e)
VMEM_LIMIT = 56 * 1024 * 1024
FFN_VMEM_LIMIT = 61 * 1024 * 1024
FFN_SLAB = 64

NSA_HEADS = 16
NSA_HEAD_DIM = 128
NSA_KV_GROUPS = 4
NSA_R = NSA_HEADS // NSA_KV_GROUPS
NSA_N_BRANCHES = 3
NSA_CMP_LEN = 32
NSA_CMP_STRIDE = 16
NSA_SEL_BLOCK = 64
NSA_N_SELECT = 16
NSA_WINDOW = 512

MLA_HEADS = 16
MLA_NOPE_DIM = 128
MLA_ROPE_DIM = 64
MLA_V_DIM = 128
MLA_QK_DIM = MLA_NOPE_DIM + MLA_ROPE_DIM


def _pick(n, pref):
    if n <= pref:
        return n
    t = (pref // LANE) * LANE
    while t > LANE and n % t:
        t -= LANE
    assert n % t == 0, (n, pref)
    return t


def _div_pow2(x, d):
    assert d & (d - 1) == 0, d
    return jnp.right_shift(x, d.bit_length() - 1)


def _mod_pow2(x, d):
    assert d & (d - 1) == 0, d
    return jnp.bitwise_and(x, d - 1)


def _params(*sem, vmem=VMEM_LIMIT):
    return pltpu.CompilerParams(dimension_semantics=sem, vmem_limit_bytes=vmem)


def _repeat_lanes(x, n):
    return jnp.concatenate([x] * n, axis=1)


def _rotate_half(y, c2, s2):
    return y * c2 + pltpu.roll(y, LANE // 2, axis=1) * s2


def _rms_rows(x, w):
    ms = jnp.mean(x * x, axis=-1, keepdims=True)
    return x * lax.rsqrt(ms + RMS_EPS) * w


def _proj_kernel(*refs, has_rope, rope_mask, scale_mask, scale, sigmoid_from, transposed):
    if has_rope:
        x_ref, nw_ref, w_ref, c2_ref, s2_ref, o_ref, xn_scr = refs
    else:
        x_ref, nw_ref, w_ref, o_ref, xn_scr = refs
    j = pl.program_id(1)

    @pl.when(j == 0)
    def _():
        xn_scr[...] = _rms_rows(x_ref[...], nw_ref[...]).astype(BF16)

    y = jnp.dot(xn_scr[...], w_ref[...], preferred_element_type=F32)
    if has_rope:
        rope_on = jnp.bitwise_and(lax.shift_right_logical(jnp.int32(rope_mask), j), 1) == 1
        factor = jnp.where(jnp.bitwise_and(lax.shift_right_logical(jnp.int32(scale_mask), j), 1) == 1, scale, 1.0)
        c2 = jnp.where(rope_on, c2_ref[...], 1.0) * factor
        s2 = jnp.where(rope_on, s2_ref[...], 0.0) * factor
    for h in range(y.shape[1] // LANE):
        slab = y[:, h * LANE:(h + 1) * LANE]
        if has_rope:
            slab = _rotate_half(slab, c2, s2)
        if sigmoid_from is not None and h >= sigmoid_from:
            slab = jax.nn.sigmoid(slab)
        if transposed:
            o_ref[h * LANE:(h + 1) * LANE, :] = slab.T.astype(o_ref.dtype)
        else:
            o_ref[:, h * LANE:(h + 1) * LANE] = slab.astype(o_ref.dtype)


def proj(x, norm_w, w, *, n_blocks, tn, col_map=None, x_col_block=0, rope=None, rope_mask=0, scale_mask=0, scale=1.0,
         sigmoid_from=None, transposed=False, out_dtype=F32, tm=1024):
    T = x.shape[0]
    K = w.shape[0]
    tm = _pick(T, tm)
    col_map = col_map or (lambda j: j)
    has_rope = rope is not None
    assert has_rope or scale_mask == 0
    in_specs = [
        pl.BlockSpec((tm, K), lambda i, j: (i, x_col_block)),
        pl.BlockSpec((1, K), lambda i, j: (0, 0)),
        pl.BlockSpec((K, tn), lambda i, j: (0, col_map(j))),
    ]
    args = [x, norm_w.reshape(1, K).astype(F32), w]
    if has_rope:
        in_specs += [pl.BlockSpec((tm, LANE), lambda i, j: (i, 0))] * 2
        args += list(rope)
    kern = functools.partial(_proj_kernel, has_rope=has_rope, rope_mask=rope_mask, scale_mask=scale_mask,
                             scale=scale, sigmoid_from=sigmoid_from, transposed=transposed)
    if transposed:
        out_spec = pl.BlockSpec((tn, tm), lambda i, j: (j, i))
        out_shape = jax.ShapeDtypeStruct((n_blocks * tn, T), out_dtype)
    else:
        out_spec = pl.BlockSpec((tm, tn), lambda i, j: (i, j))
        out_shape = jax.ShapeDtypeStruct((T, n_blocks * tn), out_dtype)
    return pl.pallas_call(
        kern,
        grid=(T // tm, n_blocks),
        in_specs=in_specs,
        out_specs=out_spec,
        out_shape=out_shape,
        scratch_shapes=[pltpu.VMEM((tm, K), BF16)],
        compiler_params=_params("parallel", "arbitrary"),
        name="proj",
    )(*args)


def _matmul_residual_kernel(*refs):
    *a_refs, w_ref, r_ref, o_ref = refs
    a = a_refs[0][...]
    if len(a_refs) > 1:
        a = a.astype(F32)
        for a_ref in a_refs[1:]:
            a = a + a_ref[...].astype(F32)
        a = a.astype(BF16)
    o_ref[...] = r_ref[...] + jnp.dot(a, w_ref[...], preferred_element_type=F32)


def matmul_residual(a_list, w, res, *, tm=1024, tn=1024):
    T, K = a_list[0].shape
    N = w.shape[1]
    tm, tn = _pick(T, tm), _pick(N, tn)
    return pl.pallas_call(
        _matmul_residual_kernel,
        grid=(T // tm, N // tn),
        in_specs=[pl.BlockSpec((tm, K), lambda i, j: (i, 0))] * len(a_list) + [
            pl.BlockSpec((K, tn), lambda i, j: (0, j)),
            pl.BlockSpec((tm, tn), lambda i, j: (i, j)),
        ],
        out_specs=pl.BlockSpec((tm, tn), lambda i, j: (i, j)),
        out_shape=jax.ShapeDtypeStruct((T, N), F32),
        compiler_params=_params("parallel", "arbitrary"),
        name="matmul_residual",
    )(*a_list, w, res)


def _ffn_kernel(x_ref, nw_ref, wg_ref, wu_ref, wo_ref, fw_ref, o_ref, xn_scr, *, final_norm):
    f = pl.program_id(1)
    tm = x_ref.shape[0]
    slab = min(FFN_SLAB, tm)

    def over_slabs(fn):
        def body(r, carry):
            fn(pl.ds(pl.multiple_of(r * slab, slab), slab))
            return carry

        lax.fori_loop(0, tm // slab, body, 0)

    @pl.when(f == 0)
    def _():
        def prologue(rows):
            xn_scr[rows, :] = _rms_rows(x_ref[rows, :], nw_ref[...]).astype(BF16)

        over_slabs(prologue)
        o_ref[...] = jnp.zeros_like(o_ref)

    xn = xn_scr[...]
    g = jnp.dot(xn, wg_ref[...], preferred_element_type=F32)
    u = jnp.dot(xn, wu_ref[...], preferred_element_type=F32)
    h = (g * jax.nn.sigmoid(g) * u).astype(BF16)
    o_ref[...] += jnp.dot(h, wo_ref[...], preferred_element_type=F32)

    @pl.when(f == pl.num_programs(1) - 1)
    def _():
        def epilogue(rows):
            y = x_ref[rows, :] + 0.5 * o_ref[rows, :]
            if final_norm:
                y = _rms_rows(y, fw_ref[...])
            o_ref[rows, :] = y

        over_slabs(epilogue)


def ffn_half_step(x, norm_w, w_in, w_out, final_w=None, *, tm=1024, tf=512):
    T, D = x.shape
    DFF = w_out.shape[0]
    tm, tf = _pick(T, tm), _pick(DFF, tf)
    nf = DFF // tf
    final_norm = final_w is not None
    fw = (final_w if final_norm else norm_w).reshape(1, D).astype(F32)
    return pl.pallas_call(
        functools.partial(_ffn_kernel, final_norm=final_norm),
        grid=(T // tm, nf),
        in_specs=[
            pl.BlockSpec((tm, D), lambda i, f: (i, 0)),
            pl.BlockSpec((1, D), lambda i, f: (0, 0)),
            pl.BlockSpec((D, tf), lambda i, f: (0, f)),
            pl.BlockSpec((D, tf), lambda i, f: (0, f + nf)),
            pl.BlockSpec((tf, D), lambda i, f: (f, 0)),
            pl.BlockSpec((1, D), lambda i, f: (0, 0)),
        ],
        out_specs=pl.BlockSpec((tm, D), lambda i, f: (i, 0)),
        out_shape=jax.ShapeDtypeStruct((T, D), F32),
        scratch_shapes=[pltpu.VMEM((tm, D), BF16)],
        compiler_params=_params("parallel", "arbitrary", vmem=FFN_VMEM_LIMIT),
        name="ffn_half_step",
    )(x, norm_w.reshape(1, D).astype(F32), w_in, w_in, w_out, fw)


def _flash_kernel(*refs, R, E, has_kx, gate_cols, tq, tk, mode, chunks_per_version, depth):
    it = iter(refs)
    qm_ref = next(it)
    qx_ref = next(it) if E else None
    km_ref = next(it)
    kx_ref = next(it) if has_kx else None
    v_ref = next(it)
    gate_ref = next(it) if gate_cols else None
    cap_ref = next(it)
    o_ref = next(it)
    q_scr, m_scr, acc_scr, s_scr = it
    M = R * tq
    i = pl.program_id(2)

    for e in range(max(E, 1)):
        for r in range(R):
            q_scr[e, r * tq:(r + 1) * tq, 0:LANE] = qm_ref[0, :, r * LANE:(r + 1) * LANE]
            if E:
                q_scr[e, r * tq:(r + 1) * tq, LANE:2 * LANE] = qx_ref[0, :, e * LANE:(e + 1) * LANE]
    m_scr[...] = jnp.full_like(m_scr, NEG_INF)
    acc_scr[...] = jnp.zeros_like(acc_scr)
    ones_col = jnp.where(lax.broadcasted_iota(jnp.int32, (tk, LANE), 1) == 0, 1.0, 0.0).astype(BF16)

    def scores(c):
        k0 = pl.multiple_of(c * tk, tk)
        kt = km_ref[:, pl.ds(k0, tk)]
        if has_kx:
            kt = jnp.concatenate([kt, kx_ref[:, pl.ds(k0, tk)]], axis=0)
        q = q_scr[c // chunks_per_version] if E > 1 else q_scr[0]
        return jnp.dot(q, kt, preferred_element_type=F32).astype(BF16)

    def softmax_pv(c, s, cap=None):
        k0 = pl.multiple_of(c * tk, tk)
        if cap is not None:
            s = jnp.minimum(s, cap)
        m_prev = m_scr[...]
        m_new = jnp.maximum(m_prev, jnp.max(s, axis=1, keepdims=True).astype(F32))
        alpha = jnp.exp2(m_prev - m_new)
        p = jnp.exp2(s - _repeat_lanes(m_new.astype(BF16), tk // LANE))
        v1 = jnp.concatenate([v_ref[0, pl.ds(k0, tk), :], ones_col], axis=1)
        acc_scr[...] = _repeat_lanes(alpha, 2) * acc_scr[...] + jnp.dot(p, v1, preferred_element_type=F32)
        m_scr[...] = m_new

    if mode == "causal":
        n_full = (i * tq) // tk
        bufs = [s_scr.at[u] for u in range(depth)]
        bufs[0][...] = scores(0)

        def group(j, carry):
            for u in range(depth):
                bufs[(u + 1) % depth][...] = scores(depth * j + u + 1)
                softmax_pv(depth * j + u, bufs[u][...])
            return carry

        n_groups = n_full // depth
        lax.fori_loop(0, n_groups, group, 0)

        def tail(c, carry):
            softmax_pv(c, bufs[0][...])
            bufs[0][...] = scores(c + 1)
            return carry

        lax.fori_loop(depth * n_groups, n_full, tail, 0)
        softmax_pv(n_full, bufs[0][...], cap_ref[0])
    else:
        prev = jnp.maximum(i - 1, 0)
        s_prev = scores(prev)
        s_diag = scores(i)
        softmax_pv(prev, s_prev, jnp.where(i > 0, cap_ref[1], NEG_INF))
        softmax_pv(i, s_diag, cap_ref[0])

    out = acc_scr[:, 0:LANE] / acc_scr[:, LANE:LANE + 1]
    for r in range(R):
        o_r = out[r * tq:(r + 1) * tq]
        if gate_cols:
            o_r = o_r * gate_ref[0, :, gate_cols[r]:gate_cols[r] + 1]
        o_ref[0, :, r * LANE:(r + 1) * LANE] = o_r.astype(o_ref.dtype)


def flash_attention(qm, qx, km, kx, v, gate=None, *, n_groups, R, E, qm_col, qx_col=None, km_col, v_col, kx_batched,
                    gate_cols=None, gate_block0=0, mode, window=0, tq, tk, depth=2, keys_per_version=None):
    B, S, _ = qm.shape
    tq, tk = _pick(S, tq), _pick(S, tk)
    assert tq == tk
    assert mode == "causal" or tq == window
    has_kx = kx is not None
    dk = 2 * LANE if has_kx else LANE
    cpv = (keys_per_version // tk) if keys_per_version else 1
    M = R * tq
    in_specs = [pl.BlockSpec((1, tq, R * LANE), lambda b, g, i: (b, i, qm_col(g)))]
    args = [qm]
    if E:
        in_specs.append(pl.BlockSpec((1, tq, E * LANE), lambda b, g, i: (b, i, qx_col(g))))
        args.append(qx)
    in_specs.append(pl.BlockSpec((LANE, S), lambda b, g, i: (km_col(g), b)))
    args.append(km)
    if has_kx:
        in_specs.append(pl.BlockSpec((LANE, S), lambda b, g, i: (0, b if kx_batched else 0)))
        args.append(kx)
    in_specs.append(pl.BlockSpec((1, S, LANE), lambda b, g, i: (b, 0, v_col(g))))
    args.append(v)
    if gate_cols:
        in_specs.append(pl.BlockSpec((1, tq, LANE), lambda b, g, i: (b, i, gate_block0 + g)))
        args.append(gate)
    q_off = lax.broadcasted_iota(jnp.int32, (M, tk), 0) % tq
    k_off = lax.broadcasted_iota(jnp.int32, (M, tk), 1)
    caps = [k_off <= q_off] + ([k_off > q_off] if mode == "window" else [])
    cap = jnp.stack([jnp.where(c, MASK_PASS, NEG_INF).astype(BF16) for c in caps])
    in_specs.append(pl.BlockSpec(cap.shape, lambda b, g, i: (0, 0, 0)))
    args.append(cap)
    kern = functools.partial(_flash_kernel, R=R, E=E, has_kx=has_kx, gate_cols=gate_cols, tq=tq, tk=tk, mode=mode,
                             chunks_per_version=cpv, depth=depth)
    return pl.pallas_call(
        kern,
        grid=(B, n_groups, S // tq),
        in_specs=in_specs,
        out_specs=pl.BlockSpec((1, tq, R * LANE), lambda b, g, i: (b, i, g)),
        out_shape=jax.ShapeDtypeStruct((B, S, n_groups * R * LANE), BF16),
        scratch_shapes=[
            pltpu.VMEM((max(E, 1), M, dk), BF16),
            pltpu.VMEM((M, LANE), F32),
            pltpu.VMEM((M, 2 * LANE), F32),
            pltpu.VMEM((depth, M, tk), BF16),
        ],
        compiler_params=_params("parallel", "parallel", "arbitrary"),
        name="flash_" + mode + ("_x%d" % E if E else ""),
    )(*args)


def _compress_kernel(tok_ref, pe_ref, w1_ref, w2_ref, c2_ref, s2_ref, o_ref, *, ncp):
    half = NSA_CMP_STRIDE
    a = b = None
    for l in range(half):
        x = tok_ref[0, pl.ds(l, ncp, stride=half), :]
        da = jnp.dot((x + pe_ref[0, l:l + 1, :]).astype(BF16), w1_ref[0, l], preferred_element_type=F32)
        db = jnp.dot((x + pe_ref[0, half + l:half + l + 1, :]).astype(BF16), w1_ref[0, half + l],
                     preferred_element_type=F32)
        a = da if a is None else a + da
        b = db if b is None else b + db
    h = jax.nn.gelu(a + pltpu.roll(b, ncp - 1, axis=0))
    y = jnp.dot(h.astype(BF16), w2_ref[0], preferred_element_type=F32)
    is_key = pl.program_id(0) == 0
    c2 = jnp.where(is_key, c2_ref[0], 1.0)
    s2 = jnp.where(is_key, s2_ref[0], 0.0)
    o_ref[0, 0, 0] = _rotate_half(y, c2, s2).astype(o_ref.dtype)


def compress_tokens(tok, pe, w1, w2, c2, s2):
    B, S, _ = tok.shape
    G, dk = NSA_KV_GROUPS, NSA_HEAD_DIM
    ncp = S // NSA_CMP_STRIDE
    hidden = w2.shape[1]
    return pl.pallas_call(
        functools.partial(_compress_kernel, ncp=ncp),
        grid=(2, B, G),
        in_specs=[
            pl.BlockSpec((1, S, dk), lambda s, b, g: (b, 0, s * G + g)),
            pl.BlockSpec((1, NSA_CMP_LEN, dk), lambda s, b, g: (s, 0, 0)),
            pl.BlockSpec((1, NSA_CMP_LEN, dk, hidden), lambda s, b, g: (s, 0, 0, 0)),
            pl.BlockSpec((1, hidden, dk), lambda s, b, g: (s, 0, 0)),
            pl.BlockSpec((1, ncp, dk), lambda s, b, g: (b, 0, 0)),
            pl.BlockSpec((1, ncp, dk), lambda s, b, g: (b, 0, 0)),
        ],
        out_specs=pl.BlockSpec((1, 1, 1, ncp, dk), lambda s, b, g: (s, b, g, 0, 0)),
        out_shape=jax.ShapeDtypeStruct((2, B, G, ncp, dk), BF16),
        compiler_params=_params("parallel", "parallel", "parallel"),
        name="compress_tokens",
    )(tok, pe, w1, w2, c2, s2)


def _cmp_select_kernel(q_ref, kc_ref, vc_ref, ovt_ref, gate_ref, oc_all_ref, nm_all_ref, oc_ref, nm_ref, *,
                       R, tq, n_sub, tile0, n_cmp, n_top):
    del oc_all_ref, nm_all_ref
    for u in range(n_sub):
        rows = slice(u * tq, (u + 1) * tq)
        t0 = (tile0 + pl.program_id(2) * n_sub + u) * tq
        _cmp_select_tile(q_ref.at[0, rows], kc_ref, vc_ref, ovt_ref, gate_ref.at[0, rows], oc_ref.at[0, rows],
                         nm_ref.at[0, rows], t0, R=R, tq=tq, n_cmp=n_cmp, n_top=n_top)


def _cmp_select_tile(q_ref, kc_ref, vc_ref, ovt_ref, gate_ref, oc_ref, nm_ref, t0, *, R, tq, n_cmp, n_top):
    M = R * tq
    q = jnp.concatenate([q_ref[:, r * LANE:(r + 1) * LANE] for r in range(R)], axis=0)
    kc = kc_ref[0, 0, 0]
    ncp = kc.shape[0]
    s = lax.dot_general(q, kc, (((1,), (1,)), ((), ())), preferred_element_type=F32)
    t = t0 + _mod_pow2(lax.broadcasted_iota(jnp.int32, (M, 1), 0), tq)
    lim = jnp.minimum(_div_pow2(t - (NSA_CMP_LEN - 1), NSA_CMP_STRIDE) + 1, n_cmp)
    valid = lax.broadcasted_iota(jnp.int32, (M, ncp), 1) < lim
    sm = jnp.where(valid, s, NEG_INF)
    e = jnp.exp2((sm - jnp.max(sm, axis=1, keepdims=True)).astype(BF16))
    ones_col = jnp.where(lax.broadcasted_iota(jnp.int32, (ncp, LANE), 1) == 0, 1.0, 0.0).astype(BF16)
    oc = jnp.dot(e, jnp.concatenate([vc_ref[0, 0, 0], ones_col], axis=1), preferred_element_type=F32)
    oc = oc[:, 0:LANE] * jnp.where(lim > 0, 1.0 / oc[:, LANE:LANE + 1], 0.0)
    for r in range(R):
        gate = gate_ref[:, NSA_N_BRANCHES * r:NSA_N_BRANCHES * r + 1]
        oc_ref[:, r * LANE:(r + 1) * LANE] = (oc[r * tq:(r + 1) * tq] * gate).astype(oc_ref.dtype)

    nb = ovt_ref.shape[0] - BF16_ROWS
    imp_un = lax.dot_general(ovt_ref[...], e, (((1,), (1,)), ((), ())), preferred_element_type=F32)
    t_l = t0 + _mod_pow2(lax.broadcasted_iota(jnp.int32, (1, M), 1), tq)
    lim_l = jnp.minimum(_div_pow2(t_l - (NSA_CMP_LEN - 1), NSA_CMP_STRIDE) + 1, n_cmp)
    imp_h = imp_un[0:nb] * jnp.where(lim_l > 0, 1.0 / imp_un[nb:nb + 1], 0.0)
    imp = imp_h[:, 0:tq]
    for r in range(1, R):
        imp = imp + imp_h[:, r * tq:(r + 1) * tq]
    j = lax.broadcasted_iota(jnp.int32, (nb, tq), 0)
    jf = j.astype(F32)
    cur = _div_pow2(t0 + lax.broadcasted_iota(jnp.int32, (nb, tq), 1), NSA_SEL_BLOCK)
    forced = jnp.logical_or(j == 0, jnp.logical_or(j == cur, j == cur - 1))
    x = jnp.where(j > cur, NEG_INF, imp + FORCE_BONUS * forced.astype(F32))
    for _ in range(n_top):
        mx = jnp.max(x, axis=0, keepdims=True)
        first = jnp.min(jnp.where(x == mx, jf, float(nb)), axis=0, keepdims=True)
        x = jnp.where(jf == first, jnp.where(mx > 0.5 * NEG_INF, -jnp.inf, NEG_INF), x)
    nm_ref[:, 0:nb] = jnp.where(x == -jnp.inf, 0.0, NEG_INF).T.astype(nm_ref.dtype)
    if nb < nm_ref.shape[1]:
        nm_ref[:, nb:] = jnp.full((tq, nm_ref.shape[1] - nb), NEG_INF, nm_ref.dtype)


def cmp_attention_select(q, kv_cmp, overlap, gate, *, gate_block0, n_cmp, n_top, tq=128, n_sub=4):
    B, S, _ = q.shape
    G, R = NSA_KV_GROUPS, NSA_R
    NB, NCP = overlap.shape
    tq = _pick(S, tq)
    tb = n_sub * tq
    n_parts = next(n for n in (4, 2, 1) if NCP % (n * LANE) == 0 and S % (n * tb) == 0)
    steps = S // (n_parts * tb)
    outs = [jnp.zeros((B, S, G * R * LANE), BF16), jnp.zeros((B, S, G * NB), BF16)]
    for part in range(n_parts):
        ncp_w = NCP * (part + 1) // n_parts
        nb_w = min(NB, -(-(S * (part + 1) // n_parts // NSA_SEL_BLOCK) // LANE) * LANE)
        ov = jnp.concatenate([overlap[:nb_w, :ncp_w], jnp.ones((1, ncp_w), BF16),
                              jnp.zeros((BF16_ROWS - 1, ncp_w), BF16)], axis=0)
        kern = functools.partial(_cmp_select_kernel, R=R, tq=tq, n_sub=n_sub, tile0=part * steps * n_sub,
                                 n_cmp=n_cmp, n_top=n_top)
        off = part * steps
        carried = list(outs)
        outs = pl.pallas_call(
            kern,
            grid=(B, G, steps),
            in_specs=[
                pl.BlockSpec((1, tb, R * LANE), lambda b, g, i, off=off: (b, i + off, g)),
                pl.BlockSpec((1, 1, 1, ncp_w, LANE), lambda b, g, i: (0, b, g, 0, 0)),
                pl.BlockSpec((1, 1, 1, ncp_w, LANE), lambda b, g, i: (1, b, g, 0, 0)),
                pl.BlockSpec((nb_w + BF16_ROWS, ncp_w), lambda b, g, i: (0, 0)),
                pl.BlockSpec((1, tb, LANE), lambda b, g, i, off=off: (b, i + off, gate_block0 + g)),
            ] + [pl.BlockSpec(memory_space=pl.ANY)] * len(carried),
            out_specs=[
                pl.BlockSpec((1, tb, R * LANE), lambda b, g, i, off=off: (b, i + off, g)),
                pl.BlockSpec((1, tb, NB), lambda b, g, i, off=off: (b, i + off, g)),
            ],
            out_shape=[
                jax.ShapeDtypeStruct((B, S, G * R * LANE), BF16),
                jax.ShapeDtypeStruct((B, S, G * NB), BF16),
            ],
            input_output_aliases={5 + k: k for k in range(len(carried))},
            compiler_params=_params("parallel", "parallel", "arbitrary"),
            name="cmp_attention_select",
        )(q, kv_cmp, kv_cmp, ov, gate, *carried)
    return outs


def _rope_tables(pos, dim):
    inv = 1.0 / (ROPE_THETA ** (jnp.arange(0, dim, 2, dtype=F32) / dim))
    ang = pos.astype(F32)[..., None] * inv
    pad = [(0, 0)] * (ang.ndim - 1) + [(0, (LANE - dim) // 2)]
    c, s = jnp.pad(jnp.cos(ang), pad), jnp.pad(jnp.sin(ang), pad)
    return jnp.concatenate([c, c], axis=-1), jnp.concatenate([-s, s], axis=-1)


def _spread_rope_cols(w):
    dim = w.shape[-1]
    pad = [(0, 0)] * (w.ndim - 1) + [(0, (LANE - dim) // 2)]
    return jnp.concatenate([jnp.pad(w[..., :dim // 2], pad), jnp.pad(w[..., dim // 2:], pad)], axis=-1)


def nsa_mixer(x, B, S, positions, norm_w, w_in, cmp_pe, cmp_w1, cmp_w2, w_out):
    T, D = x.shape
    H, G, R, dk = NSA_HEADS, NSA_KV_GROUPS, NSA_R, NSA_HEAD_DIM
    q_cols, kv_cols = H * dk, NSA_N_BRANCHES * 2 * G * dk
    grp = G * dk
    q_scale = dk ** -0.5 * LOG2E
    tables = _rope_tables(positions.reshape(T), dk)

    w_q = w_in[:, :q_cols]
    w_kv = w_in[:, q_cols:q_cols + kv_cols].reshape(D, NSA_N_BRANCHES, 2, grp)
    w_gate = w_in[:, q_cols + kv_cols:].reshape(D, G, R * NSA_N_BRANCHES)
    w_gate = jnp.pad(w_gate, ((0, 0), (0, 0), (0, LANE - R * NSA_N_BRANCHES))).reshape(D, G * LANE)
    tn = 2 * grp
    nq = q_cols // tn
    qv = proj(x, norm_w, jnp.concatenate([w_q, w_kv[:, 1, 1], w_kv[:, 2, 1]], axis=1).astype(BF16), n_blocks=nq + 1,
              tn=tn, rope=tables, rope_mask=(1 << nq) - 1, scale_mask=(1 << nq) - 1, scale=q_scale,
              out_dtype=BF16).reshape(B, S, q_cols + 2 * grp)
    k_t = proj(x, norm_w, jnp.concatenate([w_kv[:, 1, 0], w_kv[:, 2, 0]], axis=1).astype(BF16), n_blocks=1, tn=tn,
               rope=tables, rope_mask=1, transposed=True, out_dtype=BF16)
    w_cg = jnp.concatenate([w_kv[:, 0, 0], w_kv[:, 0, 1], w_gate], axis=1).astype(BF16)
    cmp_gates = proj(x, norm_w, w_cg, n_blocks=1, tn=w_cg.shape[1], sigmoid_from=2 * grp // LANE)
    cmp_gates = cmp_gates.reshape(B, S, w_cg.shape[1])
    gate0 = 2 * grp // LANE

    ncp = S // NSA_CMP_STRIDE
    n_cmp = (S - NSA_CMP_LEN) // NSA_CMP_STRIDE + 1
    cmp_end = jnp.minimum(jnp.arange(ncp) * NSA_CMP_STRIDE + NSA_CMP_LEN - 1, S - 1)
    c2c, s2c = _rope_tables(positions[:, cmp_end], dk)
    hidden = cmp_w1.shape[-1]
    kv_cmp = compress_tokens(cmp_gates, cmp_pe, cmp_w1.reshape(2, NSA_CMP_LEN, dk, hidden).astype(BF16),
                             cmp_w2.astype(BF16), c2c, s2c)

    n_sel = S // NSA_SEL_BLOCK
    nb = -(-n_sel // LANE) * LANE
    blk_start = jnp.arange(ncp) * NSA_CMP_STRIDE
    sel_start = jnp.arange(nb) * NSA_SEL_BLOCK
    overlap = ((blk_start[None, :] <= sel_start[:, None] + NSA_SEL_BLOCK - 1)
               & (blk_start[None, :] + NSA_CMP_LEN - 1 >= sel_start[:, None])
               & (jnp.arange(ncp)[None, :] < n_cmp) & (jnp.arange(nb)[:, None] < n_sel)).astype(BF16)
    o_cmp, neg_mask = cmp_attention_select(qv, kv_cmp, overlap, cmp_gates, gate_block0=gate0, n_cmp=n_cmp,
                                           n_top=min(NSA_N_SELECT, n_sel))

    blk_onehot = (jnp.arange(LANE)[:, None] == jnp.arange(S)[None, :] // NSA_SEL_BLOCK % LANE).astype(BF16)
    v0 = q_cols // dk
    o_slc = flash_attention(qv, neg_mask, k_t, blk_onehot, qv, cmp_gates, n_groups=G, R=R, E=nb // LANE,
                            qm_col=lambda g: g, qx_col=lambda g: g, km_col=lambda g: g,
                            v_col=lambda g: v0 + g, kx_batched=False,
                            gate_cols=tuple(NSA_N_BRANCHES * r + 1 for r in range(R)), gate_block0=gate0,
                            mode="causal", tq=512, tk=512, depth=4, keys_per_version=LANE * NSA_SEL_BLOCK)
    o_win = flash_attention(qv, None, k_t, None, qv, cmp_gates, n_groups=G, R=R, E=0,
                            qm_col=lambda g: g, km_col=lambda g: G + g, v_col=lambda g: v0 + G + g,
                            kx_batched=False, gate_cols=tuple(NSA_N_BRANCHES * r + 2 for r in range(R)),
                            gate_block0=gate0, mode="window", window=NSA_WINDOW, tq=512, tk=512)
    return matmul_residual([o.reshape(T, H * dk) for o in (o_cmp, o_slc, o_win)], w_out.astype(BF16), x)


def mla_mixer(x, B, S, positions, norm_w, w_in, q_norm_w, kv_norm_w, w_uq, w_ukv, w_out):
    T, D = x.shape
    H = MLA_HEADS
    q_rank, kv_rank = q_norm_w.shape[0], kv_norm_w.shape[0]
    assert q_rank == kv_rank and q_rank % LANE == 0
    q_scale = MLA_QK_DIM ** -0.5 * LOG2E
    tables = _rope_tables(positions.reshape(T), MLA_ROPE_DIM)

    latent = proj(x, norm_w, w_in[:, :2 * q_rank].astype(BF16), n_blocks=1, tn=2 * q_rank)
    w_kr = _spread_rope_cols(w_in[:, 2 * q_rank:]).astype(BF16)
    k_rope_t = proj(x, norm_w, w_kr, n_blocks=1, tn=LANE, rope=tables, rope_mask=1, transposed=True, out_dtype=BF16)
    w_uq3 = w_uq.reshape(q_rank, H, MLA_QK_DIM)
    w_uq_p = jnp.concatenate([w_uq3[:, :, :MLA_NOPE_DIM].reshape(q_rank, H * MLA_NOPE_DIM),
                              _spread_rope_cols(w_uq3[:, :, MLA_NOPE_DIM:]).reshape(q_rank, H * LANE)],
                             axis=1).astype(BF16)
    tn = H * LANE
    n_nope = H * MLA_NOPE_DIM // tn
    n_all = n_nope + H * LANE // tn
    q = proj(latent, q_norm_w, w_uq_p, n_blocks=n_all, tn=tn, x_col_block=0, rope=tables,
             rope_mask=(1 << n_all) - (1 << n_nope), scale_mask=(1 << n_all) - 1, scale=q_scale,
             out_dtype=BF16).reshape(B, S, n_all * tn)
    w_ukv3 = w_ukv.reshape(kv_rank, H, MLA_NOPE_DIM + MLA_V_DIM)
    w_uk = w_ukv3[:, :, :MLA_NOPE_DIM].reshape(kv_rank, H * MLA_NOPE_DIM).astype(BF16)
    w_uv = w_ukv3[:, :, MLA_NOPE_DIM:].reshape(kv_rank, H * MLA_V_DIM).astype(BF16)
    k_nope_t = proj(latent, kv_norm_w, w_uk, n_blocks=H * MLA_NOPE_DIM // tn, tn=tn, x_col_block=1, transposed=True,
                    out_dtype=BF16)
    v = proj(latent, kv_norm_w, w_uv, n_blocks=H * MLA_V_DIM // tn, tn=tn, x_col_block=1,
             out_dtype=BF16).reshape(B, S, H * MLA_V_DIM)
    o = flash_attention(q, q, k_nope_t, k_rope_t, v, n_groups=H, R=1, E=1,
                        qm_col=lambda h: h, qx_col=lambda h: H + h, km_col=lambda h: h, v_col=lambda h: h,
                        kx_batched=True, mode="causal", tq=1024, tk=1024, depth=4)
    return matmul_residual([o.reshape(T, H * MLA_V_DIM)], w_out.astype(BF16), x)


def kernel(x, positions, ffn_norm_w, ffn_w_in, ffn_w_out, mix_norm_w, nsa_w_in, nsa_cmp_pe, nsa_cmp_w1, nsa_cmp_w2,
           nsa_w_out, mla_w_in, mla_q_norm_w, mla_kv_norm_w, mla_w_uq, mla_w_ukv, mla_w_out, final_norm_w):
    B, S, D = x.shape
    depth = ffn_norm_w.shape[0]
    n_mixers = 2
    h = x.reshape(B * S, D)
    for i in range(depth):
        h = ffn_half_step(h, ffn_norm_w[i, 0], ffn_w_in[i, 0].astype(BF16), ffn_w_out[i, 0].astype(BF16))
        j = i // n_mixers
        if i % n_mixers == 0:
            h = nsa_mixer(h, B, S, positions, mix_norm_w[i], nsa_w_in[j], nsa_cmp_pe[j], nsa_cmp_w1[j],
                          nsa_cmp_w2[j], nsa_w_out[j])
        else:
            h = mla_mixer(h, B, S, positions, mix_norm_w[i], mla_w_in[j], mla_q_norm_w[j], mla_kv_norm_w[j],
                          mla_w_uq[j], mla_w_ukv[j], mla_w_out[j])
        last = i == depth - 1
        h = ffn_half_step(h, ffn_norm_w[i, 1], ffn_w_in[i, 1].astype(BF16), ffn_w_out[i, 1].astype(BF16),
                          final_w=final_norm_w if last else None)
    return h.reshape(B, S, D)
```

```python
import functools
import math

import jax
import jax.numpy as jnp
from jax import lax
from jax.experimental import pallas as pl
from jax.experimental.pallas import tpu as pltpu

F32 = jnp.float32
BF16 = jnp.bfloat16

LANE = 128
BF16_ROWS = 16
RMS_EPS = 1e-6
ROPE_THETA = 10000.0
NEG_INF = -1e30
MASK_PASS = 3e38
FORCE_BONUS = 1e9
LOG2E = math.log2(math.e)
VMEM_LIMIT = 56 * 1024 * 1024
FFN_VMEM_LIMIT = 61 * 1024 * 1024
ROW_TILE = 1024
FFN_HIDDEN_TILE = 512
MLA_FLASH_TILE = 1024
NSA_FLASH_TILE = 512
FLASH_DEPTH = 4
CMP_TILE = 128
CMP_SUBTILES = 4
FFN_SLAB = 64

NSA_HEADS = 16
NSA_HEAD_DIM = 128
NSA_KV_GROUPS = 4
NSA_R = NSA_HEADS // NSA_KV_GROUPS
NSA_N_BRANCHES = 3
NSA_CMP_LEN = 32
NSA_CMP_STRIDE = 16
NSA_SEL_BLOCK = 64
NSA_N_SELECT = 16
NSA_WINDOW = 512

MLA_HEADS = 16
MLA_NOPE_DIM = 128
MLA_ROPE_DIM = 64
MLA_V_DIM = 128
MLA_QK_DIM = MLA_NOPE_DIM + MLA_ROPE_DIM


def _pick(n, pref):
    if n <= pref:
        return n
    t = (pref // LANE) * LANE
    while t > LANE and n % t:
        t -= LANE
    assert n % t == 0, (n, pref)
    return t


def _div_pow2(x, d):
    assert d & (d - 1) == 0, d
    return jnp.right_shift(x, d.bit_length() - 1)


def _mod_pow2(x, d):
    assert d & (d - 1) == 0, d
    return jnp.bitwise_and(x, d - 1)


def _params(*sem, vmem=VMEM_LIMIT):
    return pltpu.CompilerParams(dimension_semantics=sem, vmem_limit_bytes=vmem)


def _repeat_lanes(x, n):
    return jnp.concatenate([x] * n, axis=1)


def _rotate_half(y, c2, s2):
    return y * c2 + pltpu.roll(y, LANE // 2, axis=1) * s2


def _rms_rows(x, w):
    ms = jnp.mean(x * x, axis=-1, keepdims=True)
    return x * lax.rsqrt(ms + RMS_EPS) * w


def _proj_kernel(*refs, has_rope, rope_mask, scale_mask, scale, sigmoid_from, transposed):
    if has_rope:
        x_ref, nw_ref, w_ref, c2_ref, s2_ref, o_ref, xn_scr = refs
    else:
        x_ref, nw_ref, w_ref, o_ref, xn_scr = refs
    j = pl.program_id(1)

    @pl.when(j == 0)
    def _():
        xn_scr[...] = _rms_rows(x_ref[...], nw_ref[...]).astype(BF16)

    y = jnp.dot(xn_scr[...], w_ref[...], preferred_element_type=F32)
    if has_rope:
        rope_on = jnp.bitwise_and(lax.shift_right_logical(jnp.int32(rope_mask), j), 1) == 1
        factor = jnp.where(jnp.bitwise_and(lax.shift_right_logical(jnp.int32(scale_mask), j), 1) == 1, scale, 1.0)
        c2 = jnp.where(rope_on, c2_ref[...], 1.0) * factor
        s2 = jnp.where(rope_on, s2_ref[...], 0.0) * factor
    for h in range(y.shape[1] // LANE):
        slab = y[:, h * LANE:(h + 1) * LANE]
        if has_rope:
            slab = _rotate_half(slab, c2, s2)
        if sigmoid_from is not None and h >= sigmoid_from:
            slab = jax.nn.sigmoid(slab)
        if transposed:
            o_ref[h * LANE:(h + 1) * LANE, :] = slab.T.astype(o_ref.dtype)
        else:
            o_ref[:, h * LANE:(h + 1) * LANE] = slab.astype(o_ref.dtype)


def proj(x, norm_w, w, *, n_blocks, tn, col_map=None, x_col_block=0, rope=None, rope_mask=0, scale_mask=0, scale=1.0,
         sigmoid_from=None, transposed=False, out_dtype=F32, tm=ROW_TILE):
    T = x.shape[0]
    K = w.shape[0]
    tm = _pick(T, tm)
    col_map = col_map or (lambda j: j)
    has_rope = rope is not None
    assert has_rope or scale_mask == 0
    in_specs = [
        pl.BlockSpec((tm, K), lambda i, j: (i, x_col_block)),
        pl.BlockSpec((1, K), lambda i, j: (0, 0)),
        pl.BlockSpec((K, tn), lambda i, j: (0, col_map(j))),
    ]
    args = [x, norm_w.reshape(1, K).astype(F32), w]
    if has_rope:
        in_specs += [pl.BlockSpec((tm, LANE), lambda i, j: (i, 0))] * 2
        args += list(rope)
    kern = functools.partial(_proj_kernel, has_rope=has_rope, rope_mask=rope_mask, scale_mask=scale_mask,
                             scale=scale, sigmoid_from=sigmoid_from, transposed=transposed)
    if transposed:
        out_spec = pl.BlockSpec((tn, tm), lambda i, j: (j, i))
        out_shape = jax.ShapeDtypeStruct((n_blocks * tn, T), out_dtype)
    else:
        out_spec = pl.BlockSpec((tm, tn), lambda i, j: (i, j))
        out_shape = jax.ShapeDtypeStruct((T, n_blocks * tn), out_dtype)
    return pl.pallas_call(
        kern,
        grid=(T // tm, n_blocks),
        in_specs=in_specs,
        out_specs=out_spec,
        out_shape=out_shape,
        scratch_shapes=[pltpu.VMEM((tm, K), BF16)],
        compiler_params=_params("parallel", "arbitrary"),
        name="proj",
    )(*args)


def _matmul_residual_kernel(*refs):
    *a_refs, w_ref, r_ref, o_ref = refs
    a = a_refs[0][...]
    if len(a_refs) > 1:
        a = a.astype(F32)
        for a_ref in a_refs[1:]:
            a = a + a_ref[...].astype(F32)
        a = a.astype(BF16)
    o_ref[...] = r_ref[...] + jnp.dot(a, w_ref[...], preferred_element_type=F32)


def matmul_residual(a_list, w, res, *, tm=ROW_TILE, tn=ROW_TILE):
    T, K = a_list[0].shape
    N = w.shape[1]
    tm, tn = _pick(T, tm), _pick(N, tn)
    return pl.pallas_call(
        _matmul_residual_kernel,
        grid=(T // tm, N // tn),
        in_specs=[pl.BlockSpec((tm, K), lambda i, j: (i, 0))] * len(a_list) + [
            pl.BlockSpec((K, tn), lambda i, j: (0, j)),
            pl.BlockSpec((tm, tn), lambda i, j: (i, j)),
        ],
        out_specs=pl.BlockSpec((tm, tn), lambda i, j: (i, j)),
        out_shape=jax.ShapeDtypeStruct((T, N), F32),
        compiler_params=_params("parallel", "arbitrary"),
        name="matmul_residual",
    )(*a_list, w, res)


def _ffn_kernel(x_ref, nw_ref, wg_ref, wu_ref, wo_ref, fw_ref, o_ref, xn_scr, *, final_norm):
    f = pl.program_id(1)
    tm = x_ref.shape[0]
    slab = min(FFN_SLAB, tm)

    def over_slabs(fn):
        def body(r, carry):
            fn(pl.ds(pl.multiple_of(r * slab, slab), slab))
            return carry

        lax.fori_loop(0, tm // slab, body, 0)

    @pl.when(f == 0)
    def _():
        def prologue(rows):
            xn_scr[rows, :] = _rms_rows(x_ref[rows, :], nw_ref[...]).astype(BF16)

        over_slabs(prologue)
        o_ref[...] = jnp.zeros_like(o_ref)

    xn = xn_scr[...]
    g = jnp.dot(xn, wg_ref[...], preferred_element_type=F32)
    u = jnp.dot(xn, wu_ref[...], preferred_element_type=F32)
    h = (g * jax.nn.sigmoid(g) * u).astype(BF16)
    o_ref[...] += jnp.dot(h, wo_ref[...], preferred_element_type=F32)

    @pl.when(f == pl.num_programs(1) - 1)
    def _():
        def epilogue(rows):
            y = x_ref[rows, :] + 0.5 * o_ref[rows, :]
            if final_norm:
                y = _rms_rows(y, fw_ref[...])
            o_ref[rows, :] = y

        over_slabs(epilogue)


def ffn_half_step(x, norm_w, w_in, w_out, final_w=None, *, tm=ROW_TILE, tf=FFN_HIDDEN_TILE):
    T, D = x.shape
    DFF = w_out.shape[0]
    tm, tf = _pick(T, tm), _pick(DFF, tf)
    nf = DFF // tf
    final_norm = final_w is not None
    fw = (final_w if final_norm else norm_w).reshape(1, D).astype(F32)
    return pl.pallas_call(
        functools.partial(_ffn_kernel, final_norm=final_norm),
        grid=(T // tm, nf),
        in_specs=[
            pl.BlockSpec((tm, D), lambda i, f: (i, 0)),
            pl.BlockSpec((1, D), lambda i, f: (0, 0)),
            pl.BlockSpec((D, tf), lambda i, f: (0, f)),
            pl.BlockSpec((D, tf), lambda i, f: (0, f + nf)),
            pl.BlockSpec((tf, D), lambda i, f: (f, 0)),
            pl.BlockSpec((1, D), lambda i, f: (0, 0)),
        ],
        out_specs=pl.BlockSpec((tm, D), lambda i, f: (i, 0)),
        out_shape=jax.ShapeDtypeStruct((T, D), F32),
        scratch_shapes=[pltpu.VMEM((tm, D), BF16)],
        compiler_params=_params("parallel", "arbitrary", vmem=FFN_VMEM_LIMIT),
        name="ffn_half_step",
    )(x, norm_w.reshape(1, D).astype(F32), w_in, w_in, w_out, fw)


def _flash_kernel(*refs, R, E, has_kx, gate_cols, tq, tk, mode, chunks_per_version, depth):
    it = iter(refs)
    qm_ref = next(it)
    qx_ref = next(it) if E else None
    km_ref = next(it)
    kx_ref = next(it) if has_kx else None
    v_ref = next(it)
    gate_ref = next(it) if gate_cols else None
    cap_ref = next(it)
    o_ref = next(it)
    q_scr, m_scr, acc_scr, s_scr = it
    M = R * tq
    i = pl.program_id(2)

    for e in range(max(E, 1)):
        for r in range(R):
            q_scr[e, r * tq:(r + 1) * tq, 0:LANE] = qm_ref[0, :, r * LANE:(r + 1) * LANE]
            if E:
                q_scr[e, r * tq:(r + 1) * tq, LANE:2 * LANE] = qx_ref[0, :, e * LANE:(e + 1) * LANE]
    m_scr[...] = jnp.full_like(m_scr, NEG_INF)
    acc_scr[...] = jnp.zeros_like(acc_scr)
    ones_col = jnp.where(lax.broadcasted_iota(jnp.int32, (tk, LANE), 1) == 0, 1.0, 0.0).astype(BF16)

    def scores(c):
        k0 = pl.multiple_of(c * tk, tk)
        kt = km_ref[:, pl.ds(k0, tk)]
        if has_kx:
            kt = jnp.concatenate([kt, kx_ref[:, pl.ds(k0, tk)]], axis=0)
        q = q_scr[c // chunks_per_version] if E > 1 else q_scr[0]
        return jnp.dot(q, kt, preferred_element_type=F32).astype(BF16)

    def softmax_pv(c, s, cap=None):
        k0 = pl.multiple_of(c * tk, tk)
        if cap is not None:
            s = jnp.minimum(s, cap)
        m_prev = m_scr[...]
        m_new = jnp.maximum(m_prev, jnp.max(s, axis=1, keepdims=True).astype(F32))
        alpha = jnp.exp2(m_prev - m_new)
        p = jnp.exp2(s - _repeat_lanes(m_new.astype(BF16), tk // LANE))
        v1 = jnp.concatenate([v_ref[0, pl.ds(k0, tk), :], ones_col], axis=1)
        acc_scr[...] = _repeat_lanes(alpha, 2) * acc_scr[...] + jnp.dot(p, v1, preferred_element_type=F32)
        m_scr[...] = m_new

    if mode == "causal":
        n_full = (i * tq) // tk
        bufs = [s_scr.at[u] for u in range(depth)]
        bufs[0][...] = scores(0)

        def group(j, carry):
            for u in range(depth):
                bufs[(u + 1) % depth][...] = scores(depth * j + u + 1)
                softmax_pv(depth * j + u, bufs[u][...])
            return carry

        n_groups = n_full // depth
        lax.fori_loop(0, n_groups, group, 0)

        def tail(c, carry):
            softmax_pv(c, bufs[0][...])
            bufs[0][...] = scores(c + 1)
            return carry

        lax.fori_loop(depth * n_groups, n_full, tail, 0)
        softmax_pv(n_full, bufs[0][...], cap_ref[0])
    else:
        prev = jnp.maximum(i - 1, 0)
        s_prev = scores(prev)
        s_diag = scores(i)
        softmax_pv(prev, s_prev, jnp.where(i > 0, cap_ref[1], NEG_INF))
        softmax_pv(i, s_diag, cap_ref[0])

    out = acc_scr[:, 0:LANE] / acc_scr[:, LANE:LANE + 1]
    for r in range(R):
        o_r = out[r * tq:(r + 1) * tq]
        if gate_cols:
            o_r = o_r * gate_ref[0, :, gate_cols[r]:gate_cols[r] + 1]
        o_ref[0, :, r * LANE:(r + 1) * LANE] = o_r.astype(o_ref.dtype)


def flash_attention(qm, qx, km, kx, v, gate=None, *, n_groups, R, E, qm_col, qx_col=None, km_col, v_col, kx_batched,
                    gate_cols=None, gate_block0=0, mode, window=0, tile, depth=FLASH_DEPTH, keys_per_version=None):
    B, S, _ = qm.shape
    tq = tk = _pick(S, tile)
    assert mode == "causal" or tq == window
    has_kx = kx is not None
    dk = 2 * LANE if has_kx else LANE
    cpv = (keys_per_version // tk) if keys_per_version else 1
    M = R * tq
    in_specs = [pl.BlockSpec((1, tq, R * LANE), lambda b, g, i: (b, i, qm_col(g)))]
    args = [qm]
    if E:
        in_specs.append(pl.BlockSpec((1, tq, E * LANE), lambda b, g, i: (b, i, qx_col(g))))
        args.append(qx)
    in_specs.append(pl.BlockSpec((LANE, S), lambda b, g, i: (km_col(g), b)))
    args.append(km)
    if has_kx:
        in_specs.append(pl.BlockSpec((LANE, S), lambda b, g, i: (0, b if kx_batched else 0)))
        args.append(kx)
    in_specs.append(pl.BlockSpec((1, S, LANE), lambda b, g, i: (b, 0, v_col(g))))
    args.append(v)
    if gate_cols:
        in_specs.append(pl.BlockSpec((1, tq, LANE), lambda b, g, i: (b, i, gate_block0 + g)))
        args.append(gate)
    q_off = lax.broadcasted_iota(jnp.int32, (M, tk), 0) % tq
    k_off = lax.broadcasted_iota(jnp.int32, (M, tk), 1)
    caps = [k_off <= q_off] + ([k_off > q_off] if mode == "window" else [])
    cap = jnp.stack([jnp.where(c, MASK_PASS, NEG_INF).astype(BF16) for c in caps])
    in_specs.append(pl.BlockSpec(cap.shape, lambda b, g, i: (0, 0, 0)))
    args.append(cap)
    kern = functools.partial(_flash_kernel, R=R, E=E, has_kx=has_kx, gate_cols=gate_cols, tq=tq, tk=tk, mode=mode,
                             chunks_per_version=cpv, depth=depth)
    return pl.pallas_call(
        kern,
        grid=(B, n_groups, S // tq),
        in_specs=in_specs,
        out_specs=pl.BlockSpec((1, tq, R * LANE), lambda b, g, i: (b, i, g)),
        out_shape=jax.ShapeDtypeStruct((B, S, n_groups * R * LANE), BF16),
        scratch_shapes=[
            pltpu.VMEM((max(E, 1), M, dk), BF16),
            pltpu.VMEM((M, LANE), F32),
            pltpu.VMEM((M, 2 * LANE), F32),
            pltpu.VMEM((depth, M, tk), BF16),
        ],
        compiler_params=_params("parallel", "parallel", "arbitrary"),
        name="flash_" + mode + ("_x%d" % E if E else ""),
    )(*args)


def _compress_kernel(tok_ref, pe_ref, w1_ref, w2_ref, c2_ref, s2_ref, o_ref, *, ncp):
    half = NSA_CMP_STRIDE
    a = b = None
    for l in range(half):
        x = tok_ref[0, pl.ds(l, ncp, stride=half), :]
        da = jnp.dot((x + pe_ref[0, l:l + 1, :]).astype(BF16), w1_ref[0, l], preferred_element_type=F32)
        db = jnp.dot((x + pe_ref[0, half + l:half + l + 1, :]).astype(BF16), w1_ref[0, half + l],
                     preferred_element_type=F32)
        a = da if a is None else a + da
        b = db if b is None else b + db
    h = jax.nn.gelu(a + pltpu.roll(b, ncp - 1, axis=0))
    y = jnp.dot(h.astype(BF16), w2_ref[0], preferred_element_type=F32)
    is_key = pl.program_id(0) == 0
    c2 = jnp.where(is_key, c2_ref[0], 1.0)
    s2 = jnp.where(is_key, s2_ref[0], 0.0)
    o_ref[0, 0, 0] = _rotate_half(y, c2, s2).astype(o_ref.dtype)


def compress_tokens(tok, pe, w1, w2, c2, s2):
    B, S, _ = tok.shape
    G, dk = NSA_KV_GROUPS, NSA_HEAD_DIM
    ncp = S // NSA_CMP_STRIDE
    hidden = w2.shape[1]
    return pl.pallas_call(
        functools.partial(_compress_kernel, ncp=ncp),
        grid=(2, B, G),
        in_specs=[
            pl.BlockSpec((1, S, dk), lambda s, b, g: (b, 0, s * G + g)),
            pl.BlockSpec((1, NSA_CMP_LEN, dk), lambda s, b, g: (s, 0, 0)),
            pl.BlockSpec((1, NSA_CMP_LEN, dk, hidden), lambda s, b, g: (s, 0, 0, 0)),
            pl.BlockSpec((1, hidden, dk), lambda s, b, g: (s, 0, 0)),
            pl.BlockSpec((1, ncp, dk), lambda s, b, g: (b, 0, 0)),
            pl.BlockSpec((1, ncp, dk), lambda s, b, g: (b, 0, 0)),
        ],
        out_specs=pl.BlockSpec((1, 1, 1, ncp, dk), lambda s, b, g: (s, b, g, 0, 0)),
        out_shape=jax.ShapeDtypeStruct((2, B, G, ncp, dk), BF16),
        compiler_params=_params("parallel", "parallel", "parallel"),
        name="compress_tokens",
    )(tok, pe, w1, w2, c2, s2)


def _cmp_select_kernel(q_ref, kc_ref, vc_ref, ovt_ref, gate_ref, oc_all_ref, nm_all_ref, oc_ref, nm_ref, *,
                       R, tq, n_sub, tile0, n_cmp, n_top):
    del oc_all_ref, nm_all_ref
    for u in range(n_sub):
        rows = slice(u * tq, (u + 1) * tq)
        t0 = (tile0 + pl.program_id(2) * n_sub + u) * tq
        _cmp_select_tile(q_ref.at[0, rows], kc_ref, vc_ref, ovt_ref, gate_ref.at[0, rows], oc_ref.at[0, rows],
                         nm_ref.at[0, rows], t0, R=R, tq=tq, n_cmp=n_cmp, n_top=n_top)


def _cmp_select_tile(q_ref, kc_ref, vc_ref, ovt_ref, gate_ref, oc_ref, nm_ref, t0, *, R, tq, n_cmp, n_top):
    M = R * tq
    q = jnp.concatenate([q_ref[:, r * LANE:(r + 1) * LANE] for r in range(R)], axis=0)
    kc = kc_ref[0, 0, 0]
    ncp = kc.shape[0]
    s = lax.dot_general(q, kc, (((1,), (1,)), ((), ())), preferred_element_type=F32)
    t = t0 + _mod_pow2(lax.broadcasted_iota(jnp.int32, (M, 1), 0), tq)
    lim = jnp.minimum(_div_pow2(t - (NSA_CMP_LEN - 1), NSA_CMP_STRIDE) + 1, n_cmp)
    valid = lax.broadcasted_iota(jnp.int32, (M, ncp), 1) < lim
    sm = jnp.where(valid, s, NEG_INF)
    e = jnp.exp2((sm - jnp.max(sm, axis=1, keepdims=True)).astype(BF16))
    ones_col = jnp.where(lax.broadcasted_iota(jnp.int32, (ncp, LANE), 1) == 0, 1.0, 0.0).astype(BF16)
    oc = jnp.dot(e, jnp.concatenate([vc_ref[0, 0, 0], ones_col], axis=1), preferred_element_type=F32)
    oc = oc[:, 0:LANE] * jnp.where(lim > 0, 1.0 / oc[:, LANE:LANE + 1], 0.0)
    for r in range(R):
        gate = gate_ref[:, NSA_N_BRANCHES * r:NSA_N_BRANCHES * r + 1]
        oc_ref[:, r * LANE:(r + 1) * LANE] = (oc[r * tq:(r + 1) * tq] * gate).astype(oc_ref.dtype)

    nb = ovt_ref.shape[0] - BF16_ROWS
    imp_un = lax.dot_general(ovt_ref[...], e, (((1,), (1,)), ((), ())), preferred_element_type=F32)
    t_l = t0 + _mod_pow2(lax.broadcasted_iota(jnp.int32, (1, M), 1), tq)
    lim_l = jnp.minimum(_div_pow2(t_l - (NSA_CMP_LEN - 1), NSA_CMP_STRIDE) + 1, n_cmp)
    imp_h = imp_un[0:nb] * jnp.where(lim_l > 0, 1.0 / imp_un[nb:nb + 1], 0.0)
    imp = imp_h[:, 0:tq]
    for r in range(1, R):
        imp = imp + imp_h[:, r * tq:(r + 1) * tq]
    j = lax.broadcasted_iota(jnp.int32, (nb, tq), 0)
    jf = j.astype(F32)
    cur = _div_pow2(t0 + lax.broadcasted_iota(jnp.int32, (nb, tq), 1), NSA_SEL_BLOCK)
    forced = jnp.logical_or(j == 0, jnp.logical_or(j == cur, j == cur - 1))
    x = jnp.where(j > cur, NEG_INF, imp + FORCE_BONUS * forced.astype(F32))
    for _ in range(n_top):
        mx = jnp.max(x, axis=0, keepdims=True)
        first = jnp.min(jnp.where(x == mx, jf, float(nb)), axis=0, keepdims=True)
        x = jnp.where(jf == first, jnp.where(mx > 0.5 * NEG_INF, -jnp.inf, NEG_INF), x)
    nm_ref[:, 0:nb] = jnp.where(x == -jnp.inf, 0.0, NEG_INF).T.astype(nm_ref.dtype)
    if nb < nm_ref.shape[1]:
        nm_ref[:, nb:] = jnp.full((tq, nm_ref.shape[1] - nb), NEG_INF, nm_ref.dtype)


def cmp_attention_select(q, kv_cmp, overlap, gate, *, gate_block0, n_cmp, n_top, tq=CMP_TILE, n_sub=CMP_SUBTILES):
    B, S, _ = q.shape
    G, R = NSA_KV_GROUPS, NSA_R
    NB, NCP = overlap.shape
    tq = _pick(S, tq)
    tb = n_sub * tq
    n_parts = next(n for n in (4, 2, 1) if NCP % (n * LANE) == 0 and S % (n * tb) == 0)
    steps = S // (n_parts * tb)
    outs = [jnp.zeros((B, S, G * R * LANE), BF16), jnp.zeros((B, S, G * NB), BF16)]
    for part in range(n_parts):
        ncp_w = NCP * (part + 1) // n_parts
        nb_w = min(NB, -(-(S * (part + 1) // n_parts // NSA_SEL_BLOCK) // LANE) * LANE)
        ov = jnp.concatenate([overlap[:nb_w, :ncp_w], jnp.ones((1, ncp_w), BF16),
                              jnp.zeros((BF16_ROWS - 1, ncp_w), BF16)], axis=0)
        kern = functools.partial(_cmp_select_kernel, R=R, tq=tq, n_sub=n_sub, tile0=part * steps * n_sub,
                                 n_cmp=n_cmp, n_top=n_top)
        off = part * steps
        carried = list(outs)
        outs = pl.pallas_call(
            kern,
            grid=(B, G, steps),
            in_specs=[
                pl.BlockSpec((1, tb, R * LANE), lambda b, g, i, off=off: (b, i + off, g)),
                pl.BlockSpec((1, 1, 1, ncp_w, LANE), lambda b, g, i: (0, b, g, 0, 0)),
                pl.BlockSpec((1, 1, 1, ncp_w, LANE), lambda b, g, i: (1, b, g, 0, 0)),
                pl.BlockSpec((nb_w + BF16_ROWS, ncp_w), lambda b, g, i: (0, 0)),
                pl.BlockSpec((1, tb, LANE), lambda b, g, i, off=off: (b, i + off, gate_block0 + g)),
            ] + [pl.BlockSpec(memory_space=pl.ANY)] * len(carried),
            out_specs=[
                pl.BlockSpec((1, tb, R * LANE), lambda b, g, i, off=off: (b, i + off, g)),
                pl.BlockSpec((1, tb, NB), lambda b, g, i, off=off: (b, i + off, g)),
            ],
            out_shape=[
                jax.ShapeDtypeStruct((B, S, G * R * LANE), BF16),
                jax.ShapeDtypeStruct((B, S, G * NB), BF16),
            ],
            input_output_aliases={5 + k: k for k in range(len(carried))},
            compiler_params=_params("parallel", "parallel", "arbitrary"),
            name="cmp_attention_select",
        )(q, kv_cmp, kv_cmp, ov, gate, *carried)
    return outs


def _rope_tables(pos, dim):
    inv = 1.0 / (ROPE_THETA ** (jnp.arange(0, dim, 2, dtype=F32) / dim))
    ang = pos.astype(F32)[..., None] * inv
    pad = [(0, 0)] * (ang.ndim - 1) + [(0, (LANE - dim) // 2)]
    c, s = jnp.pad(jnp.cos(ang), pad), jnp.pad(jnp.sin(ang), pad)
    return jnp.concatenate([c, c], axis=-1), jnp.concatenate([-s, s], axis=-1)


def _spread_rope_cols(w):
    dim = w.shape[-1]
    pad = [(0, 0)] * (w.ndim - 1) + [(0, (LANE - dim) // 2)]
    return jnp.concatenate([jnp.pad(w[..., :dim // 2], pad), jnp.pad(w[..., dim // 2:], pad)], axis=-1)


def nsa_mixer(x, B, S, positions, norm_w, w_in, cmp_pe, cmp_w1, cmp_w2, w_out):
    T, D = x.shape
    H, G, R, dk = NSA_HEADS, NSA_KV_GROUPS, NSA_R, NSA_HEAD_DIM
    q_cols, kv_cols = H * dk, NSA_N_BRANCHES * 2 * G * dk
    grp = G * dk
    q_scale = dk ** -0.5 * LOG2E
    tables = _rope_tables(positions.reshape(T), dk)

    w_q = w_in[:, :q_cols]
    w_kv = w_in[:, q_cols:q_cols + kv_cols].reshape(D, NSA_N_BRANCHES, 2, grp)
    w_gate = w_in[:, q_cols + kv_cols:].reshape(D, G, R * NSA_N_BRANCHES)
    w_gate = jnp.pad(w_gate, ((0, 0), (0, 0), (0, LANE - R * NSA_N_BRANCHES))).reshape(D, G * LANE)
    tn = 2 * grp
    nq = q_cols // tn
    qv = proj(x, norm_w, jnp.concatenate([w_q, w_kv[:, 1, 1], w_kv[:, 2, 1]], axis=1).astype(BF16), n_blocks=nq + 1,
              tn=tn, rope=tables, rope_mask=(1 << nq) - 1, scale_mask=(1 << nq) - 1, scale=q_scale,
              out_dtype=BF16).reshape(B, S, q_cols + 2 * grp)
    k_t = proj(x, norm_w, jnp.concatenate([w_kv[:, 1, 0], w_kv[:, 2, 0]], axis=1).astype(BF16), n_blocks=1, tn=tn,
               rope=tables, rope_mask=1, transposed=True, out_dtype=BF16)
    w_cg = jnp.concatenate([w_kv[:, 0, 0], w_kv[:, 0, 1], w_gate], axis=1).astype(BF16)
    cmp_gates = proj(x, norm_w, w_cg, n_blocks=1, tn=w_cg.shape[1], sigmoid_from=2 * grp // LANE)
    cmp_gates = cmp_gates.reshape(B, S, w_cg.shape[1])
    gate0 = 2 * grp // LANE

    ncp = S // NSA_CMP_STRIDE
    n_cmp = (S - NSA_CMP_LEN) // NSA_CMP_STRIDE + 1
    cmp_end = jnp.minimum(jnp.arange(ncp) * NSA_CMP_STRIDE + NSA_CMP_LEN - 1, S - 1)
    c2c, s2c = _rope_tables(positions[:, cmp_end], dk)
    hidden = cmp_w1.shape[-1]
    kv_cmp = compress_tokens(cmp_gates, cmp_pe, cmp_w1.reshape(2, NSA_CMP_LEN, dk, hidden).astype(BF16),
                             cmp_w2.astype(BF16), c2c, s2c)

    n_sel = S // NSA_SEL_BLOCK
    nb = -(-n_sel // LANE) * LANE
    blk_start = jnp.arange(ncp) * NSA_CMP_STRIDE
    sel_start = jnp.arange(nb) * NSA_SEL_BLOCK
    overlap = ((blk_start[None, :] <= sel_start[:, None] + NSA_SEL_BLOCK - 1)
               & (blk_start[None, :] + NSA_CMP_LEN - 1 >= sel_start[:, None])
               & (jnp.arange(ncp)[None, :] < n_cmp) & (jnp.arange(nb)[:, None] < n_sel)).astype(BF16)
    o_cmp, neg_mask = cmp_attention_select(qv, kv_cmp, overlap, cmp_gates, gate_block0=gate0, n_cmp=n_cmp,
                                           n_top=min(NSA_N_SELECT, n_sel))

    blk_onehot = (jnp.arange(LANE)[:, None] == jnp.arange(S)[None, :] // NSA_SEL_BLOCK % LANE).astype(BF16)
    v0 = q_cols // dk
    o_slc = flash_attention(qv, neg_mask, k_t, blk_onehot, qv, cmp_gates, n_groups=G, R=R, E=nb // LANE,
                            qm_col=lambda g: g, qx_col=lambda g: g, km_col=lambda g: g,
                            v_col=lambda g: v0 + g, kx_batched=False,
                            gate_cols=tuple(NSA_N_BRANCHES * r + 1 for r in range(R)), gate_block0=gate0,
                            mode="causal", tile=NSA_FLASH_TILE, keys_per_version=LANE * NSA_SEL_BLOCK)
    o_win = flash_attention(qv, None, k_t, None, qv, cmp_gates, n_groups=G, R=R, E=0,
                            qm_col=lambda g: g, km_col=lambda g: G + g, v_col=lambda g: v0 + G + g,
                            kx_batched=False, gate_cols=tuple(NSA_N_BRANCHES * r + 2 for r in range(R)),
                            gate_block0=gate0, mode="window", window=NSA_WINDOW, tile=NSA_FLASH_TILE)
    return matmul_residual([o.reshape(T, H * dk) for o in (o_cmp, o_slc, o_win)], w_out.astype(BF16), x)


def mla_mixer(x, B, S, positions, norm_w, w_in, q_norm_w, kv_norm_w, w_uq, w_ukv, w_out):
    T, D = x.shape
    H = MLA_HEADS
    q_rank, kv_rank = q_norm_w.shape[0], kv_norm_w.shape[0]
    assert q_rank == kv_rank and q_rank % LANE == 0
    q_scale = MLA_QK_DIM ** -0.5 * LOG2E
    tables = _rope_tables(positions.reshape(T), MLA_ROPE_DIM)

    latent = proj(x, norm_w, w_in[:, :2 * q_rank].astype(BF16), n_blocks=1, tn=2 * q_rank)
    w_kr = _spread_rope_cols(w_in[:, 2 * q_rank:]).astype(BF16)
    k_rope_t = proj(x, norm_w, w_kr, n_blocks=1, tn=LANE, rope=tables, rope_mask=1, transposed=True, out_dtype=BF16)
    w_uq3 = w_uq.reshape(q_rank, H, MLA_QK_DIM)
    w_uq_p = jnp.concatenate([w_uq3[:, :, :MLA_NOPE_DIM].reshape(q_rank, H * MLA_NOPE_DIM),
                              _spread_rope_cols(w_uq3[:, :, MLA_NOPE_DIM:]).reshape(q_rank, H * LANE)],
                             axis=1).astype(BF16)
    tn = H * LANE
    n_nope = H * MLA_NOPE_DIM // tn
    n_all = n_nope + H * LANE // tn
    q = proj(latent, q_norm_w, w_uq_p, n_blocks=n_all, tn=tn, x_col_block=0, rope=tables,
             rope_mask=(1 << n_all) - (1 << n_nope), scale_mask=(1 << n_all) - 1, scale=q_scale,
             out_dtype=BF16).reshape(B, S, n_all * tn)
    w_ukv3 = w_ukv.reshape(kv_rank, H, MLA_NOPE_DIM + MLA_V_DIM)
    w_uk = w_ukv3[:, :, :MLA_NOPE_DIM].reshape(kv_rank, H * MLA_NOPE_DIM).astype(BF16)
    w_uv = w_ukv3[:, :, MLA_NOPE_DIM:].reshape(kv_rank, H * MLA_V_DIM).astype(BF16)
    k_nope_t = proj(latent, kv_norm_w, w_uk, n_blocks=H * MLA_NOPE_DIM // tn, tn=tn, x_col_block=1, transposed=True,
                    out_dtype=BF16)
    v = proj(latent, kv_norm_w, w_uv, n_blocks=H * MLA_V_DIM // tn, tn=tn, x_col_block=1,
             out_dtype=BF16).reshape(B, S, H * MLA_V_DIM)
    o = flash_attention(q, q, k_nope_t, k_rope_t, v, n_groups=H, R=1, E=1,
                        qm_col=lambda h: h, qx_col=lambda h: H + h, km_col=lambda h: h, v_col=lambda h: h,
                        kx_batched=True, mode="causal", tile=MLA_FLASH_TILE)
    return matmul_residual([o.reshape(T, H * MLA_V_DIM)], w_out.astype(BF16), x)


def kernel(x, positions, ffn_norm_w, ffn_w_in, ffn_w_out, mix_norm_w, nsa_w_in, nsa_cmp_pe, nsa_cmp_w1, nsa_cmp_w2,
           nsa_w_out, mla_w_in, mla_q_norm_w, mla_kv_norm_w, mla_w_uq, mla_w_ukv, mla_w_out, final_norm_w):
    B, S, D = x.shape
    depth = ffn_norm_w.shape[0]
    n_mixers = 2
    h = x.reshape(B * S, D)
    for i in range(depth):
        h = ffn_half_step(h, ffn_norm_w[i, 0], ffn_w_in[i, 0].astype(BF16), ffn_w_out[i, 0].astype(BF16))
        j = i // n_mixers
        if i % n_mixers == 0:
            h = nsa_mixer(h, B, S, positions, mix_norm_w[i], nsa_w_in[j], nsa_cmp_pe[j], nsa_cmp_w1[j],
                          nsa_cmp_w2[j], nsa_w_out[j])
        else:
            h = mla_mixer(h, B, S, positions, mix_norm_w[i], mla_w_in[j], mla_q_norm_w[j], mla_kv_norm_w[j],
                          mla_w_uq[j], mla_w_ukv[j], mla_w_out[j])
        last = i == depth - 1
        h = ffn_half_step(h, ffn_norm_w[i, 1], ffn_w_in[i, 1].astype(BF16), ffn_w_out[i, 1].astype(BF16),
                          final_w=final_norm_w if last else None)
    return h.reshape(B, S, D)
```

```python
import functools
import math

import jax
import jax.numpy as jnp
from jax import lax
from jax.experimental import pallas as pl
from jax.experimental.pallas import tpu as pltpu

F32 = jnp.float32
BF16 = jnp.bfloat16

LANE = 128
BF16_ROWS = 16
RMS_EPS = 1e-6
ROPE_THETA = 10000.0
NEG_INF = -1e30
MASK_PASS = 3e38
FORCE_BONUS = 1e9
LOG2E = math.log2(math.e)
VMEM_LIMIT = 56 * 1024 * 1024
FFN_VMEM_LIMIT = 61 * 1024 * 1024
ROW_TILE = 1024
FFN_HIDDEN_TILE = 512
MLA_FLASH_TILE = 1024
NSA_FLASH_TILE = 512
FLASH_DEPTH = 4
CMP_TILE = 128
CMP_SUBTILES = 4
FFN_SLAB = 256

NSA_HEADS = 16
NSA_HEAD_DIM = 128
NSA_KV_GROUPS = 4
NSA_R = NSA_HEADS // NSA_KV_GROUPS
NSA_N_BRANCHES = 3
NSA_CMP_LEN = 32
NSA_CMP_STRIDE = 16
NSA_SEL_BLOCK = 64
NSA_N_SELECT = 16
NSA_WINDOW = 512

MLA_HEADS = 16
MLA_NOPE_DIM = 128
MLA_ROPE_DIM = 64
MLA_V_DIM = 128
MLA_QK_DIM = MLA_NOPE_DIM + MLA_ROPE_DIM


def _pick(n, pref):
    if n <= pref:
        return n
    t = (pref // LANE) * LANE
    while t > LANE and n % t:
        t -= LANE
    assert n % t == 0, (n, pref)
    return t


def _div_pow2(x, d):
    assert d & (d - 1) == 0, d
    return jnp.right_shift(x, d.bit_length() - 1)


def _mod_pow2(x, d):
    assert d & (d - 1) == 0, d
    return jnp.bitwise_and(x, d - 1)


def _params(*sem, vmem=VMEM_LIMIT):
    return pltpu.CompilerParams(dimension_semantics=sem, vmem_limit_bytes=vmem)


def _repeat_lanes(x, n):
    return jnp.concatenate([x] * n, axis=1)


def _rotate_half(y, c2, s2):
    return y * c2 + pltpu.roll(y, LANE // 2, axis=1) * s2


def _rms_rows(x, w):
    ms = jnp.mean(x * x, axis=-1, keepdims=True)
    return x * lax.rsqrt(ms + RMS_EPS) * w


def _proj_kernel(*refs, has_rope, rope_mask, scale_mask, scale, sigmoid_from, transposed):
    if has_rope:
        x_ref, nw_ref, w_ref, c2_ref, s2_ref, o_ref, xn_scr = refs
    else:
        x_ref, nw_ref, w_ref, o_ref, xn_scr = refs
    j = pl.program_id(1)

    @pl.when(j == 0)
    def _():
        xn_scr[...] = _rms_rows(x_ref[...], nw_ref[...]).astype(BF16)

    y = jnp.dot(xn_scr[...], w_ref[...], preferred_element_type=F32)
    if has_rope:
        rope_on = jnp.bitwise_and(lax.shift_right_logical(jnp.int32(rope_mask), j), 1) == 1
        factor = jnp.where(jnp.bitwise_and(lax.shift_right_logical(jnp.int32(scale_mask), j), 1) == 1, scale, 1.0)
        c2 = jnp.where(rope_on, c2_ref[...], 1.0) * factor
        s2 = jnp.where(rope_on, s2_ref[...], 0.0) * factor
    for h in range(y.shape[1] // LANE):
        slab = y[:, h * LANE:(h + 1) * LANE]
        if has_rope:
            slab = _rotate_half(slab, c2, s2)
        if sigmoid_from is not None and h >= sigmoid_from:
            slab = jax.nn.sigmoid(slab)
        if transposed:
            o_ref[h * LANE:(h + 1) * LANE, :] = slab.T.astype(o_ref.dtype)
        else:
            o_ref[:, h * LANE:(h + 1) * LANE] = slab.astype(o_ref.dtype)


def proj(x, norm_w, w, *, n_blocks, tn, col_map=None, x_col_block=0, rope=None, rope_mask=0, scale_mask=0, scale=1.0,
         sigmoid_from=None, transposed=False, out_dtype=F32, tm=ROW_TILE):
    T = x.shape[0]
    K = w.shape[0]
    tm = _pick(T, tm)
    col_map = col_map or (lambda j: j)
    has_rope = rope is not None
    assert has_rope or scale_mask == 0
    in_specs = [
        pl.BlockSpec((tm, K), lambda i, j: (i, x_col_block)),
        pl.BlockSpec((1, K), lambda i, j: (0, 0)),
        pl.BlockSpec((K, tn), lambda i, j: (0, col_map(j))),
    ]
    args = [x, norm_w.reshape(1, K).astype(F32), w]
    if has_rope:
        in_specs += [pl.BlockSpec((tm, LANE), lambda i, j: (i, 0))] * 2
        args += list(rope)
    kern = functools.partial(_proj_kernel, has_rope=has_rope, rope_mask=rope_mask, scale_mask=scale_mask,
                             scale=scale, sigmoid_from=sigmoid_from, transposed=transposed)
    if transposed:
        out_spec = pl.BlockSpec((tn, tm), lambda i, j: (j, i))
        out_shape = jax.ShapeDtypeStruct((n_blocks * tn, T), out_dtype)
    else:
        out_spec = pl.BlockSpec((tm, tn), lambda i, j: (i, j))
        out_shape = jax.ShapeDtypeStruct((T, n_blocks * tn), out_dtype)
    return pl.pallas_call(
        kern,
        grid=(T // tm, n_blocks),
        in_specs=in_specs,
        out_specs=out_spec,
        out_shape=out_shape,
        scratch_shapes=[pltpu.VMEM((tm, K), BF16)],
        compiler_params=_params("parallel", "arbitrary"),
        name="proj",
    )(*args)


def _matmul_residual_kernel(*refs):
    *a_refs, w_ref, r_ref, o_ref = refs
    a = a_refs[0][...]
    if len(a_refs) > 1:
        a = a.astype(F32)
        for a_ref in a_refs[1:]:
            a = a + a_ref[...].astype(F32)
        a = a.astype(BF16)
    o_ref[...] = r_ref[...] + jnp.dot(a, w_ref[...], preferred_element_type=F32)


def matmul_residual(a_list, w, res, *, tm=ROW_TILE, tn=ROW_TILE):
    T, K = a_list[0].shape
    N = w.shape[1]
    tm, tn = _pick(T, tm), _pick(N, tn)
    return pl.pallas_call(
        _matmul_residual_kernel,
        grid=(T // tm, N // tn),
        in_specs=[pl.BlockSpec((tm, K), lambda i, j: (i, 0))] * len(a_list) + [
            pl.BlockSpec((K, tn), lambda i, j: (0, j)),
            pl.BlockSpec((tm, tn), lambda i, j: (i, j)),
        ],
        out_specs=pl.BlockSpec((tm, tn), lambda i, j: (i, j)),
        out_shape=jax.ShapeDtypeStruct((T, N), F32),
        compiler_params=_params("parallel", "arbitrary"),
        name="matmul_residual",
    )(*a_list, w, res)


def _ffn_kernel(x_ref, nw_ref, wg_ref, wu_ref, wo_ref, fw_ref, o_ref, xn_scr, *, final_norm):
    f = pl.program_id(1)
    tm = x_ref.shape[0]
    slab = min(FFN_SLAB, tm)

    def over_slabs(fn):
        def body(r, carry):
            fn(pl.ds(pl.multiple_of(r * slab, slab), slab))
            return carry

        lax.fori_loop(0, tm // slab, body, 0)

    @pl.when(f == 0)
    def _():
        def prologue(rows):
            xn_scr[rows, :] = _rms_rows(x_ref[rows, :], nw_ref[...]).astype(BF16)

        over_slabs(prologue)
        o_ref[...] = jnp.zeros_like(o_ref)

    xn = xn_scr[...]
    g = jnp.dot(xn, wg_ref[...], preferred_element_type=F32)
    u = jnp.dot(xn, wu_ref[...], preferred_element_type=F32)
    h = (g * jax.nn.sigmoid(g) * u).astype(BF16)
    o_ref[...] += jnp.dot(h, wo_ref[...], preferred_element_type=F32)

    @pl.when(f == pl.num_programs(1) - 1)
    def _():
        def epilogue(rows):
            y = x_ref[rows, :] + 0.5 * o_ref[rows, :]
            if final_norm:
                y = _rms_rows(y, fw_ref[...])
            o_ref[rows, :] = y

        over_slabs(epilogue)


def ffn_half_step(x, norm_w, w_in, w_out, final_w=None, *, tm=ROW_TILE, tf=FFN_HIDDEN_TILE):
    T, D = x.shape
    DFF = w_out.shape[0]
    tm, tf = _pick(T, tm), _pick(DFF, tf)
    nf = DFF // tf
    final_norm = final_w is not None
    fw = (final_w if final_norm else norm_w).reshape(1, D).astype(F32)
    return pl.pallas_call(
        functools.partial(_ffn_kernel, final_norm=final_norm),
        grid=(T // tm, nf),
        in_specs=[
            pl.BlockSpec((tm, D), lambda i, f: (i, 0)),
            pl.BlockSpec((1, D), lambda i, f: (0, 0)),
            pl.BlockSpec((D, tf), lambda i, f: (0, f)),
            pl.BlockSpec((D, tf), lambda i, f: (0, f + nf)),
            pl.BlockSpec((tf, D), lambda i, f: (f, 0)),
            pl.BlockSpec((1, D), lambda i, f: (0, 0)),
        ],
        out_specs=pl.BlockSpec((tm, D), lambda i, f: (i, 0)),
        out_shape=jax.ShapeDtypeStruct((T, D), F32),
        scratch_shapes=[pltpu.VMEM((tm, D), BF16)],
        compiler_params=_params("parallel", "arbitrary", vmem=FFN_VMEM_LIMIT),
        name="ffn_half_step",
    )(x, norm_w.reshape(1, D).astype(F32), w_in, w_in, w_out, fw)


def _flash_kernel(*refs, R, E, has_kx, gate_cols, tq, tk, mode, chunks_per_version, depth):
    it = iter(refs)
    qm_ref = next(it)
    qx_ref = next(it) if E else None
    km_ref = next(it)
    kx_ref = next(it) if has_kx else None
    v_ref = next(it)
    gate_ref = next(it) if gate_cols else None
    cap_ref = next(it)
    o_ref = next(it)
    q_scr, m_scr, acc_scr, s_scr = it
    M = R * tq
    i = pl.program_id(2)

    for e in range(max(E, 1)):
        for r in range(R):
            q_scr[e, r * tq:(r + 1) * tq, 0:LANE] = qm_ref[0, :, r * LANE:(r + 1) * LANE]
            if E:
                q_scr[e, r * tq:(r + 1) * tq, LANE:2 * LANE] = qx_ref[0, :, e * LANE:(e + 1) * LANE]
    m_scr[...] = jnp.full_like(m_scr, NEG_INF)
    acc_scr[...] = jnp.zeros_like(acc_scr)
    ones_col = jnp.where(lax.broadcasted_iota(jnp.int32, (tk, LANE), 1) == 0, 1.0, 0.0).astype(BF16)

    def scores(c):
        k0 = pl.multiple_of(c * tk, tk)
        kt = km_ref[:, pl.ds(k0, tk)]
        if has_kx:
            kt = jnp.concatenate([kt, kx_ref[:, pl.ds(k0, tk)]], axis=0)
        q = q_scr[c // chunks_per_version] if E > 1 else q_scr[0]
        return jnp.dot(q, kt, preferred_element_type=F32).astype(BF16)

    def softmax_pv(c, s, cap=None):
        k0 = pl.multiple_of(c * tk, tk)
        if cap is not None:
            s = jnp.minimum(s, cap)
        m_prev = m_scr[...]
        m_new = jnp.maximum(m_prev, jnp.max(s, axis=1, keepdims=True).astype(F32))
        alpha = jnp.exp2(m_prev - m_new)
        p = jnp.exp2(s - _repeat_lanes(m_new.astype(BF16), tk // LANE))
        v1 = jnp.concatenate([v_ref[0, pl.ds(k0, tk), :], ones_col], axis=1)
        acc_scr[...] = _repeat_lanes(alpha, 2) * acc_scr[...] + jnp.dot(p, v1, preferred_element_type=F32)
        m_scr[...] = m_new

    if mode == "causal":
        n_full = (i * tq) // tk
        bufs = [s_scr.at[u] for u in range(depth)]
        bufs[0][...] = scores(0)

        def group(j, carry):
            for u in range(depth):
                bufs[(u + 1) % depth][...] = scores(depth * j + u + 1)
                softmax_pv(depth * j + u, bufs[u][...])
            return carry

        n_groups = n_full // depth
        lax.fori_loop(0, n_groups, group, 0)

        def tail(c, carry):
            softmax_pv(c, bufs[0][...])
            bufs[0][...] = scores(c + 1)
            return carry

        lax.fori_loop(depth * n_groups, n_full, tail, 0)
        softmax_pv(n_full, bufs[0][...], cap_ref[0])
    else:
        prev = jnp.maximum(i - 1, 0)
        s_prev = scores(prev)
        s_diag = scores(i)
        softmax_pv(prev, s_prev, jnp.where(i > 0, cap_ref[1], NEG_INF))
        softmax_pv(i, s_diag, cap_ref[0])

    out = acc_scr[:, 0:LANE] / acc_scr[:, LANE:LANE + 1]
    for r in range(R):
        o_r = out[r * tq:(r + 1) * tq]
        if gate_cols:
            o_r = o_r * gate_ref[0, :, gate_cols[r]:gate_cols[r] + 1]
        o_ref[0, :, r * LANE:(r + 1) * LANE] = o_r.astype(o_ref.dtype)


def flash_attention(qm, qx, km, kx, v, gate=None, *, n_groups, R, E, qm_col, qx_col=None, km_col, v_col, kx_batched,
                    gate_cols=None, gate_block0=0, mode, window=0, tile, depth=FLASH_DEPTH, keys_per_version=None):
    B, S, _ = qm.shape
    tq = tk = _pick(S, tile)
    assert mode == "causal" or tq == window
    has_kx = kx is not None
    dk = 2 * LANE if has_kx else LANE
    cpv = (keys_per_version // tk) if keys_per_version else 1
    M = R * tq
    in_specs = [pl.BlockSpec((1, tq, R * LANE), lambda b, g, i: (b, i, qm_col(g)))]
    args = [qm]
    if E:
        in_specs.append(pl.BlockSpec((1, tq, E * LANE), lambda b, g, i: (b, i, qx_col(g))))
        args.append(qx)
    in_specs.append(pl.BlockSpec((LANE, S), lambda b, g, i: (km_col(g), b)))
    args.append(km)
    if has_kx:
        in_specs.append(pl.BlockSpec((LANE, S), lambda b, g, i: (0, b if kx_batched else 0)))
        args.append(kx)
    in_specs.append(pl.BlockSpec((1, S, LANE), lambda b, g, i: (b, 0, v_col(g))))
    args.append(v)
    if gate_cols:
        in_specs.append(pl.BlockSpec((1, tq, LANE), lambda b, g, i: (b, i, gate_block0 + g)))
        args.append(gate)
    q_off = lax.broadcasted_iota(jnp.int32, (M, tk), 0) % tq
    k_off = lax.broadcasted_iota(jnp.int32, (M, tk), 1)
    caps = [k_off <= q_off] + ([k_off > q_off] if mode == "window" else [])
    cap = jnp.stack([jnp.where(c, MASK_PASS, NEG_INF).astype(BF16) for c in caps])
    in_specs.append(pl.BlockSpec(cap.shape, lambda b, g, i: (0, 0, 0)))
    args.append(cap)
    kern = functools.partial(_flash_kernel, R=R, E=E, has_kx=has_kx, gate_cols=gate_cols, tq=tq, tk=tk, mode=mode,
                             chunks_per_version=cpv, depth=depth)
    return pl.pallas_call(
        kern,
        grid=(B, n_groups, S // tq),
        in_specs=in_specs,
        out_specs=pl.BlockSpec((1, tq, R * LANE), lambda b, g, i: (b, i, g)),
        out_shape=jax.ShapeDtypeStruct((B, S, n_groups * R * LANE), BF16),
        scratch_shapes=[
            pltpu.VMEM((max(E, 1), M, dk), BF16),
            pltpu.VMEM((M, LANE), F32),
            pltpu.VMEM((M, 2 * LANE), F32),
            pltpu.VMEM((depth, M, tk), BF16),
        ],
        compiler_params=_params("parallel", "parallel", "arbitrary"),
        name="flash_" + mode + ("_x%d" % E if E else ""),
    )(*args)


def _compress_kernel(tok_ref, pe_ref, w1_ref, w2_ref, c2_ref, s2_ref, o_ref, *, ncp):
    half = NSA_CMP_STRIDE
    a = b = None
    for l in range(half):
        x = tok_ref[0, pl.ds(l, ncp, stride=half), :]
        da = jnp.dot((x + pe_ref[0, l:l + 1, :]).astype(BF16), w1_ref[0, l], preferred_element_type=F32)
        db = jnp.dot((x + pe_ref[0, half + l:half + l + 1, :]).astype(BF16), w1_ref[0, half + l],
                     preferred_element_type=F32)
        a = da if a is None else a + da
        b = db if b is None else b + db
    h = jax.nn.gelu(a + pltpu.roll(b, ncp - 1, axis=0))
    y = jnp.dot(h.astype(BF16), w2_ref[0], preferred_element_type=F32)
    is_key = pl.program_id(0) == 0
    c2 = jnp.where(is_key, c2_ref[0], 1.0)
    s2 = jnp.where(is_key, s2_ref[0], 0.0)
    o_ref[0, 0, 0] = _rotate_half(y, c2, s2).astype(o_ref.dtype)


def compress_tokens(tok, pe, w1, w2, c2, s2):
    B, S, _ = tok.shape
    G, dk = NSA_KV_GROUPS, NSA_HEAD_DIM
    ncp = S // NSA_CMP_STRIDE
    hidden = w2.shape[1]
    return pl.pallas_call(
        functools.partial(_compress_kernel, ncp=ncp),
        grid=(2, B, G),
        in_specs=[
            pl.BlockSpec((1, S, dk), lambda s, b, g: (b, 0, s * G + g)),
            pl.BlockSpec((1, NSA_CMP_LEN, dk), lambda s, b, g: (s, 0, 0)),
            pl.BlockSpec((1, NSA_CMP_LEN, dk, hidden), lambda s, b, g: (s, 0, 0, 0)),
            pl.BlockSpec((1, hidden, dk), lambda s, b, g: (s, 0, 0)),
            pl.BlockSpec((1, ncp, dk), lambda s, b, g: (b, 0, 0)),
            pl.BlockSpec((1, ncp, dk), lambda s, b, g: (b, 0, 0)),
        ],
        out_specs=pl.BlockSpec((1, 1, 1, ncp, dk), lambda s, b, g: (s, b, g, 0, 0)),
        out_shape=jax.ShapeDtypeStruct((2, B, G, ncp, dk), BF16),
        compiler_params=_params("parallel", "parallel", "parallel"),
        name="compress_tokens",
    )(tok, pe, w1, w2, c2, s2)


def _cmp_select_kernel(q_ref, kc_ref, vc_ref, ovt_ref, gate_ref, oc_all_ref, nm_all_ref, oc_ref, nm_ref, *,
                       R, tq, n_sub, tile0, n_cmp, n_top):
    del oc_all_ref, nm_all_ref
    for u in range(n_sub):
        rows = slice(u * tq, (u + 1) * tq)
        t0 = (tile0 + pl.program_id(2) * n_sub + u) * tq
        _cmp_select_tile(q_ref.at[0, rows], kc_ref, vc_ref, ovt_ref, gate_ref.at[0, rows], oc_ref.at[0, rows],
                         nm_ref.at[0, rows], t0, R=R, tq=tq, n_cmp=n_cmp, n_top=n_top)


def _cmp_select_tile(q_ref, kc_ref, vc_ref, ovt_ref, gate_ref, oc_ref, nm_ref, t0, *, R, tq, n_cmp, n_top):
    M = R * tq
    q = jnp.concatenate([q_ref[:, r * LANE:(r + 1) * LANE] for r in range(R)], axis=0)
    kc = kc_ref[0, 0, 0]
    ncp = kc.shape[0]
    s = lax.dot_general(q, kc, (((1,), (1,)), ((), ())), preferred_element_type=F32)
    t = t0 + _mod_pow2(lax.broadcasted_iota(jnp.int32, (M, 1), 0), tq)
    lim = jnp.minimum(_div_pow2(t - (NSA_CMP_LEN - 1), NSA_CMP_STRIDE) + 1, n_cmp)
    valid = lax.broadcasted_iota(jnp.int32, (M, ncp), 1) < lim
    sm = jnp.where(valid, s, NEG_INF)
    e = jnp.exp2((sm - jnp.max(sm, axis=1, keepdims=True)).astype(BF16))
    ones_col = jnp.where(lax.broadcasted_iota(jnp.int32, (ncp, LANE), 1) == 0, 1.0, 0.0).astype(BF16)
    oc = jnp.dot(e, jnp.concatenate([vc_ref[0, 0, 0], ones_col], axis=1), preferred_element_type=F32)
    oc = oc[:, 0:LANE] * jnp.where(lim > 0, 1.0 / oc[:, LANE:LANE + 1], 0.0)
    for r in range(R):
        gate = gate_ref[:, NSA_N_BRANCHES * r:NSA_N_BRANCHES * r + 1]
        oc_ref[:, r * LANE:(r + 1) * LANE] = (oc[r * tq:(r + 1) * tq] * gate).astype(oc_ref.dtype)

    nb = ovt_ref.shape[0] - BF16_ROWS
    imp_un = lax.dot_general(ovt_ref[...], e, (((1,), (1,)), ((), ())), preferred_element_type=F32)
    t_l = t0 + _mod_pow2(lax.broadcasted_iota(jnp.int32, (1, M), 1), tq)
    lim_l = jnp.minimum(_div_pow2(t_l - (NSA_CMP_LEN - 1), NSA_CMP_STRIDE) + 1, n_cmp)
    imp_h = imp_un[0:nb] * jnp.where(lim_l > 0, 1.0 / imp_un[nb:nb + 1], 0.0)
    imp = imp_h[:, 0:tq]
    for r in range(1, R):
        imp = imp + imp_h[:, r * tq:(r + 1) * tq]
    j = lax.broadcasted_iota(jnp.int32, (nb, tq), 0)
    jf = j.astype(F32)
    cur = _div_pow2(t0 + lax.broadcasted_iota(jnp.int32, (nb, tq), 1), NSA_SEL_BLOCK)
    forced = jnp.logical_or(j == 0, jnp.logical_or(j == cur, j == cur - 1))
    x = jnp.where(j > cur, NEG_INF, imp + FORCE_BONUS * forced.astype(F32))
    for _ in range(n_top):
        mx = jnp.max(x, axis=0, keepdims=True)
        first = jnp.min(jnp.where(x == mx, jf, float(nb)), axis=0, keepdims=True)
        x = jnp.where(jf == first, jnp.where(mx > 0.5 * NEG_INF, -jnp.inf, NEG_INF), x)
    nm_ref[:, 0:nb] = jnp.where(x == -jnp.inf, 0.0, NEG_INF).T.astype(nm_ref.dtype)
    if nb < nm_ref.shape[1]:
        nm_ref[:, nb:] = jnp.full((tq, nm_ref.shape[1] - nb), NEG_INF, nm_ref.dtype)


def cmp_attention_select(q, kv_cmp, overlap, gate, *, gate_block0, n_cmp, n_top, tq=CMP_TILE, n_sub=CMP_SUBTILES):
    B, S, _ = q.shape
    G, R = NSA_KV_GROUPS, NSA_R
    NB, NCP = overlap.shape
    tq = _pick(S, tq)
    tb = n_sub * tq
    n_parts = next(n for n in (4, 2, 1) if NCP % (n * LANE) == 0 and S % (n * tb) == 0)
    steps = S // (n_parts * tb)
    outs = [jnp.zeros((B, S, G * R * LANE), BF16), jnp.zeros((B, S, G * NB), BF16)]
    for part in range(n_parts):
        ncp_w = NCP * (part + 1) // n_parts
        nb_w = min(NB, -(-(S * (part + 1) // n_parts // NSA_SEL_BLOCK) // LANE) * LANE)
        ov = jnp.concatenate([overlap[:nb_w, :ncp_w], jnp.ones((1, ncp_w), BF16),
                              jnp.zeros((BF16_ROWS - 1, ncp_w), BF16)], axis=0)
        kern = functools.partial(_cmp_select_kernel, R=R, tq=tq, n_sub=n_sub, tile0=part * steps * n_sub,
                                 n_cmp=n_cmp, n_top=n_top)
        off = part * steps
        carried = list(outs)
        outs = pl.pallas_call(
            kern,
            grid=(B, G, steps),
            in_specs=[
                pl.BlockSpec((1, tb, R * LANE), lambda b, g, i, off=off: (b, i + off, g)),
                pl.BlockSpec((1, 1, 1, ncp_w, LANE), lambda b, g, i: (0, b, g, 0, 0)),
                pl.BlockSpec((1, 1, 1, ncp_w, LANE), lambda b, g, i: (1, b, g, 0, 0)),
                pl.BlockSpec((nb_w + BF16_ROWS, ncp_w), lambda b, g, i: (0, 0)),
                pl.BlockSpec((1, tb, LANE), lambda b, g, i, off=off: (b, i + off, gate_block0 + g)),
            ] + [pl.BlockSpec(memory_space=pl.ANY)] * len(carried),
            out_specs=[
                pl.BlockSpec((1, tb, R * LANE), lambda b, g, i, off=off: (b, i + off, g)),
                pl.BlockSpec((1, tb, NB), lambda b, g, i, off=off: (b, i + off, g)),
            ],
            out_shape=[
                jax.ShapeDtypeStruct((B, S, G * R * LANE), BF16),
                jax.ShapeDtypeStruct((B, S, G * NB), BF16),
            ],
            input_output_aliases={5 + k: k for k in range(len(carried))},
            compiler_params=_params("parallel", "parallel", "arbitrary"),
            name="cmp_attention_select",
        )(q, kv_cmp, kv_cmp, ov, gate, *carried)
    return outs


def _rope_tables(pos, dim):
    inv = 1.0 / (ROPE_THETA ** (jnp.arange(0, dim, 2, dtype=F32) / dim))
    ang = pos.astype(F32)[..., None] * inv
    pad = [(0, 0)] * (ang.ndim - 1) + [(0, (LANE - dim) // 2)]
    c, s = jnp.pad(jnp.cos(ang), pad), jnp.pad(jnp.sin(ang), pad)
    return jnp.concatenate([c, c], axis=-1), jnp.concatenate([-s, s], axis=-1)


def _spread_rope_cols(w):
    dim = w.shape[-1]
    pad = [(0, 0)] * (w.ndim - 1) + [(0, (LANE - dim) // 2)]
    return jnp.concatenate([jnp.pad(w[..., :dim // 2], pad), jnp.pad(w[..., dim // 2:], pad)], axis=-1)


def nsa_mixer(x, B, S, positions, norm_w, w_in, cmp_pe, cmp_w1, cmp_w2, w_out):
    T, D = x.shape
    H, G, R, dk = NSA_HEADS, NSA_KV_GROUPS, NSA_R, NSA_HEAD_DIM
    q_cols, kv_cols = H * dk, NSA_N_BRANCHES * 2 * G * dk
    grp = G * dk
    q_scale = dk ** -0.5 * LOG2E
    tables = _rope_tables(positions.reshape(T), dk)

    w_q = w_in[:, :q_cols]
    w_kv = w_in[:, q_cols:q_cols + kv_cols].reshape(D, NSA_N_BRANCHES, 2, grp)
    w_gate = w_in[:, q_cols + kv_cols:].reshape(D, G, R * NSA_N_BRANCHES)
    w_gate = jnp.pad(w_gate, ((0, 0), (0, 0), (0, LANE - R * NSA_N_BRANCHES))).reshape(D, G * LANE)
    tn = 2 * grp
    nq = q_cols // tn
    qv = proj(x, norm_w, jnp.concatenate([w_q, w_kv[:, 1, 1], w_kv[:, 2, 1]], axis=1).astype(BF16), n_blocks=nq + 1,
              tn=tn, rope=tables, rope_mask=(1 << nq) - 1, scale_mask=(1 << nq) - 1, scale=q_scale,
              out_dtype=BF16).reshape(B, S, q_cols + 2 * grp)
    k_t = proj(x, norm_w, jnp.concatenate([w_kv[:, 1, 0], w_kv[:, 2, 0]], axis=1).astype(BF16), n_blocks=1, tn=tn,
               rope=tables, rope_mask=1, transposed=True, out_dtype=BF16)
    w_cg = jnp.concatenate([w_kv[:, 0, 0], w_kv[:, 0, 1], w_gate], axis=1).astype(BF16)
    cmp_gates = proj(x, norm_w, w_cg, n_blocks=1, tn=w_cg.shape[1], sigmoid_from=2 * grp // LANE)
    cmp_gates = cmp_gates.reshape(B, S, w_cg.shape[1])
    gate0 = 2 * grp // LANE

    ncp = S // NSA_CMP_STRIDE
    n_cmp = (S - NSA_CMP_LEN) // NSA_CMP_STRIDE + 1
    cmp_end = jnp.minimum(jnp.arange(ncp) * NSA_CMP_STRIDE + NSA_CMP_LEN - 1, S - 1)
    c2c, s2c = _rope_tables(positions[:, cmp_end], dk)
    hidden = cmp_w1.shape[-1]
    kv_cmp = compress_tokens(cmp_gates, cmp_pe, cmp_w1.reshape(2, NSA_CMP_LEN, dk, hidden).astype(BF16),
                             cmp_w2.astype(BF16), c2c, s2c)

    n_sel = S // NSA_SEL_BLOCK
    nb = -(-n_sel // LANE) * LANE
    blk_start = jnp.arange(ncp) * NSA_CMP_STRIDE
    sel_start = jnp.arange(nb) * NSA_SEL_BLOCK
    overlap = ((blk_start[None, :] <= sel_start[:, None] + NSA_SEL_BLOCK - 1)
               & (blk_start[None, :] + NSA_CMP_LEN - 1 >= sel_start[:, None])
               & (jnp.arange(ncp)[None, :] < n_cmp) & (jnp.arange(nb)[:, None] < n_sel)).astype(BF16)
    o_cmp, neg_mask = cmp_attention_select(qv, kv_cmp, overlap, cmp_gates, gate_block0=gate0, n_cmp=n_cmp,
                                           n_top=min(NSA_N_SELECT, n_sel))

    blk_onehot = (jnp.arange(LANE)[:, None] == jnp.arange(S)[None, :] // NSA_SEL_BLOCK % LANE).astype(BF16)
    v0 = q_cols // dk
    o_slc = flash_attention(qv, neg_mask, k_t, blk_onehot, qv, cmp_gates, n_groups=G, R=R, E=nb // LANE,
                            qm_col=lambda g: g, qx_col=lambda g: g, km_col=lambda g: g,
                            v_col=lambda g: v0 + g, kx_batched=False,
                            gate_cols=tuple(NSA_N_BRANCHES * r + 1 for r in range(R)), gate_block0=gate0,
                            mode="causal", tile=NSA_FLASH_TILE, keys_per_version=LANE * NSA_SEL_BLOCK)
    o_win = flash_attention(qv, None, k_t, None, qv, cmp_gates, n_groups=G, R=R, E=0,
                            qm_col=lambda g: g, km_col=lambda g: G + g, v_col=lambda g: v0 + G + g,
                            kx_batched=False, gate_cols=tuple(NSA_N_BRANCHES * r + 2 for r in range(R)),
                            gate_block0=gate0, mode="window", window=NSA_WINDOW, tile=NSA_FLASH_TILE)
    return matmul_residual([o.reshape(T, H * dk) for o in (o_cmp, o_slc, o_win)], w_out.astype(BF16), x)


def mla_mixer(x, B, S, positions, norm_w, w_in, q_norm_w, kv_norm_w, w_uq, w_ukv, w_out):
    T, D = x.shape
    H = MLA_HEADS
    q_rank, kv_rank = q_norm_w.shape[0], kv_norm_w.shape[0]
    assert q_rank == kv_rank and q_rank % LANE == 0
    q_scale = MLA_QK_DIM ** -0.5 * LOG2E
    tables = _rope_tables(positions.reshape(T), MLA_ROPE_DIM)

    latent = proj(x, norm_w, w_in[:, :2 * q_rank].astype(BF16), n_blocks=1, tn=2 * q_rank)
    w_kr = _spread_rope_cols(w_in[:, 2 * q_rank:]).astype(BF16)
    k_rope_t = proj(x, norm_w, w_kr, n_blocks=1, tn=LANE, rope=tables, rope_mask=1, transposed=True, out_dtype=BF16)
    w_uq3 = w_uq.reshape(q_rank, H, MLA_QK_DIM)
    w_uq_p = jnp.concatenate([w_uq3[:, :, :MLA_NOPE_DIM].reshape(q_rank, H * MLA_NOPE_DIM),
                              _spread_rope_cols(w_uq3[:, :, MLA_NOPE_DIM:]).reshape(q_rank, H * LANE)],
                             axis=1).astype(BF16)
    tn = H * LANE
    n_nope = H * MLA_NOPE_DIM // tn
    n_all = n_nope + H * LANE // tn
    q = proj(latent, q_norm_w, w_uq_p, n_blocks=n_all, tn=tn, x_col_block=0, rope=tables,
             rope_mask=(1 << n_all) - (1 << n_nope), scale_mask=(1 << n_all) - 1, scale=q_scale,
             out_dtype=BF16).reshape(B, S, n_all * tn)
    w_ukv3 = w_ukv.reshape(kv_rank, H, MLA_NOPE_DIM + MLA_V_DIM)
    w_uk = w_ukv3[:, :, :MLA_NOPE_DIM].reshape(kv_rank, H * MLA_NOPE_DIM).astype(BF16)
    w_uv = w_ukv3[:, :, MLA_NOPE_DIM:].reshape(kv_rank, H * MLA_V_DIM).astype(BF16)
    k_nope_t = proj(latent, kv_norm_w, w_uk, n_blocks=H * MLA_NOPE_DIM // tn, tn=tn, x_col_block=1, transposed=True,
                    out_dtype=BF16)
    v = proj(latent, kv_norm_w, w_uv, n_blocks=H * MLA_V_DIM // tn, tn=tn, x_col_block=1,
             out_dtype=BF16).reshape(B, S, H * MLA_V_DIM)
    o = flash_attention(q, q, k_nope_t, k_rope_t, v, n_groups=H, R=1, E=1,
                        qm_col=lambda h: h, qx_col=lambda h: H + h, km_col=lambda h: h, v_col=lambda h: h,
                        kx_batched=True, mode="causal", tile=MLA_FLASH_TILE)
    return matmul_residual([o.reshape(T, H * MLA_V_DIM)], w_out.astype(BF16), x)


def kernel(x, positions, ffn_norm_w, ffn_w_in, ffn_w_out, mix_norm_w, nsa_w_in, nsa_cmp_pe, nsa_cmp_w1, nsa_cmp_w2,
           nsa_w_out, mla_w_in, mla_q_norm_w, mla_kv_norm_w, mla_w_uq, mla_w_ukv, mla_w_out, final_norm_w):
    B, S, D = x.shape
    depth = ffn_norm_w.shape[0]
    n_mixers = 2
    h = x.reshape(B * S, D)
    for i in range(depth):
        h = ffn_half_step(h, ffn_norm_w[i, 0], ffn_w_in[i, 0].astype(BF16), ffn_w_out[i, 0].astype(BF16))
        j = i // n_mixers
        if i % n_mixers == 0:
            h = nsa_mixer(h, B, S, positions, mix_norm_w[i], nsa_w_in[j], nsa_cmp_pe[j], nsa_cmp_w1[j],
                          nsa_cmp_w2[j], nsa_w_out[j])
        else:
            h = mla_mixer(h, B, S, positions, mix_norm_w[i], mla_w_in[j], mla_q_norm_w[j], mla_kv_norm_w[j],
                          mla_w_uq[j], mla_w_ukv[j], mla_w_out[j])
        last = i == depth - 1
        h = ffn_half_step(h, ffn_norm_w[i, 1], ffn_w_in[i, 1].astype(BF16), ffn_w_out[i, 1].astype(BF16),
                          final_w=final_norm_w if last else None)
    return h.reshape(B, S, D)
```
